```python
import jax, jax.numpy as jnp
from jax import lax
import numpy as np

D_MODEL = 1024
BATCH = 4
SEQ = 8192
DEPTH = 1

SC_WIDTH = 512
SC_GROUPS = 4
SC_KERNEL = 3
DN_HEADS = 8
DN_DK = 64
DN_DV = 64
DN_KEY = DN_HEADS * DN_DK
DN_VAL = DN_HEADS * DN_DV
DN_CONV = 4
DN_CHUNK = 64
PEER_HEADS = 8
PEER_NKEYS = 128
PEER_EXPERTS = PEER_NKEYS * PEER_NKEYS
PEER_QDIM = 256
PEER_HALF = PEER_QDIM // 2
PEER_TOPK = 16
PEER_BLOCK = 128
NORM_EPS = 1e-6

IN_SIZES = (SC_WIDTH, SC_WIDTH, SC_WIDTH, DN_KEY, DN_KEY, DN_VAL, DN_VAL,
            DN_HEADS, DN_HEADS, D_MODEL, D_MODEL)
IN_COLS = 3 * SC_WIDTH + 2 * DN_KEY + 2 * DN_VAL + 2 * DN_HEADS + 2 * D_MODEL

kernel_name = "hybrid_conv_deltanet_peer_block"


def rmsnorm(x, w):
    xf = x.astype(jnp.float32)
    y = xf * lax.rsqrt(jnp.mean(xf * xf, axis=-1, keepdims=True) + NORM_EPS)
    return (y * w.astype(jnp.float32)).astype(x.dtype)


def l2norm(x):
    return x * lax.rsqrt(jnp.sum(x * x, axis=-1, keepdims=True) + NORM_EPS)


def split_cols(t, sizes):
    outs, start = [], 0
    for s in sizes:
        outs.append(t[..., start:start + s])
        start += s
    return outs


def causal_dwconv(x, w):
    K = w.shape[0]
    T = x.shape[1]
    xp = jnp.pad(x, ((0, 0), (K - 1, 0), (0, 0)))
    acc = xp[:, 0:T] * w[0]
    for k in range(1, K):
        acc = acc + xp[:, k:k + T] * w[k]
    return acc


def gated_delta_rule(q, k, v, g, beta):
    B, T, H, dk = q.shape
    dv = v.shape[-1]
    C = DN_CHUNK
    n = T // C

    def to_chunks(t):
        return jnp.moveaxis(t.reshape((B, n, C) + t.shape[2:]), 3, 1)

    qc, kc, vc, gc, bc = (to_chunks(t) for t in (q, k, v, g, beta))
    G = jnp.cumsum(gc, axis=-1)
    incl = jnp.tril(jnp.ones((C, C), dtype=bool))
    strict = jnp.tril(jnp.ones((C, C), dtype=bool), -1)
    diff = G[..., :, None] - G[..., None, :]
    decay = jnp.where(incl, jnp.exp(jnp.where(incl, diff, 0.0)), 0.0)
    kb = kc * bc[..., None]
    vb = vc * bc[..., None]
    L = jnp.einsum('bhnid,bhnjd->bhnij', kb, kc) * jnp.where(strict, decay, 0.0)
    eye = jnp.eye(C, dtype=L.dtype)
    rhs = jnp.concatenate([vb, kb * jnp.exp(G)[..., None]], axis=-1)
    sol = lax.linalg.triangular_solve(eye + L, rhs, left_side=True, lower=True,
                                      unit_diagonal=True)
    u, w = sol[..., :dv], sol[..., dv:]
    a_qk = jnp.einsum('bhnid,bhnjd->bhnij', qc, kc) * decay
    qg = qc * jnp.exp(G)[..., None]
    kdec = kc * jnp.exp(G[..., -1:] - G)[..., None]
    glast = jnp.exp(G[..., -1])

    xs = tuple(jnp.moveaxis(t, 2, 0) for t in (u, w, qg, a_qk, kdec, glast))

    def step(S, inp):
        u_n, w_n, qg_n, a_n, kd_n, gl_n = inp
        v_new = u_n - jnp.einsum('bhck,bhkv->bhcv', w_n, S)
        o = jnp.einsum('bhck,bhkv->bhcv', qg_n, S) + jnp.einsum('bhij,bhjv->bhiv', a_n, v_new)
        S = S * gl_n[..., None, None] + jnp.einsum('bhck,bhcv->bhkv', kd_n, v_new)
        return S, o

    S0 = jnp.zeros((B, H, dk, dv), dtype=q.dtype)
    _, o = lax.scan(step, S0, xs)
    return o.transpose(1, 0, 3, 2, 4).reshape(B, T, H, dv)


def token_mixer(h, w_in, sc_conv_w, dn_conv_w, dn_a_log, dn_dt_bias, dn_out_norm,
                w_sc_out, w_dn_out, w_mix_out):
    B, T, _ = h.shape
    proj = h @ w_in
    (sc_b, sc_c, sc_x, dn_q, dn_k, dn_v, dn_z, dn_a, dn_b,
     gate_sc, gate_dn) = split_cols(proj, IN_SIZES)

    br_sc = (sc_b * causal_dwconv(sc_c * sc_x, sc_conv_w)) @ w_sc_out

    qkv = jax.nn.silu(causal_dwconv(jnp.concatenate([dn_q, dn_k, dn_v], axis=-1), dn_conv_w))
    q, k, v = split_cols(qkv.astype(jnp.float32), (DN_KEY, DN_KEY, DN_VAL))
    q = l2norm(q.reshape(B, T, DN_HEADS, DN_DK)) * (DN_DK ** -0.5)
    k = l2norm(k.reshape(B, T, DN_HEADS, DN_DK))
    v = v.reshape(B, T, DN_HEADS, DN_DV)
    beta = jax.nn.sigmoid(dn_b.astype(jnp.float32))
    g = -jnp.exp(dn_a_log.astype(jnp.float32)) * jax.nn.softplus(
        dn_a.astype(jnp.float32) + dn_dt_bias.astype(jnp.float32))
    o = gated_delta_rule(q, k, v, g, beta)
    z = dn_z.astype(jnp.float32).reshape(B, T, DN_HEADS, DN_DV)
    o = (rmsnorm(o, dn_out_norm) * jax.nn.silu(z)).astype(h.dtype).reshape(B, T, DN_VAL)
    br_dn = o @ w_dn_out

    merged = jax.nn.sigmoid(gate_sc) * br_sc + jax.nn.sigmoid(gate_dn) * br_dn
    return merged @ w_mix_out


def peer(h, w_q, sub_keys, u_tab, v_tab):
    B, T, D = h.shape
    K = PEER_TOPK
    q = (h @ w_q).reshape(B, T, PEER_HEADS, 2, PEER_HALF)
    s = jnp.einsum('bthpd,hpnd->bthpn', q, sub_keys).astype(jnp.float32)
    v1, i1 = lax.top_k(s[..., 0, :], K)
    v2, i2 = lax.top_k(s[..., 1, :], K)
    cand = (v1[..., :, None] + v2[..., None, :]).reshape(B, T, PEER_HEADS, K * K)
    cidx = (i1[..., :, None] * PEER_NKEYS + i2[..., None, :]).reshape(B, T, PEER_HEADS, K * K)
    top_s, pos = lax.top_k(cand, K)
    eidx = jnp.take_along_axis(cidx, pos, axis=-1)
    gates = jax.nn.softmax(top_s, axis=-1)
    E = PEER_HEADS * K
    eidx = eidx.reshape(B, T, E)
    gates = gates.reshape(B, T, E).astype(h.dtype)
    nb = T // PEER_BLOCK

    def blk(t):
        return jnp.moveaxis(t.reshape((B, nb, PEER_BLOCK) + t.shape[2:]), 1, 0)

    def expert_block(args):
        hb, ib, gb = args
        u = jnp.take(u_tab, ib, axis=0)
        act = jax.nn.gelu(jnp.einsum('btd,bted->bte', hb, u), approximate=False)
        vv = jnp.take(v_tab, ib, axis=0)
        return jnp.einsum('bte,bted->btd', gb * act, vv)

    out = lax.map(expert_block, (blk(h), blk(eidx), blk(gates)))
    return jnp.moveaxis(out, 0, 1).reshape(B, T, D)


def setup_inputs(seed: int = 0) -> dict:
    key = jax.random.key(seed)
    ks = jax.random.split(key, 24)
    D = D_MODEL
    nrm = jax.random.normal
    f32 = jnp.float32
    dt = jnp.exp(jax.random.uniform(ks[9], (DEPTH, DN_HEADS), f32,
                                    np.log(1e-3).astype(np.float32), np.log(1e-1).astype(np.float32)))
    return {
        "x": nrm(ks[0], (BATCH, SEQ, D), f32),
        "c": nrm(ks[1], (BATCH, D), f32),
        "w_ada": nrm(ks[2], (DEPTH, D, 6 * D), f32) * (0.5 * D ** -0.5),
        "b_ada": nrm(ks[3], (DEPTH, 6 * D), f32) * 0.01,
        "norm_pre_mix": 1.0 + 0.05 * nrm(ks[4], (DEPTH, D), f32),
        "norm_post_mix": 1.0 + 0.05 * nrm(ks[5], (DEPTH, D), f32),
        "w_in_mix": nrm(ks[6], (DEPTH, D, IN_COLS), f32) * D ** -0.5,
        "sc_conv_w": nrm(ks[7], (DEPTH, SC_KERNEL, SC_WIDTH), f32) * SC_KERNEL ** -0.5,
        "dn_conv_w": nrm(ks[8], (DEPTH, DN_CONV, 2 * DN_KEY + DN_VAL), f32) * DN_CONV ** -0.5,
        "dn_a_log": jnp.log(jax.random.uniform(ks[10], (DEPTH, DN_HEADS), f32, 1.0, 16.0)),
        "dn_dt_bias": dt + jnp.log(-jnp.expm1(-dt)),
        "dn_out_norm": 1.0 + 0.05 * nrm(ks[11], (DEPTH, DN_DV), f32),
        "w_sc_out": nrm(ks[12], (DEPTH, SC_WIDTH, D), f32) * SC_WIDTH ** -0.5,
        "w_dn_out": nrm(ks[13], (DEPTH, DN_VAL, D), f32) * DN_VAL ** -0.5,
        "w_mix_out": nrm(ks[14], (DEPTH, D, D), f32) * D ** -0.5,
        "norm_pre_ffn": 1.0 + 0.05 * nrm(ks[15], (DEPTH, D), f32),
        "norm_post_ffn": 1.0 + 0.05 * nrm(ks[16], (DEPTH, D), f32),
        "peer_w_q": nrm(ks[17], (DEPTH, D, PEER_HEADS * PEER_QDIM), f32) * D ** -0.5,
        "peer_sub_keys": nrm(ks[18], (DEPTH, PEER_HEADS, 2, PEER_NKEYS, PEER_HALF), f32) * PEER_HALF ** -0.5,
        "peer_u": nrm(ks[19], (PEER_EXPERTS, D), f32) * D ** -0.5,
        "peer_v": nrm(ks[20], (PEER_EXPERTS, D), f32) * (PEER_HEADS * PEER_TOPK) ** -0.5,
    }


def reference(x, c, w_ada, b_ada, norm_pre_mix, norm_post_mix, w_in_mix, sc_conv_w,
              dn_conv_w, dn_a_log, dn_dt_bias, dn_out_norm, w_sc_out, w_dn_out, w_mix_out,
              norm_pre_ffn, norm_post_ffn, peer_w_q, peer_sub_keys, peer_u, peer_v):
    for l in range(DEPTH):
        mod = jax.nn.silu(c) @ w_ada[l] + b_ada[l]
        sh1, sc1, gt1, sh2, sc2, gt2 = jnp.split(mod[:, None, :], 6, axis=-1)
        h = rmsnorm(x, norm_pre_mix[l]) * (1.0 + sc1) + sh1
        y = token_mixer(h, w_in_mix[l], sc_conv_w[l], dn_conv_w[l], dn_a_log[l],
                        dn_dt_bias[l], dn_out_norm[l], w_sc_out[l], w_dn_out[l], w_mix_out[l])
        x = x + gt1 * rmsnorm(y, norm_post_mix[l])
        h = rmsnorm(x, norm_pre_ffn[l]) * (1.0 + sc2) + sh2
        y = peer(h, peer_w_q[l], peer_sub_keys[l], peer_u, peer_v)
        x = x + gt2 * rmsnorm(y, norm_post_ffn[l])
    return x
```

```python
import functools

import jax
import jax.numpy as jnp
from jax import lax
from jax.experimental import pallas as pl
from jax.experimental.pallas import tpu as pltpu

F32 = jnp.float32
BF16 = jnp.bfloat16
I32 = jnp.int32
U32 = jnp.uint32

D_MODEL = 1024
SC_WIDTH = 512
SC_KERNEL = 3
DN_HEADS = 8
DN_DK = 64
DN_KEY = DN_HEADS * DN_DK
DN_VAL = DN_KEY
DN_CONV = 4
DN_CHUNK = 64
PEER_HEADS = 8
PEER_NKEYS = 128
PEER_HALF = 128
PEER_QDIM = 256
PEER_TOPK = 16
PEER_E = PEER_HEADS * PEER_TOPK
NORM_EPS = 1e-6

LANES = 128
SUBLANES = 8
HALO = SUBLANES
PAIR = 2 * DN_DK
N_PAIRS = DN_HEADS // 2
VMEM_TABLE_LIMIT = 56 * 1024 * 1024
VMEM_MIXER_LIMIT = 56 * 1024 * 1024

C_SC = 3 * SC_WIDTH
C_QKV = 2 * DN_KEY + DN_VAL
C_Z = DN_VAL
C_A = DN_KEY
C_B = DN_KEY
C_G = 2 * D_MODEL
C_ALL = C_SC + C_QKV + C_Z + C_A + C_B + C_G


def _silu(x):
    return x * jax.nn.sigmoid(x)


def _softplus(x):
    return jnp.maximum(x, 0.0) + jnp.log(1.0 + jnp.exp(-jnp.abs(x)))


def _gelu(x):
    return 0.5 * x * (1.0 + lax.erf(x * (2.0 ** -0.5)))


def _rms(x, w):
    return x * lax.rsqrt(jnp.mean(x * x, axis=-1, keepdims=True) + NORM_EPS) * w


def _bdot(a, b):
    return jnp.dot(a.astype(BF16), b.astype(BF16), preferred_element_type=F32)


def _xdot(a, b):
    return jnp.dot(a, b, preferred_element_type=F32, precision=lax.Precision.HIGHEST)


def _ada_kernel(c_ref, w_ref, b_ref, o_ref):
    o_ref[...] = _bdot(_silu(c_ref[...]), w_ref[...]) + b_ref[...]


def _ada(c_pad, w_ada, b_ada):
    n_out = w_ada.shape[1]
    tn = 1024
    return pl.pallas_call(
        _ada_kernel,
        grid=(n_out // tn,),
        in_specs=[
            pl.BlockSpec((SUBLANES, D_MODEL), lambda j: (0, 0)),
            pl.BlockSpec((D_MODEL, tn), lambda j: (0, j)),
            pl.BlockSpec((1, tn), lambda j: (0, j)),
        ],
        out_specs=pl.BlockSpec((SUBLANES, tn), lambda j: (0, j)),
        out_shape=jax.ShapeDtypeStruct((SUBLANES, n_out), F32),
        name="ada",
    )(c_pad, w_ada, b_ada)


def _inproj_kernel(x_ref, mod_ref, nw_ref, w_ref, sc_ref, qkv_ref, z_ref, a_ref, b_ref, g_ref):
    b = pl.program_id(0)
    mod = mod_ref[pl.ds(b, 1), :]
    sh1 = mod[:, 0:D_MODEL]
    sc1 = mod[:, D_MODEL:2 * D_MODEL]
    h = (_rms(x_ref[...], nw_ref[...]) * (1.0 + sc1) + sh1).astype(BF16)
    col = 0
    for ref, width in ((sc_ref, C_SC), (qkv_ref, C_QKV), (z_ref, C_Z), (a_ref, C_A), (b_ref, C_B), (g_ref, C_G)):
        for j in range(0, width, 512):
            ref[:, j:j + 512] = jnp.dot(h, w_ref[:, col + j:col + j + 512],
                                        preferred_element_type=F32).astype(ref.dtype)
        col += width


def _inproj(x2, mod, nw, w_pad, batch, seq, tm):
    nt = seq // tm
    row = lambda b, t: (b * nt + t, 0)
    const = lambda b, t: (0, 0)
    n = batch * seq
    outs = [(C_SC, BF16), (C_QKV, BF16), (C_Z, BF16), (C_A, F32), (C_B, F32), (C_G, BF16)]
    return pl.pallas_call(
        _inproj_kernel,
        grid=(batch, nt),
        in_specs=[
            pl.BlockSpec((tm, D_MODEL), row),
            pl.BlockSpec(mod.shape, const),
            pl.BlockSpec((1, D_MODEL), const),
            pl.BlockSpec((D_MODEL, C_ALL), const, pipeline_mode=pl.Buffered(1)),
        ],
        out_specs=[pl.BlockSpec((tm, w), row) for w, _ in outs],
        out_shape=[jax.ShapeDtypeStruct((n, w), dt) for w, dt in outs],
        compiler_params=pltpu.CompilerParams(
            dimension_semantics=("arbitrary", "arbitrary"), vmem_limit_bytes=VMEM_MIXER_LIMIT),
        name="inproj",
    )(x2, mod, nw, w_pad)


def _pair_consts():
    r = lax.broadcasted_iota(I32, (PAIR, PAIR), 0)
    c = lax.broadcasted_iota(I32, (PAIR, PAIR), 1)
    bd = ((r >> 6) == (c >> 6)).astype(F32)
    i = lax.broadcasted_iota(I32, (DN_CHUNK, PAIR), 0)
    j = lax.broadcasted_iota(I32, (DN_CHUNK, PAIR), 1) & (DN_DK - 1)
    return bd, i >= j, i > j, (i == j).astype(F32)


def _stack_bd(y, bd):
    return jnp.concatenate([y, y], axis=0) * bd


def _mixer_kernel(sc_ref, qkv_ref, z_ref, a_ref, b_ref, g_ref, x_ref, mod_ref,
                  scw_ref, dnw_ref, alog_ref, dtb_ref, onw_ref, wsc_ref, wdn_ref, wmix_ref, pnw_ref,
                  o_ref,
                  scx_s, qkvx_s, q_s, k_s, v_s, g_s, beta_s, o_s, state_s, *, tt):
    b = pl.program_id(0)
    t = pl.program_id(1)

    @pl.when(t == 0)
    def _():
        scx_s[0:HALO, :] = jnp.zeros((HALO, SC_WIDTH), F32)
        qkvx_s[0:HALO, :] = jnp.zeros((HALO, C_QKV), F32)
        state_s[...] = jnp.zeros(state_s.shape, F32)

    sc = sc_ref[...].astype(F32)
    scx_s[HALO:HALO + tt, :] = sc[:, SC_WIDTH:2 * SC_WIDTH] * sc[:, 2 * SC_WIDTH:]
    conv = scw_ref[0:1, :] * scx_s[pl.ds(HALO - (SC_KERNEL - 1), tt), :]
    for kk in range(1, SC_KERNEL):
        conv = conv + scw_ref[kk:kk + 1, :] * scx_s[pl.ds(HALO - (SC_KERNEL - 1) + kk, tt), :]
    y_sc = _bdot(sc[:, 0:SC_WIDTH] * conv, wsc_ref[...])
    scx_s[0:HALO, :] = scx_s[tt:tt + HALO, :]

    qkvx_s[HALO:HALO + tt, :] = qkv_ref[...].astype(F32)
    cq = dnw_ref[0:1, :] * qkvx_s[pl.ds(HALO - (DN_CONV - 1), tt), :]
    for kk in range(1, DN_CONV):
        cq = cq + dnw_ref[kk:kk + 1, :] * qkvx_s[pl.ds(HALO - (DN_CONV - 1) + kk, tt), :]
    qkvx_s[0:HALO, :] = qkvx_s[tt:tt + HALO, :]
    cq = _silu(cq)
    r512 = lax.broadcasted_iota(I32, (DN_KEY, DN_KEY), 0) >> 6
    c512 = lax.broadcasted_iota(I32, (DN_KEY, DN_KEY), 1) >> 6
    head_ones = (r512 == c512).astype(F32)
    q = cq[:, 0:DN_KEY]
    k = cq[:, DN_KEY:2 * DN_KEY]
    q_s[...] = q * lax.rsqrt(_xdot(q * q, head_ones) + NORM_EPS) * (DN_DK ** -0.5)
    k_s[...] = k * lax.rsqrt(_xdot(k * k, head_ones) + NORM_EPS)
    v_s[...] = cq[:, 2 * DN_KEY:]
    beta_s[...] = jax.nn.sigmoid(b_ref[...])
    g_s[...] = -jnp.exp(alog_ref[...]) * _softplus(a_ref[...] + dtb_ref[...])

    bd, incl, strict, eye = _pair_consts()
    ri = lax.broadcasted_iota(I32, (DN_CHUNK, DN_CHUNK), 0)
    ci = lax.broadcasted_iota(I32, (DN_CHUNK, DN_CHUNK), 1)
    lt = (ci <= ri).astype(F32)

    def chunk(c, carry):
        rows = pl.ds(pl.multiple_of(c * DN_CHUNK, DN_CHUNK), DN_CHUNK)
        gcum_all = _xdot(lt, g_s[rows, :])
        for p in range(N_PAIRS):
            sl = slice(p * PAIR, (p + 1) * PAIR)
            gc = gcum_all[:, sl]
            qp = q_s[rows, sl]
            kp = k_s[rows, sl]
            vp = v_s[rows, sl]
            bp = beta_s[rows, sl]
            grow = jnp.sum(gc * eye, axis=0, keepdims=True)
            diff = gc - grow
            dec = jnp.where(incl, jnp.exp(jnp.where(incl, diff, 0.0)), 0.0)
            eg = jnp.exp(gc)
            glast = gc[DN_CHUNK - 1:DN_CHUNK, :]
            kb = kp * bp
            kbig = _stack_bd(kp, bd).astype(BF16)
            nt = (((1,), (1,)), ((), ()))
            kk_s = lax.dot_general(kb.astype(BF16), kbig, nt, preferred_element_type=F32)
            qk_s = lax.dot_general(qp.astype(BF16), kbig, nt, preferred_element_type=F32)
            lmat = kk_s * jnp.where(strict, dec, 0.0)
            aqk = qk_s * dec
            pw = -lmat
            inv = eye + pw
            for _ in range(5):
                pw = _xdot(pw, _stack_bd(pw, bd))
                inv = inv + _xdot(inv, _stack_bd(pw, bd))
            u = _bdot(inv, _stack_bd(vp * bp, bd))
            w = _bdot(inv, _stack_bd(kb * eg, bd))
            st = state_s[p]
            ws = _bdot(jnp.concatenate([w, qp * eg], axis=0), st)
            vnew = u - ws[0:DN_CHUNK]
            o_s[rows, sl] = ws[DN_CHUNK:] + _bdot(aqk, _stack_bd(vnew, bd))
            kdec = kp * jnp.exp(glast - gc)
            state_s[p] = st * jnp.exp(glast) + bd * _bdot(kdec.T, vnew)
        return carry

    lax.fori_loop(0, tt // DN_CHUNK, chunk, 0)

    o = o_s[...]
    ms = _xdot(o * o, head_ones) * (1.0 / DN_DK)
    og = o * lax.rsqrt(ms + NORM_EPS) * onw_ref[...] * _silu(z_ref[...].astype(F32))
    y_dn = _bdot(og, wdn_ref[...])
    gates = g_ref[...].astype(F32)
    merged = jax.nn.sigmoid(gates[:, 0:D_MODEL]) * y_sc + jax.nn.sigmoid(gates[:, D_MODEL:]) * y_dn
    y = _bdot(merged, wmix_ref[...])
    gt1 = mod_ref[pl.ds(b, 1), :][:, 2 * D_MODEL:3 * D_MODEL]
    o_ref[...] = x_ref[...] + gt1 * _rms(y, pnw_ref[...])


def _mixer(parts, x2, mod, weights, batch, seq, tt):
    nt = seq // tt
    row = lambda b, t: (b * nt + t, 0)
    const2 = lambda b, t: (0, 0)
    sc, qkv, z, a, bb, g = parts
    data = [(sc, C_SC), (qkv, C_QKV), (z, C_Z), (a, C_A), (bb, C_B), (g, C_G), (x2, D_MODEL)]
    in_specs = [pl.BlockSpec((tt, w), row) for _, w in data]
    in_specs.append(pl.BlockSpec(mod.shape, const2))
    in_specs += [pl.BlockSpec(wt.shape, const2) for wt in weights]
    scratch = [
        pltpu.VMEM((tt + HALO, SC_WIDTH), F32),
        pltpu.VMEM((tt + HALO, C_QKV), F32),
        pltpu.VMEM((tt, DN_KEY), F32),
        pltpu.VMEM((tt, DN_KEY), F32),
        pltpu.VMEM((tt, DN_VAL), F32),
        pltpu.VMEM((tt, DN_KEY), F32),
        pltpu.VMEM((tt, DN_KEY), F32),
        pltpu.VMEM((tt, DN_VAL), F32),
        pltpu.VMEM((N_PAIRS, PAIR, PAIR), F32),
    ]
    return pl.pallas_call(
        functools.partial(_mixer_kernel, tt=tt),
        grid=(batch, nt),
        in_specs=in_specs,
        out_specs=pl.BlockSpec((tt, D_MODEL), row),
        out_shape=jax.ShapeDtypeStruct((batch * seq, D_MODEL), F32),
        scratch_shapes=scratch,
        compiler_params=pltpu.CompilerParams(
            dimension_semantics=("arbitrary", "arbitrary"), vmem_limit_bytes=VMEM_MIXER_LIMIT),
        name="mixer",
    )(*[d for d, _ in data], mod, *weights)


def _qproj_kernel(x_ref, mod_ref, nw_ref, w_ref, h_ref, q_ref):
    b = pl.program_id(0)
    mod = mod_ref[pl.ds(b, 1), :]
    sh2 = mod[:, 3 * D_MODEL:4 * D_MODEL]
    sc2 = mod[:, 4 * D_MODEL:5 * D_MODEL]
    h = _rms(x_ref[...], nw_ref[...]) * (1.0 + sc2) + sh2
    h_ref[...] = h
    q_ref[...] = _bdot(h, w_ref[...])


def _qproj(x1, mod, nw, wq, batch, seq, tm):
    nt = seq // tm
    row = lambda b, t: (b * nt + t, 0)
    const = lambda b, t: (0, 0)
    n = batch * seq
    nq = wq.shape[1]
    return pl.pallas_call(
        _qproj_kernel,
        grid=(batch, nt),
        in_specs=[
            pl.BlockSpec((tm, D_MODEL), row),
            pl.BlockSpec(mod.shape, const),
            pl.BlockSpec((1, D_MODEL), const),
            pl.BlockSpec(wq.shape, const),
        ],
        out_specs=[pl.BlockSpec((tm, D_MODEL), row), pl.BlockSpec((tm, nq), row)],
        out_shape=[jax.ShapeDtypeStruct((n, D_MODEL), F32), jax.ShapeDtypeStruct((n, nq), F32)],
        compiler_params=pltpu.CompilerParams(dimension_semantics=("arbitrary", "arbitrary")),
        name="qproj",
    )(x1, mod, nw, wq)


def _topk_rows(vals, idx_payload, n_rows, k, v_out, i_out, row0):
    pos_iota = lax.broadcasted_iota(I32, vals.shape, 0)
    for it in range(k):
        m = jnp.max(vals, axis=0, keepdims=True)
        pos = jnp.min(jnp.where(vals == m, pos_iota, n_rows), axis=0, keepdims=True)
        hit = pos_iota == pos
        v_out[row0 + it:row0 + it + 1, :] = m
        i_out[row0 + it:row0 + it + 1, :] = jnp.sum(jnp.where(hit, idx_payload, 0), axis=0, keepdims=True)
        vals = jnp.where(hit, -jnp.inf, vals)


def _topk_kernel(q_ref, keys_ref, r_ref, s_ref, g_ref, v1_s, i1_s, v2_s, i2_s, ts_s, ei_s, gate_s, *, tt):
    key_iota = lax.broadcasted_iota(I32, (PEER_NKEYS, tt), 0)
    nt = (((1,), (1,)), ((), ()))
    k = PEER_TOPK
    for h in range(PEER_HEADS):
        for p, (v_s, i_s) in enumerate(((v1_s, i1_s), (v2_s, i2_s))):
            col = (2 * h + p) * PEER_HALF
            st = lax.dot_general(keys_ref[2 * h + p], q_ref[:, col:col + PEER_HALF], nt,
                                 preferred_element_type=F32, precision=lax.Precision.HIGHEST)
            _topk_rows(st, key_iota, PEER_NKEYS, k, v_s, i_s, 0)
        v2 = v2_s[...]
        i2 = i2_s[...]
        cv = jnp.concatenate([v1_s[a:a + 1, :] + v2 for a in range(k)], axis=0)
        ce = jnp.concatenate([i1_s[a:a + 1, :] * PEER_NKEYS + i2 for a in range(k)], axis=0)
        _topk_rows(cv, ce, k * k, k, ts_s, ei_s, h * k)
        ts = ts_s[h * k:(h + 1) * k, :]
        e = jnp.exp(ts - jnp.max(ts, axis=0, keepdims=True))
        gate_s[h * k:(h + 1) * k, :] = e / jnp.sum(e, axis=0, keepdims=True)
    eidx = ei_s[...].T
    r_ref[...] = eidx >> 1
    s_ref[...] = (1 - (eidx & 1)) * 16
    g_ref[...] = gate_s[...].T


def _topk(q, keys, n, tt):
    row = lambda i: (i, 0)
    return pl.pallas_call(
        functools.partial(_topk_kernel, tt=tt),
        grid=(n // tt,),
        in_specs=[
            pl.BlockSpec((tt, q.shape[1]), row),
            pl.BlockSpec(keys.shape, lambda i: (0, 0, 0)),
        ],
        out_specs=[pl.BlockSpec((tt, PEER_E), row)] * 3,
        out_shape=[jax.ShapeDtypeStruct((n, PEER_E), I32), jax.ShapeDtypeStruct((n, PEER_E), I32),
                   jax.ShapeDtypeStruct((n, PEER_E), F32)],
        scratch_shapes=[
            pltpu.VMEM((PEER_TOPK, tt), F32), pltpu.VMEM((PEER_TOPK, tt), I32),
            pltpu.VMEM((PEER_TOPK, tt), F32), pltpu.VMEM((PEER_TOPK, tt), I32),
            pltpu.VMEM((PEER_E, tt), F32), pltpu.VMEM((PEER_E, tt), I32),
            pltpu.VMEM((PEER_E, tt), F32),
        ],
        compiler_params=pltpu.CompilerParams(dimension_semantics=("arbitrary",)),
        name="topk",
    )(q, keys)


def _unpack_row(tab_ref, r, s):
    w = tab_ref[r]
    return pltpu.bitcast((w << s.astype(U32)) & jnp.uint32(0xFFFF0000), F32)


def _peer_u_kernel(r_ref, s_ref, h_ref, g_ref, tab_ref, o_ref, p_s, *, tt):
    rs = lax.broadcasted_iota(I32, (PEER_E, PEER_E * SUBLANES), 0)
    cs = lax.broadcasted_iota(I32, (PEER_E, PEER_E * SUBLANES), 1) >> 3
    sel = (rs == cs).astype(BF16)
    ones = jnp.ones((SUBLANES, LANES), F32)
    nt = (((1,), (1,)), ((), ()))

    def tok(t, carry):
        hv = h_ref[t]

        def grp(gi, c2):
            for j in range(SUBLANES):
                e = gi * SUBLANES + j
                x = _unpack_row(tab_ref, r_ref[t, e], s_ref[t, e])
                p_s[pl.ds(pl.multiple_of(e * SUBLANES, SUBLANES), SUBLANES), :] = x * hv
            return c2

        lax.fori_loop(0, PEER_E // SUBLANES, grp, 0)
        part = jnp.dot(sel, p_s[...].astype(BF16), preferred_element_type=F32)
        act = lax.dot_general(ones, part, nt, preferred_element_type=F32,
                              precision=lax.Precision.HIGHEST)[0:1, :]
        o_ref[pl.ds(t, 1), :] = g_ref[pl.ds(t, 1), :] * _gelu(act)
        return carry

    lax.fori_loop(0, tt, tok, 0)


def _peer_u(r, s, h3, gates, tab, n, tt):
    row = lambda i: (i, 0)
    smem = pltpu.SMEM
    return pl.pallas_call(
        functools.partial(_peer_u_kernel, tt=tt),
        grid=(n // tt,),
        in_specs=[
            pl.BlockSpec((tt, PEER_E), row, memory_space=smem),
            pl.BlockSpec((tt, PEER_E), row, memory_space=smem),
            pl.BlockSpec((tt, SUBLANES, LANES), lambda i: (i, 0, 0)),
            pl.BlockSpec((tt, PEER_E), row),
            pl.BlockSpec(tab.shape, lambda i: (0, 0, 0), pipeline_mode=pl.Buffered(1)),
        ],
        out_specs=pl.BlockSpec((tt, PEER_E), row),
        out_shape=jax.ShapeDtypeStruct((n, PEER_E), F32),
        scratch_shapes=[pltpu.VMEM((PEER_E * SUBLANES, LANES), F32)],
        compiler_params=pltpu.CompilerParams(
            dimension_semantics=("arbitrary",), vmem_limit_bytes=VMEM_TABLE_LIMIT),
        name="peer_u",
    )(r, s, h3, gates, tab)


def _peer_v_kernel(r_ref, s_ref, a_ref, tab_ref, o_ref, *, tt):
    def tok(t, carry):
        def grp(gi, accs):
            accs = list(accs)
            for j in range(SUBLANES):
                e = gi * SUBLANES + j
                x = _unpack_row(tab_ref, r_ref[t, e], s_ref[t, e])
                accs[j % 2] = accs[j % 2] + a_ref[t, e] * x
            return tuple(accs)

        zero = jnp.zeros((SUBLANES, LANES), F32)
        a0, a1 = lax.fori_loop(0, PEER_E // SUBLANES, grp, (zero, zero))
        o_ref[t] = a0 + a1
        return carry

    lax.fori_loop(0, tt, tok, 0)


def _peer_v(r, s, act, tab, n, tt):
    row = lambda i: (i, 0)
    smem = pltpu.SMEM
    return pl.pallas_call(
        functools.partial(_peer_v_kernel, tt=tt),
        grid=(n // tt,),
        in_specs=[
            pl.BlockSpec((tt, PEER_E), row, memory_space=smem),
            pl.BlockSpec((tt, PEER_E), row, memory_space=smem),
            pl.BlockSpec((tt, PEER_E), row, memory_space=smem),
            pl.BlockSpec(tab.shape, lambda i: (0, 0, 0), pipeline_mode=pl.Buffered(1)),
        ],
        out_specs=pl.BlockSpec((tt, SUBLANES, LANES), lambda i: (i, 0, 0)),
        out_shape=jax.ShapeDtypeStruct((n, SUBLANES, LANES), F32),
        compiler_params=pltpu.CompilerParams(
            dimension_semantics=("arbitrary",), vmem_limit_bytes=VMEM_TABLE_LIMIT),
        name="peer_v",
    )(r, s, act, tab)


def _final_kernel(x_ref, y_ref, mod_ref, nw_ref, o_ref):
    b = pl.program_id(0)
    gt2 = mod_ref[pl.ds(b, 1), :][:, 5 * D_MODEL:6 * D_MODEL]
    o_ref[...] = x_ref[...] + gt2 * _rms(y_ref[...], nw_ref[...])


def _final(x1, y2, mod, nw, batch, seq, tm):
    nt = seq // tm
    row = lambda b, t: (b * nt + t, 0)
    const = lambda b, t: (0, 0)
    return pl.pallas_call(
        _final_kernel,
        grid=(batch, nt),
        in_specs=[
            pl.BlockSpec((tm, D_MODEL), row),
            pl.BlockSpec((tm, D_MODEL), row),
            pl.BlockSpec(mod.shape, const),
            pl.BlockSpec((1, D_MODEL), const),
        ],
        out_specs=pl.BlockSpec((tm, D_MODEL), row),
        out_shape=jax.ShapeDtypeStruct((batch * seq, D_MODEL), F32),
        compiler_params=pltpu.CompilerParams(dimension_semantics=("arbitrary", "arbitrary")),
        name="final",
    )(x1, y2, mod, nw)


def _pack_table(tab):
    e, d = tab.shape
    bits = lax.bitcast_convert_type(tab.astype(BF16), jnp.uint16).astype(U32).reshape(e // 2, 2, d)
    return (bits[:, 0, :] | (bits[:, 1, :] << 16)).reshape(e // 2, d // LANES, LANES)


def _tile(seq, cap):
    t = min(seq, cap)
    assert seq % t == 0
    return t


def kernel(x, c, w_ada, b_ada, norm_pre_mix, norm_post_mix, w_in_mix, sc_conv_w, dn_conv_w, dn_a_log,
           dn_dt_bias, dn_out_norm, w_sc_out, w_dn_out, w_mix_out, norm_pre_ffn, norm_post_ffn, peer_w_q,
           peer_sub_keys, peer_u, peer_v):
    batch, seq, d = x.shape
    assert d == D_MODEL and batch <= SUBLANES and seq % DN_CHUNK == 0
    depth = w_ada.shape[0]
    n = batch * seq
    x2 = x.reshape(n, d)
    c_pad = jnp.pad(c, ((0, SUBLANES - batch), (0, 0)))
    tab_u = _pack_table(peer_u)
    tab_v = _pack_table(peer_v)
    rep = lambda v: jnp.repeat(v, DN_DK, axis=-1)
    for l in range(depth):
        mod = _ada(c_pad, w_ada[l], b_ada[l][None, :])
        w = w_in_mix[l]
        o_z = C_SC + C_QKV
        o_a = o_z + C_Z
        w_pad = jnp.concatenate(
            [w[:, :o_a], rep(w[:, o_a:o_a + DN_HEADS]), rep(w[:, o_a + DN_HEADS:o_a + 2 * DN_HEADS]),
             w[:, o_a + 2 * DN_HEADS:]], axis=1).astype(BF16)
        tm = _tile(seq, 512)
        parts = _inproj(x2, mod, norm_pre_mix[l][None, :], w_pad, batch, seq, tm)
        weights = [
            sc_conv_w[l], dn_conv_w[l], rep(dn_a_log[l])[None, :], rep(dn_dt_bias[l])[None, :],
            jnp.tile(dn_out_norm[l], DN_HEADS)[None, :], w_sc_out[l].astype(BF16), w_dn_out[l].astype(BF16),
            w_mix_out[l].astype(BF16), norm_post_mix[l][None, :],
        ]
        x1 = _mixer(parts, x2, mod, weights, batch, seq, _tile(seq, 512))
        h2, q = _qproj(x1, mod, norm_pre_ffn[l][None, :], peer_w_q[l].astype(BF16), batch, seq, tm)
        keys = peer_sub_keys[l].reshape(2 * PEER_HEADS, PEER_NKEYS, PEER_HALF)
        r, s, gates = _topk(q, keys, n, _tile(n, 256))
        tt = _tile(n, 64)
        act = _peer_u(r, s, h2.reshape(n, SUBLANES, LANES), gates, tab_u, n, tt)
        y2 = _peer_v(r, s, act, tab_v, n, tt).reshape(n, d)
        x2 = _final(x1, y2, mod, norm_post_ffn[l][None, :], batch, seq, tm)
    return x2.reshape(batch, seq, d)
```

```python
import functools

import jax
import jax.numpy as jnp
from jax import lax
from jax.experimental import pallas as pl
from jax.experimental.pallas import tpu as pltpu

F32 = jnp.float32
BF16 = jnp.bfloat16
I32 = jnp.int32
U32 = jnp.uint32

D_MODEL = 1024
SC_WIDTH = 512
SC_KERNEL = 3
DN_HEADS = 8
DN_DK = 64
DN_KEY = DN_HEADS * DN_DK
DN_VAL = DN_KEY
DN_CONV = 4
DN_CHUNK = 64
PEER_HEADS = 8
PEER_NKEYS = 128
PEER_HALF = 128
PEER_QDIM = 256
PEER_TOPK = 16
PEER_E = PEER_HEADS * PEER_TOPK
NORM_EPS = 1e-6

LANES = 128
SUBLANES = 8
HALO = SUBLANES
PAIR = 2 * DN_DK
N_PAIRS = DN_HEADS // 2
HALF_ROWS = SUBLANES // 2
GROUP = SUBLANES
GROUP_ROWS = HALF_ROWS * GROUP
N_GROUPS = PEER_E // GROUP
N_SCR = 4
VMEM_TABLE_LIMIT = 56 * 1024 * 1024
VMEM_MIXER_LIMIT = 56 * 1024 * 1024

C_SC = 3 * SC_WIDTH
C_QKV = 2 * DN_KEY + DN_VAL
C_Z = DN_VAL
C_A = DN_KEY
C_B = DN_KEY
C_G = 2 * D_MODEL
C_ALL = C_SC + C_QKV + C_Z + C_A + C_B + C_G


def _silu(x):
    return x * jax.nn.sigmoid(x)


def _softplus(x):
    return jnp.maximum(x, 0.0) + jnp.log(1.0 + jnp.exp(-jnp.abs(x)))


def _gelu(x):
    return 0.5 * x * (1.0 + lax.erf(x * (2.0 ** -0.5)))


def _rms(x, w):
    return x * lax.rsqrt(jnp.mean(x * x, axis=-1, keepdims=True) + NORM_EPS) * w


def _bdot(a, b):
    return jnp.dot(a.astype(BF16), b.astype(BF16), preferred_element_type=F32)


def _xdot(a, b):
    return jnp.dot(a, b, preferred_element_type=F32, precision=lax.Precision.HIGHEST)


def _ada_kernel(c_ref, w_ref, b_ref, o_ref):
    o_ref[...] = _bdot(_silu(c_ref[...]), w_ref[...]) + b_ref[...]


def _ada(c_pad, w_ada, b_ada):
    n_out = w_ada.shape[1]
    tn = 1024
    return pl.pallas_call(
        _ada_kernel,
        grid=(n_out // tn,),
        in_specs=[
            pl.BlockSpec((SUBLANES, D_MODEL), lambda j: (0, 0)),
            pl.BlockSpec((D_MODEL, tn), lambda j: (0, j)),
            pl.BlockSpec((1, tn), lambda j: (0, j)),
        ],
        out_specs=pl.BlockSpec((SUBLANES, tn), lambda j: (0, j)),
        out_shape=jax.ShapeDtypeStruct((SUBLANES, n_out), F32),
        name="ada",
    )(c_pad, w_ada, b_ada)


def _inproj_kernel(x_ref, mod_ref, nw_ref, w_ref, sc_ref, qkv_ref, z_ref, a_ref, b_ref, g_ref):
    b = pl.program_id(0)
    mod = mod_ref[pl.ds(b, 1), :]
    sh1 = mod[:, 0:D_MODEL]
    sc1 = mod[:, D_MODEL:2 * D_MODEL]
    h = (_rms(x_ref[...], nw_ref[...]) * (1.0 + sc1) + sh1).astype(BF16)
    col = 0
    for ref, width in ((sc_ref, C_SC), (qkv_ref, C_QKV), (z_ref, C_Z), (a_ref, C_A), (b_ref, C_B), (g_ref, C_G)):
        for j in range(0, width, 512):
            ref[:, j:j + 512] = jnp.dot(h, w_ref[:, col + j:col + j + 512],
                                        preferred_element_type=F32).astype(ref.dtype)
        col += width


def _inproj(x2, mod, nw, w_pad, batch, seq, tm):
    nt = seq // tm
    row = lambda b, t: (b * nt + t, 0)
    const = lambda b, t: (0, 0)
    n = batch * seq
    outs = [(C_SC, BF16), (C_QKV, BF16), (C_Z, BF16), (C_A, F32), (C_B, F32), (C_G, BF16)]
    return pl.pallas_call(
        _inproj_kernel,
        grid=(batch, nt),
        in_specs=[
            pl.BlockSpec((tm, D_MODEL), row),
            pl.BlockSpec(mod.shape, const),
            pl.BlockSpec((1, D_MODEL), const),
            pl.BlockSpec((D_MODEL, C_ALL), const, pipeline_mode=pl.Buffered(1)),
        ],
        out_specs=[pl.BlockSpec((tm, w), row) for w, _ in outs],
        out_shape=[jax.ShapeDtypeStruct((n, w), dt) for w, dt in outs],
        compiler_params=pltpu.CompilerParams(
            dimension_semantics=("arbitrary", "arbitrary"), vmem_limit_bytes=VMEM_MIXER_LIMIT),
        name="inproj",
    )(x2, mod, nw, w_pad)


def _pair_consts():
    r = lax.broadcasted_iota(I32, (PAIR, PAIR), 0)
    c = lax.broadcasted_iota(I32, (PAIR, PAIR), 1)
    bd = ((r >> 6) == (c >> 6)).astype(F32)
    i = lax.broadcasted_iota(I32, (DN_CHUNK, PAIR), 0)
    j = lax.broadcasted_iota(I32, (DN_CHUNK, PAIR), 1) & (DN_DK - 1)
    return bd, i >= j, i > j, (i == j).astype(F32)


def _stack_bd(y, bd):
    return jnp.concatenate([y, y], axis=0) * bd


def _mixer_kernel(sc_ref, qkv_ref, z_ref, a_ref, b_ref, g_ref, x_ref, mod_ref,
                  scw_ref, dnw_ref, alog_ref, dtb_ref, onw_ref, wsc_ref, wdn_ref, wmix_ref, pnw_ref,
                  o_ref,
                  scx_s, qkvx_s, q_s, k_s, v_s, g_s, beta_s, o_s, state_s, *, tt):
    b = pl.program_id(0)
    t = pl.program_id(1)

    @pl.when(t == 0)
    def _():
        scx_s[0:HALO, :] = jnp.zeros((HALO, SC_WIDTH), F32)
        qkvx_s[0:HALO, :] = jnp.zeros((HALO, C_QKV), F32)
        state_s[...] = jnp.zeros(state_s.shape, F32)

    sc = sc_ref[...].astype(F32)
    scx_s[HALO:HALO + tt, :] = sc[:, SC_WIDTH:2 * SC_WIDTH] * sc[:, 2 * SC_WIDTH:]
    conv = scw_ref[0:1, :] * scx_s[pl.ds(HALO - (SC_KERNEL - 1), tt), :]
    for kk in range(1, SC_KERNEL):
        conv = conv + scw_ref[kk:kk + 1, :] * scx_s[pl.ds(HALO - (SC_KERNEL - 1) + kk, tt), :]
    y_sc = _bdot(sc[:, 0:SC_WIDTH] * conv, wsc_ref[...])
    scx_s[0:HALO, :] = scx_s[tt:tt + HALO, :]

    qkvx_s[HALO:HALO + tt, :] = qkv_ref[...].astype(F32)
    cq = dnw_ref[0:1, :] * qkvx_s[pl.ds(HALO - (DN_CONV - 1), tt), :]
    for kk in range(1, DN_CONV):
        cq = cq + dnw_ref[kk:kk + 1, :] * qkvx_s[pl.ds(HALO - (DN_CONV - 1) + kk, tt), :]
    qkvx_s[0:HALO, :] = qkvx_s[tt:tt + HALO, :]
    cq = _silu(cq)
    r512 = lax.broadcasted_iota(I32, (DN_KEY, DN_KEY), 0) >> 6
    c512 = lax.broadcasted_iota(I32, (DN_KEY, DN_KEY), 1) >> 6
    head_ones = (r512 == c512).astype(F32)
    q = cq[:, 0:DN_KEY]
    k = cq[:, DN_KEY:2 * DN_KEY]
    q_s[...] = q * lax.rsqrt(_xdot(q * q, head_ones) + NORM_EPS) * (DN_DK ** -0.5)
    k_s[...] = k * lax.rsqrt(_xdot(k * k, head_ones) + NORM_EPS)
    v_s[...] = cq[:, 2 * DN_KEY:]
    beta_s[...] = jax.nn.sigmoid(b_ref[...])
    g_s[...] = -jnp.exp(alog_ref[...]) * _softplus(a_ref[...] + dtb_ref[...])

    bd, incl, strict, eye = _pair_consts()
    ri = lax.broadcasted_iota(I32, (DN_CHUNK, DN_CHUNK), 0)
    ci = lax.broadcasted_iota(I32, (DN_CHUNK, DN_CHUNK), 1)
    lt = (ci <= ri).astype(F32)

    def chunk(c, carry):
        rows = pl.ds(pl.multiple_of(c * DN_CHUNK, DN_CHUNK), DN_CHUNK)
        gcum_all = _xdot(lt, g_s[rows, :])
        for p in range(N_PAIRS):
            sl = slice(p * PAIR, (p + 1) * PAIR)
            gc = gcum_all[:, sl]
            qp = q_s[rows, sl]
            kp = k_s[rows, sl]
            vp = v_s[rows, sl]
            bp = beta_s[rows, sl]
            grow = jnp.sum(gc * eye, axis=0, keepdims=True)
            diff = gc - grow
            dec = jnp.where(incl, jnp.exp(jnp.where(incl, diff, 0.0)), 0.0)
            eg = jnp.exp(gc)
            glast = gc[DN_CHUNK - 1:DN_CHUNK, :]
            kb = kp * bp
            kbig = _stack_bd(kp, bd).astype(BF16)
            nt = (((1,), (1,)), ((), ()))
            kk_s = lax.dot_general(kb.astype(BF16), kbig, nt, preferred_element_type=F32)
            qk_s = lax.dot_general(qp.astype(BF16), kbig, nt, preferred_element_type=F32)
            lmat = kk_s * jnp.where(strict, dec, 0.0)
            aqk = qk_s * dec
            pw = -lmat
            inv = eye + pw
            for _ in range(5):
                pw = _xdot(pw, _stack_bd(pw, bd))
                inv = inv + _xdot(inv, _stack_bd(pw, bd))
            u = _bdot(inv, _stack_bd(vp * bp, bd))
            w = _bdot(inv, _stack_bd(kb * eg, bd))
            st = state_s[p]
            ws = _bdot(jnp.concatenate([w, qp * eg], axis=0), st)
            vnew = u - ws[0:DN_CHUNK]
            o_s[rows, sl] = ws[DN_CHUNK:] + _bdot(aqk, _stack_bd(vnew, bd))
            kdec = kp * jnp.exp(glast - gc)
            state_s[p] = st * jnp.exp(glast) + bd * _bdot(kdec.T, vnew)
        return carry

    lax.fori_loop(0, tt // DN_CHUNK, chunk, 0)

    o = o_s[...]
    ms = _xdot(o * o, head_ones) * (1.0 / DN_DK)
    og = o * lax.rsqrt(ms + NORM_EPS) * onw_ref[...] * _silu(z_ref[...].astype(F32))
    y_dn = _bdot(og, wdn_ref[...])
    gates = g_ref[...].astype(F32)
    merged = jax.nn.sigmoid(gates[:, 0:D_MODEL]) * y_sc + jax.nn.sigmoid(gates[:, D_MODEL:]) * y_dn
    y = _bdot(merged, wmix_ref[...])
    gt1 = mod_ref[pl.ds(b, 1), :][:, 2 * D_MODEL:3 * D_MODEL]
    o_ref[...] = x_ref[...] + gt1 * _rms(y, pnw_ref[...])


def _mixer(parts, x2, mod, weights, batch, seq, tt):
    nt = seq // tt
    row = lambda b, t: (b * nt + t, 0)
    const2 = lambda b, t: (0, 0)
    sc, qkv, z, a, bb, g = parts
    data = [(sc, C_SC), (qkv, C_QKV), (z, C_Z), (a, C_A), (bb, C_B), (g, C_G), (x2, D_MODEL)]
    in_specs = [pl.BlockSpec((tt, w), row) for _, w in data]
    in_specs.append(pl.BlockSpec(mod.shape, const2))
    in_specs += [pl.BlockSpec(wt.shape, const2) for wt in weights]
    scratch = [
        pltpu.VMEM((tt + HALO, SC_WIDTH), F32),
        pltpu.VMEM((tt + HALO, C_QKV), F32),
        pltpu.VMEM((tt, DN_KEY), F32),
        pltpu.VMEM((tt, DN_KEY), F32),
        pltpu.VMEM((tt, DN_VAL), F32),
        pltpu.VMEM((tt, DN_KEY), F32),
        pltpu.VMEM((tt, DN_KEY), F32),
        pltpu.VMEM((tt, DN_VAL), F32),
        pltpu.VMEM((N_PAIRS, PAIR, PAIR), F32),
    ]
    return pl.pallas_call(
        functools.partial(_mixer_kernel, tt=tt),
        grid=(batch, nt),
        in_specs=in_specs,
        out_specs=pl.BlockSpec((tt, D_MODEL), row),
        out_shape=jax.ShapeDtypeStruct((batch * seq, D_MODEL), F32),
        scratch_shapes=scratch,
        compiler_params=pltpu.CompilerParams(
            dimension_semantics=("arbitrary", "arbitrary"), vmem_limit_bytes=VMEM_MIXER_LIMIT),
        name="mixer",
    )(*[d for d, _ in data], mod, *weights)


def _qproj_kernel(x_ref, mod_ref, nw_ref, w_ref, h_ref, q_ref):
    b = pl.program_id(0)
    mod = mod_ref[pl.ds(b, 1), :]
    sh2 = mod[:, 3 * D_MODEL:4 * D_MODEL]
    sc2 = mod[:, 4 * D_MODEL:5 * D_MODEL]
    h = _rms(x_ref[...], nw_ref[...]) * (1.0 + sc2) + sh2
    h_ref[...] = h
    q_ref[...] = _bdot(h, w_ref[...])


def _qproj(x1, mod, nw, wq, batch, seq, tm):
    nt = seq // tm
    row = lambda b, t: (b * nt + t, 0)
    const = lambda b, t: (0, 0)
    n = batch * seq
    nq = wq.shape[1]
    return pl.pallas_call(
        _qproj_kernel,
        grid=(batch, nt),
        in_specs=[
            pl.BlockSpec((tm, D_MODEL), row),
            pl.BlockSpec(mod.shape, const),
            pl.BlockSpec((1, D_MODEL), const),
            pl.BlockSpec(wq.shape, const),
        ],
        out_specs=[pl.BlockSpec((tm, D_MODEL), row), pl.BlockSpec((tm, nq), row)],
        out_shape=[jax.ShapeDtypeStruct((n, D_MODEL), F32), jax.ShapeDtypeStruct((n, nq), F32)],
        compiler_params=pltpu.CompilerParams(dimension_semantics=("arbitrary", "arbitrary")),
        name="qproj",
    )(x1, mod, nw, wq)


def _topk_rows(vals, idx_payload, n_rows, k, v_out, i_out, row0):
    pos_iota = lax.broadcasted_iota(I32, vals.shape, 0)
    for it in range(k):
        m = jnp.max(vals, axis=0, keepdims=True)
        pos = jnp.min(jnp.where(vals == m, pos_iota, n_rows), axis=0, keepdims=True)
        hit = pos_iota == pos
        v_out[row0 + it:row0 + it + 1, :] = m
        i_out[row0 + it:row0 + it + 1, :] = jnp.sum(jnp.where(hit, idx_payload, 0), axis=0, keepdims=True)
        vals = jnp.where(hit, -jnp.inf, vals)


def _topk_kernel(q_ref, keys_ref, e_ref, g_ref, v1_s, i1_s, v2_s, i2_s, ts_s, ei_s, gate_s, *, tt):
    key_iota = lax.broadcasted_iota(I32, (PEER_NKEYS, tt), 0)
    nt = (((1,), (1,)), ((), ()))
    k = PEER_TOPK
    for h in range(PEER_HEADS):
        for p, (v_s, i_s) in enumerate(((v1_s, i1_s), (v2_s, i2_s))):
            col = (2 * h + p) * PEER_HALF
            st = lax.dot_general(keys_ref[2 * h + p], q_ref[:, col:col + PEER_HALF], nt,
                                 preferred_element_type=F32, precision=lax.Precision.HIGHEST)
            _topk_rows(st, key_iota, PEER_NKEYS, k, v_s, i_s, 0)
        v2 = v2_s[...]
        i2 = i2_s[...]
        cv = jnp.concatenate([v1_s[a:a + 1, :] + v2 for a in range(k)], axis=0)
        ce = jnp.concatenate([i1_s[a:a + 1, :] * PEER_NKEYS + i2 for a in range(k)], axis=0)
        _topk_rows(cv, ce, k * k, k, ts_s, ei_s, h * k)
        ts = ts_s[h * k:(h + 1) * k, :]
        e = jnp.exp(ts - jnp.max(ts, axis=0, keepdims=True))
        gate_s[h * k:(h + 1) * k, :] = e / jnp.sum(e, axis=0, keepdims=True)
    e_ref[...] = ei_s[...].T * HALF_ROWS
    g_ref[...] = gate_s[...].T


def _topk(q, keys, n, tt):
    row = lambda i: (i, 0)
    return pl.pallas_call(
        functools.partial(_topk_kernel, tt=tt),
        grid=(n // tt,),
        in_specs=[
            pl.BlockSpec((tt, q.shape[1]), row),
            pl.BlockSpec(keys.shape, lambda i: (0, 0, 0)),
        ],
        out_specs=[pl.BlockSpec((tt, PEER_E), row)] * 2,
        out_shape=[jax.ShapeDtypeStruct((n, PEER_E), I32), jax.ShapeDtypeStruct((n, PEER_E), F32)],
        scratch_shapes=[
            pltpu.VMEM((PEER_TOPK, tt), F32), pltpu.VMEM((PEER_TOPK, tt), I32),
            pltpu.VMEM((PEER_TOPK, tt), F32), pltpu.VMEM((PEER_TOPK, tt), I32),
            pltpu.VMEM((PEER_E, tt), F32), pltpu.VMEM((PEER_E, tt), I32),
            pltpu.VMEM((PEER_E, tt), F32),
        ],
        compiler_params=pltpu.CompilerParams(dimension_semantics=("arbitrary",)),
        name="topk",
    )(q, keys)


def _gather_group(tab_ref, e_ref, t, g, scr):
    ref = scr[g % N_SCR]
    base = (g // N_SCR) * GROUP_ROWS
    for j in range(GROUP):
        e4 = pl.multiple_of(e_ref[t, g * GROUP + j], HALF_ROWS)
        ref[pl.ds(base + j, HALF_ROWS, stride=GROUP), :] = tab_ref[pl.ds(e4, HALF_ROWS), :]
    return [ref[base + s * GROUP:base + (s + 1) * GROUP, :] for s in range(HALF_ROWS)]


def _unpack(w):
    return pltpu.bitcast(w << 16, F32), pltpu.bitcast(w & jnp.uint32(0xFFFF0000), F32)


def _row_sums_on_lanes(m, eye, ones):
    hi = m.astype(BF16)
    lo = (m - hi.astype(F32)).astype(BF16)
    sums = jnp.dot(hi, ones, preferred_element_type=F32) + jnp.dot(lo, ones, preferred_element_type=F32)
    return jnp.sum(sums * eye, axis=0, keepdims=True)


def _peer_u_kernel(e_ref, h_ref, g_ref, tab_ref, o_ref, part_s, *scr, tt):
    eye = (lax.broadcasted_iota(I32, (PEER_E, LANES), 0) == lax.broadcasted_iota(I32, (PEER_E, LANES), 1)).astype(F32)
    ones = jnp.ones((LANES, LANES), BF16)

    def finish(t, slot):
        act = _row_sums_on_lanes(part_s[slot], eye, ones)
        o_ref[pl.ds(t, 1), :] = g_ref[pl.ds(t, 1), :] * _gelu(act)

    part_s[1] = jnp.zeros((PEER_E, LANES), F32)

    def tok(t, carry):
        finish(jnp.maximum(t - 1, 0), (t + 1) & 1)
        h_lo = [jnp.broadcast_to(h_ref[t, s:s + 1, :], (GROUP, LANES)) for s in range(HALF_ROWS)]
        h_hi = [jnp.broadcast_to(h_ref[t, HALF_ROWS + s:HALF_ROWS + s + 1, :], (GROUP, LANES))
                for s in range(HALF_ROWS)]
        for g in range(N_GROUPS):
            r = None
            for s, tile in enumerate(_gather_group(tab_ref, e_ref, t, g, scr)):
                lo, hi = _unpack(tile)
                p = lo * h_lo[s] + hi * h_hi[s]
                r = p if r is None else r + p
            part_s[t & 1, g * GROUP:(g + 1) * GROUP, :] = r
        return carry

    lax.fori_loop(0, tt, tok, 0)
    finish(tt - 1, (tt - 1) & 1)


def _peer_u(e4, h3, gates, tab, n, tt):
    row = lambda i: (i, 0)
    return pl.pallas_call(
        functools.partial(_peer_u_kernel, tt=tt),
        grid=(n // tt,),
        in_specs=[
            pl.BlockSpec((tt, PEER_E), row, memory_space=pltpu.SMEM),
            pl.BlockSpec((tt, SUBLANES, LANES), lambda i: (i, 0, 0)),
            pl.BlockSpec((tt, PEER_E), row),
            pl.BlockSpec(tab.shape, lambda i: (0, 0), pipeline_mode=pl.Buffered(1)),
        ],
        out_specs=pl.BlockSpec((tt, PEER_E), row),
        out_shape=jax.ShapeDtypeStruct((n, PEER_E), F32),
        scratch_shapes=[pltpu.VMEM((2, PEER_E, LANES), F32)]
        + [pltpu.VMEM((N_GROUPS // N_SCR * GROUP_ROWS, LANES), U32) for _ in range(N_SCR)],
        compiler_params=pltpu.CompilerParams(
            dimension_semantics=("arbitrary",), vmem_limit_bytes=VMEM_TABLE_LIMIT),
        name="peer_u",
    )(e4, h3, gates, tab)


def _peer_v_kernel(e_ref, a_ref, tab_ref, o_ref, w_s, *scr, tt):
    eye = (lax.broadcasted_iota(I32, (PEER_E, LANES), 0) == lax.broadcasted_iota(I32, (PEER_E, LANES), 1)).astype(F32)
    ones = jnp.ones((LANES, LANES), BF16)

    def spread(t):
        m = eye * a_ref[pl.ds(t, 1), :]
        hi = m.astype(BF16)
        lo = (m - hi.astype(F32)).astype(BF16)
        return jnp.dot(hi, ones, preferred_element_type=F32) + jnp.dot(lo, ones, preferred_element_type=F32)

    w_s[0] = spread(0)

    def tok(t, carry):
        w_next = spread(jnp.minimum(t + 1, tt - 1))
        acc_lo = [None] * HALF_ROWS
        acc_hi = [None] * HALF_ROWS
        for g in range(N_GROUPS):
            wg = w_s[t & 1, g * GROUP:(g + 1) * GROUP, :]
            for s, tile in enumerate(_gather_group(tab_ref, e_ref, t, g, scr)):
                lo, hi = _unpack(tile)
                acc_lo[s] = lo * wg if g == 0 else acc_lo[s] + lo * wg
                acc_hi[s] = hi * wg if g == 0 else acc_hi[s] + hi * wg
        for s in range(HALF_ROWS):
            o_ref[t, s:s + 1, :] = jnp.sum(acc_lo[s], axis=0, keepdims=True)
            o_ref[t, HALF_ROWS + s:HALF_ROWS + s + 1, :] = jnp.sum(acc_hi[s], axis=0, keepdims=True)
        w_s[(t + 1) & 1] = w_next
        return carry

    lax.fori_loop(0, tt, tok, 0)


def _peer_v(e4, act, tab, n, tt):
    row = lambda i: (i, 0)
    return pl.pallas_call(
        functools.partial(_peer_v_kernel, tt=tt),
        grid=(n // tt,),
        in_specs=[
            pl.BlockSpec((tt, PEER_E), row, memory_space=pltpu.SMEM),
            pl.BlockSpec((tt, PEER_E), row),
            pl.BlockSpec(tab.shape, lambda i: (0, 0), pipeline_mode=pl.Buffered(1)),
        ],
        out_specs=pl.BlockSpec((tt, SUBLANES, LANES), lambda i: (i, 0, 0)),
        out_shape=jax.ShapeDtypeStruct((n, SUBLANES, LANES), F32),
        scratch_shapes=[pltpu.VMEM((2, PEER_E, LANES), F32)]
        + [pltpu.VMEM((N_GROUPS // N_SCR * GROUP_ROWS, LANES), U32) for _ in range(N_SCR)],
        compiler_params=pltpu.CompilerParams(
            dimension_semantics=("arbitrary",), vmem_limit_bytes=VMEM_TABLE_LIMIT),
        name="peer_v",
    )(e4, act, tab)


def _final_kernel(x_ref, y_ref, mod_ref, nw_ref, o_ref):
    b = pl.program_id(0)
    gt2 = mod_ref[pl.ds(b, 1), :][:, 5 * D_MODEL:6 * D_MODEL]
    o_ref[...] = x_ref[...] + gt2 * _rms(y_ref[...], nw_ref[...])


def _final(x1, y2, mod, nw, batch, seq, tm):
    nt = seq // tm
    row = lambda b, t: (b * nt + t, 0)
    const = lambda b, t: (0, 0)
    return pl.pallas_call(
        _final_kernel,
        grid=(batch, nt),
        in_specs=[
            pl.BlockSpec((tm, D_MODEL), row),
            pl.BlockSpec((tm, D_MODEL), row),
            pl.BlockSpec(mod.shape, const),
            pl.BlockSpec((1, D_MODEL), const),
        ],
        out_specs=pl.BlockSpec((tm, D_MODEL), row),
        out_shape=jax.ShapeDtypeStruct((batch * seq, D_MODEL), F32),
        compiler_params=pltpu.CompilerParams(dimension_semantics=("arbitrary", "arbitrary")),
        name="final",
    )(x1, y2, mod, nw)


def _pack_table(tab):
    e, d = tab.shape
    bits = lax.bitcast_convert_type(tab.astype(BF16), jnp.uint16).astype(U32)
    return (bits[:, :d // 2] | (bits[:, d // 2:] << 16)).reshape(e * HALF_ROWS, LANES)


def _tile(seq, cap):
    t = min(seq, cap)
    assert seq % t == 0
    return t


def kernel(x, c, w_ada, b_ada, norm_pre_mix, norm_post_mix, w_in_mix, sc_conv_w, dn_conv_w, dn_a_log,
           dn_dt_bias, dn_out_norm, w_sc_out, w_dn_out, w_mix_out, norm_pre_ffn, norm_post_ffn, peer_w_q,
           peer_sub_keys, peer_u, peer_v):
    batch, seq, d = x.shape
    assert d == D_MODEL and batch <= SUBLANES and seq % DN_CHUNK == 0
    depth = w_ada.shape[0]
    n = batch * seq
    x2 = x.reshape(n, d)
    c_pad = jnp.pad(c, ((0, SUBLANES - batch), (0, 0)))
    tab_u = _pack_table(peer_u)
    tab_v = _pack_table(peer_v)
    rep = lambda v: jnp.repeat(v, DN_DK, axis=-1)
    for l in range(depth):
        mod = _ada(c_pad, w_ada[l], b_ada[l][None, :])
        w = w_in_mix[l]
        o_z = C_SC + C_QKV
        o_a = o_z + C_Z
        w_pad = jnp.concatenate(
            [w[:, :o_a], rep(w[:, o_a:o_a + DN_HEADS]), rep(w[:, o_a + DN_HEADS:o_a + 2 * DN_HEADS]),
             w[:, o_a + 2 * DN_HEADS:]], axis=1).astype(BF16)
        tm = _tile(seq, 512)
        parts = _inproj(x2, mod, norm_pre_mix[l][None, :], w_pad, batch, seq, tm)
        weights = [
            sc_conv_w[l], dn_conv_w[l], rep(dn_a_log[l])[None, :], rep(dn_dt_bias[l])[None, :],
            jnp.tile(dn_out_norm[l], DN_HEADS)[None, :], w_sc_out[l].astype(BF16), w_dn_out[l].astype(BF16),
            w_mix_out[l].astype(BF16), norm_post_mix[l][None, :],
        ]
        x1 = _mixer(parts, x2, mod, weights, batch, seq, _tile(seq, 512))
        h2, q = _qproj(x1, mod, norm_pre_ffn[l][None, :], peer_w_q[l].astype(BF16), batch, seq, tm)
        keys = peer_sub_keys[l].reshape(2 * PEER_HEADS, PEER_NKEYS, PEER_HALF)
        e4, gates = _topk(q, keys, n, _tile(n, 256))
        tt = _tile(n, 64)
        act = _peer_u(e4, h2.reshape(n, SUBLANES, LANES), gates, tab_u, n, tt)
        y2 = _peer_v(e4, act, tab_v, n, tt).reshape(n, d)
        x2 = _final(x1, y2, mod, norm_post_ffn[l][None, :], batch, seq, tm)
    return x2.reshape(batch, seq, d)
```

```python
import functools

import jax
import jax.numpy as jnp
from jax import lax
from jax.experimental import pallas as pl
from jax.experimental.pallas import tpu as pltpu

F32 = jnp.float32
BF16 = jnp.bfloat16
I32 = jnp.int32
U32 = jnp.uint32

D_MODEL = 1024
SC_WIDTH = 512
SC_KERNEL = 3
DN_HEADS = 8
DN_DK = 64
DN_KEY = DN_HEADS * DN_DK
DN_VAL = DN_KEY
DN_CONV = 4
DN_CHUNK = 64
PEER_HEADS = 8
PEER_NKEYS = 128
PEER_HALF = 128
PEER_QDIM = 256
PEER_TOPK = 16
PEER_E = PEER_HEADS * PEER_TOPK
NORM_EPS = 1e-6

LANES = 128
SUBLANES = 8
HALO = SUBLANES
PAIR = 2 * DN_DK
N_PAIRS = DN_HEADS // 2
INV_TERMS = 2
PREP_CHUNKS = 2
HALF_ROWS = SUBLANES // 2
GROUP = SUBLANES
GROUP_ROWS = HALF_ROWS * GROUP
N_GROUPS = PEER_E // GROUP
N_SCR = 4
VMEM_TABLE_LIMIT = 56 * 1024 * 1024
VMEM_MIXER_LIMIT = 56 * 1024 * 1024

C_SC = 3 * SC_WIDTH
C_QKV = 2 * DN_KEY + DN_VAL
C_Z = DN_VAL
C_A = DN_KEY
C_B = DN_KEY
C_G = 2 * D_MODEL
C_ALL = C_SC + C_QKV + C_Z + C_A + C_B + C_G


def _silu(x):
    return x * jax.nn.sigmoid(x)


def _softplus(x):
    return jnp.maximum(x, 0.0) + jnp.log(1.0 + jnp.exp(-jnp.abs(x)))


def _gelu(x):
    return 0.5 * x * (1.0 + lax.erf(x * (2.0 ** -0.5)))


def _rms(x, w):
    return x * lax.rsqrt(jnp.mean(x * x, axis=-1, keepdims=True) + NORM_EPS) * w


def _bdot(a, b):
    return jnp.dot(a.astype(BF16), b.astype(BF16), preferred_element_type=F32)


def _xdot(a, b):
    return jnp.dot(a, b, preferred_element_type=F32, precision=lax.Precision.HIGHEST)


def _split(x, n):
    terms = []
    for i in range(n):
        t = x.astype(BF16)
        terms.append(t)
        if i + 1 < n:
            x = x - t.astype(F32)
    return terms


NN_DIMS = (((1,), (0,)), ((), ()))
NT_DIMS = (((1,), (1,)), ((), ()))


def _sdot(a_terms, b_terms, dims=NN_DIMS):
    order = max(len(a_terms), len(b_terms))
    out = None
    for i, a in enumerate(a_terms):
        for j, b in enumerate(b_terms):
            if i + j < order:
                p = lax.dot_general(a, b, dims, preferred_element_type=F32)
                out = p if out is None else out + p
    return out


def _ada_kernel(c_ref, w_ref, b_ref, o_ref):
    o_ref[...] = _bdot(_silu(c_ref[...]), w_ref[...]) + b_ref[...]


def _ada(c_pad, w_ada, b_ada):
    n_out = w_ada.shape[1]
    tn = 1024
    return pl.pallas_call(
        _ada_kernel,
        grid=(n_out // tn,),
        in_specs=[
            pl.BlockSpec((SUBLANES, D_MODEL), lambda j: (0, 0)),
            pl.BlockSpec((D_MODEL, tn), lambda j: (0, j)),
            pl.BlockSpec((1, tn), lambda j: (0, j)),
        ],
        out_specs=pl.BlockSpec((SUBLANES, tn), lambda j: (0, j)),
        out_shape=jax.ShapeDtypeStruct((SUBLANES, n_out), F32),
        name="ada",
    )(c_pad, w_ada, b_ada)


def _inproj_kernel(x_ref, mod_ref, nw_ref, w_ref, sc_ref, qkv_ref, z_ref, a_ref, b_ref, g_ref):
    b = pl.program_id(0)
    mod = mod_ref[pl.ds(b, 1), :]
    sh1 = mod[:, 0:D_MODEL]
    sc1 = mod[:, D_MODEL:2 * D_MODEL]
    h = (_rms(x_ref[...], nw_ref[...]) * (1.0 + sc1) + sh1).astype(BF16)
    col = 0
    for ref, width in ((sc_ref, C_SC), (qkv_ref, C_QKV), (z_ref, C_Z), (a_ref, C_A), (b_ref, C_B), (g_ref, C_G)):
        for j in range(0, width, 512):
            ref[:, j:j + 512] = jnp.dot(h, w_ref[:, col + j:col + j + 512],
                                        preferred_element_type=F32).astype(ref.dtype)
        col += width


def _inproj(x2, mod, nw, w_pad, batch, seq, tm):
    nt = seq // tm
    row = lambda b, t: (b * nt + t, 0)
    const = lambda b, t: (0, 0)
    n = batch * seq
    outs = [(C_SC, BF16), (C_QKV, BF16), (C_Z, BF16), (C_A, F32), (C_B, F32), (C_G, BF16)]
    return pl.pallas_call(
        _inproj_kernel,
        grid=(batch, nt),
        in_specs=[
            pl.BlockSpec((tm, D_MODEL), row),
            pl.BlockSpec(mod.shape, const),
            pl.BlockSpec((1, D_MODEL), const),
            pl.BlockSpec((D_MODEL, C_ALL), const, pipeline_mode=pl.Buffered(1)),
        ],
        out_specs=[pl.BlockSpec((tm, w), row) for w, _ in outs],
        out_shape=[jax.ShapeDtypeStruct((n, w), dt) for w, dt in outs],
        compiler_params=pltpu.CompilerParams(
            dimension_semantics=("arbitrary", "arbitrary"), vmem_limit_bytes=VMEM_MIXER_LIMIT),
        name="inproj",
    )(x2, mod, nw, w_pad)


def _pair_consts():
    r = lax.broadcasted_iota(I32, (PAIR, PAIR), 0)
    c = lax.broadcasted_iota(I32, (PAIR, PAIR), 1)
    bd = ((r >> 6) == (c >> 6)).astype(F32)
    i = lax.broadcasted_iota(I32, (DN_CHUNK, PAIR), 0)
    j = lax.broadcasted_iota(I32, (DN_CHUNK, PAIR), 1) & (DN_DK - 1)
    return bd, i >= j, i > j, (i == j).astype(F32)


def _stack_bd(y, bd):
    return jnp.concatenate([y, y], axis=0) * bd


def _mixer_kernel(sc_ref, qkv_ref, z_ref, a_ref, b_ref, g_ref, x_ref, mod_ref,
                  scw_ref, dnw_ref, alog_ref, dtb_ref, onw_ref, wsc_ref, wdn_ref, wmix_ref, pnw_ref,
                  o_ref,
                  scx_s, qkvx_s, q_s, k_s, v_s, g_s, beta_s, o_s, gcum_s, aqk_s, u_s, wk_s, qg_s, kdec_s,
                  state_s, *, tt):
    b = pl.program_id(0)
    t = pl.program_id(1)

    @pl.when(t == 0)
    def _():
        scx_s[0:HALO, :] = jnp.zeros((HALO, SC_WIDTH), F32)
        qkvx_s[0:HALO, :] = jnp.zeros((HALO, C_QKV), F32)
        state_s[...] = jnp.zeros(state_s.shape, F32)

    sc = sc_ref[...].astype(F32)
    scx_s[HALO:HALO + tt, :] = sc[:, SC_WIDTH:2 * SC_WIDTH] * sc[:, 2 * SC_WIDTH:]
    conv = scw_ref[0:1, :] * scx_s[pl.ds(HALO - (SC_KERNEL - 1), tt), :]
    for kk in range(1, SC_KERNEL):
        conv = conv + scw_ref[kk:kk + 1, :] * scx_s[pl.ds(HALO - (SC_KERNEL - 1) + kk, tt), :]
    y_sc = _bdot(sc[:, 0:SC_WIDTH] * conv, wsc_ref[...])
    scx_s[0:HALO, :] = scx_s[tt:tt + HALO, :]

    qkvx_s[HALO:HALO + tt, :] = qkv_ref[...].astype(F32)
    cq = dnw_ref[0:1, :] * qkvx_s[pl.ds(HALO - (DN_CONV - 1), tt), :]
    for kk in range(1, DN_CONV):
        cq = cq + dnw_ref[kk:kk + 1, :] * qkvx_s[pl.ds(HALO - (DN_CONV - 1) + kk, tt), :]
    qkvx_s[0:HALO, :] = qkvx_s[tt:tt + HALO, :]
    cq = _silu(cq)
    r512 = lax.broadcasted_iota(I32, (DN_KEY, DN_KEY), 0) >> 6
    c512 = lax.broadcasted_iota(I32, (DN_KEY, DN_KEY), 1) >> 6
    head_ones = [(r512 == c512).astype(BF16)]
    q = cq[:, 0:DN_KEY]
    k = cq[:, DN_KEY:2 * DN_KEY]
    q_s[...] = q * lax.rsqrt(_sdot(_split(q * q, 2), head_ones) + NORM_EPS) * (DN_DK ** -0.5)
    k_s[...] = k * lax.rsqrt(_sdot(_split(k * k, 2), head_ones) + NORM_EPS)
    v_s[...] = cq[:, 2 * DN_KEY:]
    beta_s[...] = jax.nn.sigmoid(b_ref[...])
    g_s[...] = -jnp.exp(alog_ref[...]) * _softplus(a_ref[...] + dtb_ref[...])

    bd, incl, strict, eye = _pair_consts()
    bd_b = bd.astype(BF16)
    ri = lax.broadcasted_iota(I32, (DN_CHUNK, DN_CHUNK), 0)
    ci = lax.broadcasted_iota(I32, (DN_CHUNK, DN_CHUNK), 1)
    lt = [(ci <= ri).astype(BF16)]
    nt = (((1,), (1,)), ((), ()))

    def stack_terms(y):
        return [jnp.concatenate([t_, t_], axis=0) * bd_b for t_ in _split(y, INV_TERMS)]

    pairs = [slice(p * PAIR, (p + 1) * PAIR) for p in range(N_PAIRS)]

    def prepare(c, carry):
        prob = []
        for cc in range(PREP_CHUNKS):
            rows = pl.ds(pl.multiple_of((c * PREP_CHUNKS + cc) * DN_CHUNK, DN_CHUNK), DN_CHUNK)
            gcum_all = _sdot(lt, _split(g_s[rows, :], 3))
            gcum_s[rows, :] = gcum_all
            prob += [(rows, sl, gcum_all[:, sl]) for sl in pairs]
        pws, invs, rest = [], [], []
        for rows, sl, gc in prob:
            qp = q_s[rows, sl]
            kp = k_s[rows, sl]
            bp = beta_s[rows, sl]
            grow = jnp.sum(gc * eye, axis=0, keepdims=True)
            dec = jnp.where(incl, jnp.exp(jnp.where(incl, gc - grow, 0.0)), 0.0)
            eg = jnp.exp(gc)
            kb = kp * bp
            kbig = _stack_bd(kp, bd).astype(BF16)
            kk_s = lax.dot_general(kb.astype(BF16), kbig, nt, preferred_element_type=F32)
            qk_s = lax.dot_general(qp.astype(BF16), kbig, nt, preferred_element_type=F32)
            aqk_s[rows, sl] = qk_s * dec
            qg_s[rows, sl] = qp * eg
            kdec_s[rows, sl] = kp * jnp.exp(gc[DN_CHUNK - 1:DN_CHUNK, :] - gc)
            pw = -(kk_s * jnp.where(strict, dec, 0.0))
            pws.append(pw)
            invs.append(eye + pw)
            rest.append((v_s[rows, sl] * bp, kb * eg))
        for _ in range(5):
            pws = [_sdot(_split(pw, INV_TERMS), stack_terms(pw)) for pw in pws]
            invs = [inv + _sdot(_split(inv, INV_TERMS), stack_terms(pw)) for inv, pw in zip(invs, pws)]
        for (rows, sl, _), inv, (vb, kbg) in zip(prob, invs, rest):
            u_s[rows, sl] = _bdot(inv, _stack_bd(vb, bd))
            wk_s[rows, sl] = _bdot(inv, _stack_bd(kbg, bd))
        return carry

    def advance(c, carry):
        rows = pl.ds(pl.multiple_of(c * DN_CHUNK, DN_CHUNK), DN_CHUNK)
        tail = pl.ds(pl.multiple_of(c * DN_CHUNK + DN_CHUNK - SUBLANES, SUBLANES), SUBLANES)
        sts = [state_s[p] for p in range(N_PAIRS)]
        wss = [_bdot(jnp.concatenate([wk_s[rows, sl], qg_s[rows, sl]], axis=0), st) for sl, st in zip(pairs, sts)]
        vnews = [u_s[rows, sl] - ws[0:DN_CHUNK] for sl, ws in zip(pairs, wss)]
        for p, (sl, st, ws, vnew) in enumerate(zip(pairs, sts, wss, vnews)):
            o_s[rows, sl] = ws[DN_CHUNK:] + _bdot(aqk_s[rows, sl], _stack_bd(vnew, bd))
            glast = gcum_s[tail, sl][SUBLANES - 1:SUBLANES, :]
            state_s[p] = st * jnp.exp(glast) + bd * _bdot(kdec_s[rows, sl].T, vnew)
        return carry

    lax.fori_loop(0, tt // (DN_CHUNK * PREP_CHUNKS), prepare, 0)
    lax.fori_loop(0, tt // DN_CHUNK, advance, 0)

    o = o_s[...]
    ms = _sdot(_split(o * o, 2), head_ones) * (1.0 / DN_DK)
    og = o * lax.rsqrt(ms + NORM_EPS) * onw_ref[...] * _silu(z_ref[...].astype(F32))
    y_dn = _bdot(og, wdn_ref[...])
    gates = g_ref[...].astype(F32)
    merged = jax.nn.sigmoid(gates[:, 0:D_MODEL]) * y_sc + jax.nn.sigmoid(gates[:, D_MODEL:]) * y_dn
    y = _bdot(merged, wmix_ref[...])
    gt1 = mod_ref[pl.ds(b, 1), :][:, 2 * D_MODEL:3 * D_MODEL]
    o_ref[...] = x_ref[...] + gt1 * _rms(y, pnw_ref[...])


def _mixer(parts, x2, mod, weights, batch, seq, tt):
    nt = seq // tt
    row = lambda b, t: (b * nt + t, 0)
    const2 = lambda b, t: (0, 0)
    sc, qkv, z, a, bb, g = parts
    data = [(sc, C_SC), (qkv, C_QKV), (z, C_Z), (a, C_A), (bb, C_B), (g, C_G), (x2, D_MODEL)]
    in_specs = [pl.BlockSpec((tt, w), row) for _, w in data]
    in_specs.append(pl.BlockSpec(mod.shape, const2))
    in_specs += [pl.BlockSpec(wt.shape, const2) for wt in weights]
    scratch = [
        pltpu.VMEM((tt + HALO, SC_WIDTH), F32),
        pltpu.VMEM((tt + HALO, C_QKV), F32),
        pltpu.VMEM((tt, DN_KEY), F32),
        pltpu.VMEM((tt, DN_KEY), F32),
        pltpu.VMEM((tt, DN_VAL), F32),
        pltpu.VMEM((tt, DN_KEY), F32),
        pltpu.VMEM((tt, DN_KEY), F32),
        pltpu.VMEM((tt, DN_VAL), F32),
    ] + [pltpu.VMEM((tt, DN_KEY), F32) for _ in range(6)] + [
        pltpu.VMEM((N_PAIRS, PAIR, PAIR), F32),
    ]
    return pl.pallas_call(
        functools.partial(_mixer_kernel, tt=tt),
        grid=(batch, nt),
        in_specs=in_specs,
        out_specs=pl.BlockSpec((tt, D_MODEL), row),
        out_shape=jax.ShapeDtypeStruct((batch * seq, D_MODEL), F32),
        scratch_shapes=scratch,
        compiler_params=pltpu.CompilerParams(
            dimension_semantics=("arbitrary", "arbitrary"), vmem_limit_bytes=VMEM_MIXER_LIMIT),
        name="mixer",
    )(*[d for d, _ in data], mod, *weights)


def _qproj_kernel(x_ref, mod_ref, nw_ref, w_ref, h_ref, q_ref):
    b = pl.program_id(0)
    mod = mod_ref[pl.ds(b, 1), :]
    sh2 = mod[:, 3 * D_MODEL:4 * D_MODEL]
    sc2 = mod[:, 4 * D_MODEL:5 * D_MODEL]
    h = _rms(x_ref[...], nw_ref[...]) * (1.0 + sc2) + sh2
    h_ref[...] = h
    q_ref[...] = _bdot(h, w_ref[...])


def _qproj(x1, mod, nw, wq, batch, seq, tm):
    nt = seq // tm
    row = lambda b, t: (b * nt + t, 0)
    const = lambda b, t: (0, 0)
    n = batch * seq
    nq = wq.shape[1]
    return pl.pallas_call(
        _qproj_kernel,
        grid=(batch, nt),
        in_specs=[
            pl.BlockSpec((tm, D_MODEL), row),
            pl.BlockSpec(mod.shape, const),
            pl.BlockSpec((1, D_MODEL), const),
            pl.BlockSpec(wq.shape, const),
        ],
        out_specs=[pl.BlockSpec((tm, D_MODEL), row), pl.BlockSpec((tm, nq), row)],
        out_shape=[jax.ShapeDtypeStruct((n, D_MODEL), F32), jax.ShapeDtypeStruct((n, nq), F32)],
        compiler_params=pltpu.CompilerParams(dimension_semantics=("arbitrary", "arbitrary")),
        name="qproj",
    )(x1, mod, nw, wq)


def _topk_rows(vals, idx_payload, k, v_out, i_out, row0):
    n_rows = vals.shape[0]
    pos_iota = lax.broadcasted_iota(I32, vals.shape, 0)
    for it in range(k):
        m = jnp.max(vals, axis=0, keepdims=True)
        pos = jnp.min(jnp.where(vals == m, pos_iota, n_rows), axis=0, keepdims=True)
        hit = pos_iota == pos
        v_out[row0 + it:row0 + it + 1, :] = m
        if idx_payload is None:
            i_out[row0 + it:row0 + it + 1, :] = pos
        else:
            i_out[row0 + it:row0 + it + 1, :] = jnp.sum(jnp.where(hit, idx_payload, 0), axis=0, keepdims=True)
        vals = jnp.where(hit, -jnp.inf, vals)


def _topk_kernel(q_ref, keys_ref, e_ref, g_ref, v1_s, i1_s, v2_s, i2_s, ts_s, ei_s, gate_s, *, tt):
    k = PEER_TOPK
    for h in range(PEER_HEADS):
        for p, (v_s, i_s) in enumerate(((v1_s, i1_s), (v2_s, i2_s))):
            col = (2 * h + p) * PEER_HALF
            st = _sdot(_split(keys_ref[2 * h + p], 2), _split(q_ref[:, col:col + PEER_HALF], 2), NT_DIMS)
            _topk_rows(st, None, k, v_s, i_s, 0)
        v2h = v2_s[0:SUBLANES, :]
        i2h = i2_s[0:SUBLANES, :]
        cv = jnp.concatenate([v1_s[0:1, :] + v2_s[...]] + [v1_s[a:a + 1, :] + v2h for a in range(1, k)], axis=0)
        ce = jnp.concatenate([i1_s[0:1, :] * PEER_NKEYS + i2_s[...]]
                             + [i1_s[a:a + 1, :] * PEER_NKEYS + i2h for a in range(1, k)], axis=0)
        _topk_rows(cv, ce, k, ts_s, ei_s, h * k)
        ts = ts_s[h * k:(h + 1) * k, :]
        e = jnp.exp(ts - jnp.max(ts, axis=0, keepdims=True))
        gate_s[h * k:(h + 1) * k, :] = e / jnp.sum(e, axis=0, keepdims=True)
    e_ref[...] = ei_s[...].T * HALF_ROWS
    g_ref[...] = gate_s[...].T


def _topk(q, keys, n, tt):
    row = lambda i: (i, 0)
    return pl.pallas_call(
        functools.partial(_topk_kernel, tt=tt),
        grid=(n // tt,),
        in_specs=[
            pl.BlockSpec((tt, q.shape[1]), row),
            pl.BlockSpec(keys.shape, lambda i: (0, 0, 0)),
        ],
        out_specs=[pl.BlockSpec((tt, PEER_E), row)] * 2,
        out_shape=[jax.ShapeDtypeStruct((n, PEER_E), I32), jax.ShapeDtypeStruct((n, PEER_E), F32)],
        scratch_shapes=[
            pltpu.VMEM((PEER_TOPK, tt), F32), pltpu.VMEM((PEER_TOPK, tt), I32),
            pltpu.VMEM((PEER_TOPK, tt), F32), pltpu.VMEM((PEER_TOPK, tt), I32),
            pltpu.VMEM((PEER_E, tt), F32), pltpu.VMEM((PEER_E, tt), I32),
            pltpu.VMEM((PEER_E, tt), F32),
        ],
        compiler_params=pltpu.CompilerParams(dimension_semantics=("arbitrary",)),
        name="topk",
    )(q, keys)


def _gather_group(tab_ref, e_ref, t, g, scr):
    ref = scr[g % N_SCR]
    base = (g // N_SCR) * GROUP_ROWS
    for j in range(GROUP):
        e4 = pl.multiple_of(e_ref[t, g * GROUP + j], HALF_ROWS)
        ref[pl.ds(base + j, HALF_ROWS, stride=GROUP), :] = tab_ref[pl.ds(e4, HALF_ROWS), :]
    return [ref[base + s * GROUP:base + (s + 1) * GROUP, :] for s in range(HALF_ROWS)]


def _unpack(w):
    return pltpu.bitcast(w << 16, F32), pltpu.bitcast(w & jnp.uint32(0xFFFF0000), F32)


def _row_sums_on_lanes(m, eye, ones):
    hi = m.astype(BF16)
    lo = (m - hi.astype(F32)).astype(BF16)
    sums = jnp.dot(hi, ones, preferred_element_type=F32) + jnp.dot(lo, ones, preferred_element_type=F32)
    return jnp.sum(sums * eye, axis=0, keepdims=True)


def _peer_u_kernel(e_ref, h_ref, g_ref, tab_ref, o_ref, part_s, *scr, tt):
    eye = (lax.broadcasted_iota(I32, (PEER_E, LANES), 0) == lax.broadcasted_iota(I32, (PEER_E, LANES), 1)).astype(F32)
    ones = jnp.ones((LANES, LANES), BF16)

    def finish(t, slot):
        act = _row_sums_on_lanes(part_s[slot], eye, ones)
        o_ref[pl.ds(t, 1), :] = g_ref[pl.ds(t, 1), :] * _gelu(act)

    part_s[1] = jnp.zeros((PEER_E, LANES), F32)

    def tok(t, carry):
        finish(jnp.maximum(t - 1, 0), (t + 1) & 1)
        h_lo = [jnp.broadcast_to(h_ref[t, s:s + 1, :], (GROUP, LANES)) for s in range(HALF_ROWS)]
        h_hi = [jnp.broadcast_to(h_ref[t, HALF_ROWS + s:HALF_ROWS + s + 1, :], (GROUP, LANES))
                for s in range(HALF_ROWS)]
        for g in range(N_GROUPS):
            r = None
            for s, tile in enumerate(_gather_group(tab_ref, e_ref, t, g, scr)):
                lo, hi = _unpack(tile)
                p = lo * h_lo[s] + hi * h_hi[s]
                r = p if r is None else r + p
            part_s[t & 1, g * GROUP:(g + 1) * GROUP, :] = r
        return carry

    lax.fori_loop(0, tt, tok, 0)
    finish(tt - 1, (tt - 1) & 1)


def _peer_u(e4, h3, gates, tab, n, tt):
    row = lambda i: (i, 0)
    return pl.pallas_call(
        functools.partial(_peer_u_kernel, tt=tt),
        grid=(n // tt,),
        in_specs=[
            pl.BlockSpec((tt, PEER_E), row, memory_space=pltpu.SMEM),
            pl.BlockSpec((tt, SUBLANES, LANES), lambda i: (i, 0, 0)),
            pl.BlockSpec((tt, PEER_E), row),
            pl.BlockSpec(tab.shape, lambda i: (0, 0), pipeline_mode=pl.Buffered(1)),
        ],
        out_specs=pl.BlockSpec((tt, PEER_E), row),
        out_shape=jax.ShapeDtypeStruct((n, PEER_E), F32),
        scratch_shapes=[pltpu.VMEM((2, PEER_E, LANES), F32)]
        + [pltpu.VMEM((N_GROUPS // N_SCR * GROUP_ROWS, LANES), U32) for _ in range(N_SCR)],
        compiler_params=pltpu.CompilerParams(
            dimension_semantics=("arbitrary",), vmem_limit_bytes=VMEM_TABLE_LIMIT),
        name="peer_u",
    )(e4, h3, gates, tab)


def _peer_v_kernel(e_ref, a_ref, tab_ref, o_ref, w_s, *scr, tt):
    eye = (lax.broadcasted_iota(I32, (PEER_E, LANES), 0) == lax.broadcasted_iota(I32, (PEER_E, LANES), 1)).astype(F32)
    ones = jnp.ones((LANES, LANES), BF16)

    def spread(t):
        m = eye * a_ref[pl.ds(t, 1), :]
        hi = m.astype(BF16)
        lo = (m - hi.astype(F32)).astype(BF16)
        return jnp.dot(hi, ones, preferred_element_type=F32) + jnp.dot(lo, ones, preferred_element_type=F32)

    w_s[0] = spread(0)

    def tok(t, carry):
        w_next = spread(jnp.minimum(t + 1, tt - 1))
        acc_lo = [None] * HALF_ROWS
        acc_hi = [None] * HALF_ROWS
        for g in range(N_GROUPS):
            wg = w_s[t & 1, g * GROUP:(g + 1) * GROUP, :]
            for s, tile in enumerate(_gather_group(tab_ref, e_ref, t, g, scr)):
                lo, hi = _unpack(tile)
                acc_lo[s] = lo * wg if g == 0 else acc_lo[s] + lo * wg
                acc_hi[s] = hi * wg if g == 0 else acc_hi[s] + hi * wg
        for s in range(HALF_ROWS):
            o_ref[t, s:s + 1, :] = jnp.sum(acc_lo[s], axis=0, keepdims=True)
            o_ref[t, HALF_ROWS + s:HALF_ROWS + s + 1, :] = jnp.sum(acc_hi[s], axis=0, keepdims=True)
        w_s[(t + 1) & 1] = w_next
        return carry

    lax.fori_loop(0, tt, tok, 0)


def _peer_v(e4, act, tab, n, tt):
    row = lambda i: (i, 0)
    return pl.pallas_call(
        functools.partial(_peer_v_kernel, tt=tt),
        grid=(n // tt,),
        in_specs=[
            pl.BlockSpec((tt, PEER_E), row, memory_space=pltpu.SMEM),
            pl.BlockSpec((tt, PEER_E), row),
            pl.BlockSpec(tab.shape, lambda i: (0, 0), pipeline_mode=pl.Buffered(1)),
        ],
        out_specs=pl.BlockSpec((tt, SUBLANES, LANES), lambda i: (i, 0, 0)),
        out_shape=jax.ShapeDtypeStruct((n, SUBLANES, LANES), F32),
        scratch_shapes=[pltpu.VMEM((2, PEER_E, LANES), F32)]
        + [pltpu.VMEM((N_GROUPS // N_SCR * GROUP_ROWS, LANES), U32) for _ in range(N_SCR)],
        compiler_params=pltpu.CompilerParams(
            dimension_semantics=("arbitrary",), vmem_limit_bytes=VMEM_TABLE_LIMIT),
        name="peer_v",
    )(e4, act, tab)


def _final_kernel(x_ref, y_ref, mod_ref, nw_ref, o_ref):
    b = pl.program_id(0)
    gt2 = mod_ref[pl.ds(b, 1), :][:, 5 * D_MODEL:6 * D_MODEL]
    o_ref[...] = x_ref[...] + gt2 * _rms(y_ref[...], nw_ref[...])


def _final(x1, y2, mod, nw, batch, seq, tm):
    nt = seq // tm
    row = lambda b, t: (b * nt + t, 0)
    const = lambda b, t: (0, 0)
    return pl.pallas_call(
        _final_kernel,
        grid=(batch, nt),
        in_specs=[
            pl.BlockSpec((tm, D_MODEL), row),
            pl.BlockSpec((tm, D_MODEL), row),
            pl.BlockSpec(mod.shape, const),
            pl.BlockSpec((1, D_MODEL), const),
        ],
        out_specs=pl.BlockSpec((tm, D_MODEL), row),
        out_shape=jax.ShapeDtypeStruct((batch * seq, D_MODEL), F32),
        compiler_params=pltpu.CompilerParams(dimension_semantics=("arbitrary", "arbitrary")),
        name="final",
    )(x1, y2, mod, nw)


def _pack_table(tab):
    e, d = tab.shape
    bits = lax.bitcast_convert_type(tab.astype(BF16), jnp.uint16).astype(U32)
    return (bits[:, :d // 2] | (bits[:, d // 2:] << 16)).reshape(e * HALF_ROWS, LANES)


def _tile(seq, cap):
    t = min(seq, cap)
    assert seq % t == 0
    return t


def kernel(x, c, w_ada, b_ada, norm_pre_mix, norm_post_mix, w_in_mix, sc_conv_w, dn_conv_w, dn_a_log,
           dn_dt_bias, dn_out_norm, w_sc_out, w_dn_out, w_mix_out, norm_pre_ffn, norm_post_ffn, peer_w_q,
           peer_sub_keys, peer_u, peer_v):
    batch, seq, d = x.shape
    assert d == D_MODEL and batch <= SUBLANES and seq % DN_CHUNK == 0
    depth = w_ada.shape[0]
    n = batch * seq
    x2 = x.reshape(n, d)
    c_pad = jnp.pad(c, ((0, SUBLANES - batch), (0, 0)))
    tab_u = _pack_table(peer_u)
    tab_v = _pack_table(peer_v)
    rep = lambda v: jnp.repeat(v, DN_DK, axis=-1)
    for l in range(depth):
        mod = _ada(c_pad, w_ada[l], b_ada[l][None, :])
        w = w_in_mix[l]
        o_z = C_SC + C_QKV
        o_a = o_z + C_Z
        w_pad = jnp.concatenate(
            [w[:, :o_a], rep(w[:, o_a:o_a + DN_HEADS]), rep(w[:, o_a + DN_HEADS:o_a + 2 * DN_HEADS]),
             w[:, o_a + 2 * DN_HEADS:]], axis=1).astype(BF16)
        tm = _tile(seq, 512)
        parts = _inproj(x2, mod, norm_pre_mix[l][None, :], w_pad, batch, seq, tm)
        weights = [
            sc_conv_w[l], dn_conv_w[l], rep(dn_a_log[l])[None, :], rep(dn_dt_bias[l])[None, :],
            jnp.tile(dn_out_norm[l], DN_HEADS)[None, :], w_sc_out[l].astype(BF16), w_dn_out[l].astype(BF16),
            w_mix_out[l].astype(BF16), norm_post_mix[l][None, :],
        ]
        x1 = _mixer(parts, x2, mod, weights, batch, seq, _tile(seq, 512))
        h2, q = _qproj(x1, mod, norm_pre_ffn[l][None, :], peer_w_q[l].astype(BF16), batch, seq, tm)
        keys = peer_sub_keys[l].reshape(2 * PEER_HEADS, PEER_NKEYS, PEER_HALF)
        e4, gates = _topk(q, keys, n, _tile(n, LANES))
        tt = _tile(n, 64)
        act = _peer_u(e4, h2.reshape(n, SUBLANES, LANES), gates, tab_u, n, tt)
        y2 = _peer_v(e4, act, tab_v, n, tt).reshape(n, d)
        x2 = _final(x1, y2, mod, norm_post_ffn[l][None, :], batch, seq, tm)
    return x2.reshape(batch, seq, d)
```

```python
import functools

import jax
import jax.numpy as jnp
from jax import lax
from jax.experimental import pallas as pl
from jax.experimental.pallas import tpu as pltpu

F32 = jnp.float32
BF16 = jnp.bfloat16
I32 = jnp.int32
U32 = jnp.uint32

D_MODEL = 1024
SC_WIDTH = 512
SC_KERNEL = 3
DN_HEADS = 8
DN_DK = 64
DN_KEY = DN_HEADS * DN_DK
DN_VAL = DN_KEY
DN_CONV = 4
DN_CHUNK = 64
PEER_HEADS = 8
PEER_NKEYS = 128
PEER_HALF = 128
PEER_QDIM = 256
PEER_TOPK = 16
PEER_E = PEER_HEADS * PEER_TOPK
NORM_EPS = 1e-6

LANES = 128
SUBLANES = 8
HALO = SUBLANES
PAIR = 2 * DN_DK
N_PAIRS = DN_HEADS // 2
INV_TERMS = 2
PREP_CHUNKS = 2
HALF_ROWS = SUBLANES // 2
GROUP = SUBLANES
GROUP_ROWS = HALF_ROWS * GROUP
N_GROUPS = PEER_E // GROUP
N_SCR = 4
STAGE_TOKENS = 8
VMEM_TABLE_LIMIT = 56 * 1024 * 1024
VMEM_MIXER_LIMIT = 56 * 1024 * 1024

C_SC = 3 * SC_WIDTH
C_QKV = 2 * DN_KEY + DN_VAL
C_Z = DN_VAL
C_A = DN_KEY
C_B = DN_KEY
C_G = 2 * D_MODEL
C_ALL = C_SC + C_QKV + C_Z + C_A + C_B + C_G


def _silu(x):
    return x * jax.nn.sigmoid(x)


def _softplus(x):
    return jnp.maximum(x, 0.0) + jnp.log(1.0 + jnp.exp(-jnp.abs(x)))


def _gelu(x):
    return 0.5 * x * (1.0 + lax.erf(x * (2.0 ** -0.5)))


def _rms(x, w):
    return x * lax.rsqrt(jnp.mean(x * x, axis=-1, keepdims=True) + NORM_EPS) * w


def _bdot(a, b):
    return jnp.dot(a.astype(BF16), b.astype(BF16), preferred_element_type=F32)


def _xdot(a, b):
    return jnp.dot(a, b, preferred_element_type=F32, precision=lax.Precision.HIGHEST)


def _split(x, n):
    terms = []
    for i in range(n):
        t = x.astype(BF16)
        terms.append(t)
        if i + 1 < n:
            x = x - t.astype(F32)
    return terms


NN_DIMS = (((1,), (0,)), ((), ()))
NT_DIMS = (((1,), (1,)), ((), ()))


def _sdot(a_terms, b_terms, dims=NN_DIMS):
    order = max(len(a_terms), len(b_terms))
    out = None
    for i, a in enumerate(a_terms):
        for j, b in enumerate(b_terms):
            if i + j < order:
                p = lax.dot_general(a, b, dims, preferred_element_type=F32)
                out = p if out is None else out + p
    return out


def _ada_kernel(c_ref, w_ref, b_ref, o_ref):
    o_ref[...] = _bdot(_silu(c_ref[...]), w_ref[...]) + b_ref[...]


def _ada(c_pad, w_ada, b_ada):
    n_out = w_ada.shape[1]
    tn = 1024
    return pl.pallas_call(
        _ada_kernel,
        grid=(n_out // tn,),
        in_specs=[
            pl.BlockSpec((SUBLANES, D_MODEL), lambda j: (0, 0)),
            pl.BlockSpec((D_MODEL, tn), lambda j: (0, j)),
            pl.BlockSpec((1, tn), lambda j: (0, j)),
        ],
        out_specs=pl.BlockSpec((SUBLANES, tn), lambda j: (0, j)),
        out_shape=jax.ShapeDtypeStruct((SUBLANES, n_out), F32),
        name="ada",
    )(c_pad, w_ada, b_ada)


def _inproj_kernel(x_ref, mod_ref, nw_ref, w_ref, sc_ref, qkv_ref, z_ref, a_ref, b_ref, g_ref):
    b = pl.program_id(0)
    mod = mod_ref[pl.ds(b, 1), :]
    sh1 = mod[:, 0:D_MODEL]
    sc1 = mod[:, D_MODEL:2 * D_MODEL]
    h = (_rms(x_ref[...], nw_ref[...]) * (1.0 + sc1) + sh1).astype(BF16)
    col = 0
    for ref, width in ((sc_ref, C_SC), (qkv_ref, C_QKV), (z_ref, C_Z), (a_ref, C_A), (b_ref, C_B), (g_ref, C_G)):
        for j in range(0, width, 512):
            ref[:, j:j + 512] = jnp.dot(h, w_ref[:, col + j:col + j + 512],
                                        preferred_element_type=F32).astype(ref.dtype)
        col += width


def _inproj(x2, mod, nw, w_pad, batch, seq, tm):
    nt = seq // tm
    row = lambda b, t: (b * nt + t, 0)
    const = lambda b, t: (0, 0)
    n = batch * seq
    outs = [(C_SC, BF16), (C_QKV, BF16), (C_Z, BF16), (C_A, F32), (C_B, F32), (C_G, BF16)]
    return pl.pallas_call(
        _inproj_kernel,
        grid=(batch, nt),
        in_specs=[
            pl.BlockSpec((tm, D_MODEL), row),
            pl.BlockSpec(mod.shape, const),
            pl.BlockSpec((1, D_MODEL), const),
            pl.BlockSpec((D_MODEL, C_ALL), const, pipeline_mode=pl.Buffered(1)),
        ],
        out_specs=[pl.BlockSpec((tm, w), row) for w, _ in outs],
        out_shape=[jax.ShapeDtypeStruct((n, w), dt) for w, dt in outs],
        compiler_params=pltpu.CompilerParams(
            dimension_semantics=("arbitrary", "arbitrary"), vmem_limit_bytes=VMEM_MIXER_LIMIT),
        name="inproj",
    )(x2, mod, nw, w_pad)


def _pair_consts():
    r = lax.broadcasted_iota(I32, (PAIR, PAIR), 0)
    c = lax.broadcasted_iota(I32, (PAIR, PAIR), 1)
    bd = ((r >> 6) == (c >> 6)).astype(F32)
    i = lax.broadcasted_iota(I32, (DN_CHUNK, PAIR), 0)
    j = lax.broadcasted_iota(I32, (DN_CHUNK, PAIR), 1) & (DN_DK - 1)
    return bd, i >= j, i > j, (i == j).astype(F32)


def _stack_bd(y, bd):
    return jnp.concatenate([y, y], axis=0) * bd


def _mixer_kernel(sc_ref, qkv_ref, z_ref, a_ref, b_ref, g_ref, x_ref, mod_ref,
                  scw_ref, dnw_ref, alog_ref, dtb_ref, onw_ref, wsc_ref, wdn_ref, wmix_ref, pnw_ref,
                  o_ref,
                  scx_s, qkvx_s, q_s, k_s, v_s, g_s, beta_s, o_s, gcum_s, aqk_s, u_s, wk_s, qg_s, kdec_s,
                  state_s, *, tt):
    b = pl.program_id(0)
    t = pl.program_id(1)

    @pl.when(t == 0)
    def _():
        scx_s[0:HALO, :] = jnp.zeros((HALO, SC_WIDTH), F32)
        qkvx_s[0:HALO, :] = jnp.zeros((HALO, C_QKV), F32)
        state_s[...] = jnp.zeros(state_s.shape, F32)

    sc = sc_ref[...].astype(F32)
    scx_s[HALO:HALO + tt, :] = sc[:, SC_WIDTH:2 * SC_WIDTH] * sc[:, 2 * SC_WIDTH:]
    conv = scw_ref[0:1, :] * scx_s[pl.ds(HALO - (SC_KERNEL - 1), tt), :]
    for kk in range(1, SC_KERNEL):
        conv = conv + scw_ref[kk:kk + 1, :] * scx_s[pl.ds(HALO - (SC_KERNEL - 1) + kk, tt), :]
    y_sc = _bdot(sc[:, 0:SC_WIDTH] * conv, wsc_ref[...])
    scx_s[0:HALO, :] = scx_s[tt:tt + HALO, :]

    qkvx_s[HALO:HALO + tt, :] = qkv_ref[...].astype(F32)
    cq = dnw_ref[0:1, :] * qkvx_s[pl.ds(HALO - (DN_CONV - 1), tt), :]
    for kk in range(1, DN_CONV):
        cq = cq + dnw_ref[kk:kk + 1, :] * qkvx_s[pl.ds(HALO - (DN_CONV - 1) + kk, tt), :]
    qkvx_s[0:HALO, :] = qkvx_s[tt:tt + HALO, :]
    cq = _silu(cq)
    r512 = lax.broadcasted_iota(I32, (DN_KEY, DN_KEY), 0) >> 6
    c512 = lax.broadcasted_iota(I32, (DN_KEY, DN_KEY), 1) >> 6
    head_ones = [(r512 == c512).astype(BF16)]
    q = cq[:, 0:DN_KEY]
    k = cq[:, DN_KEY:2 * DN_KEY]
    q_s[...] = q * lax.rsqrt(_sdot(_split(q * q, 2), head_ones) + NORM_EPS) * (DN_DK ** -0.5)
    k_s[...] = k * lax.rsqrt(_sdot(_split(k * k, 2), head_ones) + NORM_EPS)
    v_s[...] = cq[:, 2 * DN_KEY:]
    beta_s[...] = jax.nn.sigmoid(b_ref[...])
    g_s[...] = -jnp.exp(alog_ref[...]) * _softplus(a_ref[...] + dtb_ref[...])

    bd, incl, strict, eye = _pair_consts()
    bd_b = bd.astype(BF16)
    ri = lax.broadcasted_iota(I32, (DN_CHUNK, DN_CHUNK), 0)
    ci = lax.broadcasted_iota(I32, (DN_CHUNK, DN_CHUNK), 1)
    lt = [(ci <= ri).astype(BF16)]
    nt = (((1,), (1,)), ((), ()))

    def stack_terms(y):
        return [jnp.concatenate([t_, t_], axis=0) * bd_b for t_ in _split(y, INV_TERMS)]

    pairs = [slice(p * PAIR, (p + 1) * PAIR) for p in range(N_PAIRS)]

    def prepare(c, carry):
        prob = []
        for cc in range(PREP_CHUNKS):
            rows = pl.ds(pl.multiple_of((c * PREP_CHUNKS + cc) * DN_CHUNK, DN_CHUNK), DN_CHUNK)
            gcum_all = _sdot(lt, _split(g_s[rows, :], 3))
            gcum_s[rows, :] = gcum_all
            prob += [(rows, sl, gcum_all[:, sl]) for sl in pairs]
        pws, invs, rest = [], [], []
        for rows, sl, gc in prob:
            qp = q_s[rows, sl]
            kp = k_s[rows, sl]
            bp = beta_s[rows, sl]
            grow = jnp.sum(gc * eye, axis=0, keepdims=True)
            dec = jnp.where(incl, jnp.exp(jnp.where(incl, gc - grow, 0.0)), 0.0)
            eg = jnp.exp(gc)
            kb = kp * bp
            kbig = _stack_bd(kp, bd).astype(BF16)
            kk_s = lax.dot_general(kb.astype(BF16), kbig, nt, preferred_element_type=F32)
            qk_s = lax.dot_general(qp.astype(BF16), kbig, nt, preferred_element_type=F32)
            aqk_s[rows, sl] = qk_s * dec
            qg_s[rows, sl] = qp * eg
            kdec_s[rows, sl] = kp * jnp.exp(gc[DN_CHUNK - 1:DN_CHUNK, :] - gc)
            pw = -(kk_s * jnp.where(strict, dec, 0.0))
            pws.append(pw)
            invs.append(eye + pw)
            rest.append((v_s[rows, sl] * bp, kb * eg))
        for _ in range(5):
            pws = [_sdot(_split(pw, INV_TERMS), stack_terms(pw)) for pw in pws]
            invs = [inv + _sdot(_split(inv, INV_TERMS), stack_terms(pw)) for inv, pw in zip(invs, pws)]
        for (rows, sl, _), inv, (vb, kbg) in zip(prob, invs, rest):
            u_s[rows, sl] = _bdot(inv, _stack_bd(vb, bd))
            wk_s[rows, sl] = _bdot(inv, _stack_bd(kbg, bd))
        return carry

    def advance(c, carry):
        rows = pl.ds(pl.multiple_of(c * DN_CHUNK, DN_CHUNK), DN_CHUNK)
        tail = pl.ds(pl.multiple_of(c * DN_CHUNK + DN_CHUNK - SUBLANES, SUBLANES), SUBLANES)
        sts = [state_s[p] for p in range(N_PAIRS)]
        wss = [_bdot(jnp.concatenate([wk_s[rows, sl], qg_s[rows, sl]], axis=0), st) for sl, st in zip(pairs, sts)]
        vnews = [u_s[rows, sl] - ws[0:DN_CHUNK] for sl, ws in zip(pairs, wss)]
        for p, (sl, st, ws, vnew) in enumerate(zip(pairs, sts, wss, vnews)):
            o_s[rows, sl] = ws[DN_CHUNK:] + _bdot(aqk_s[rows, sl], _stack_bd(vnew, bd))
            glast = gcum_s[tail, sl][SUBLANES - 1:SUBLANES, :]
            state_s[p] = st * jnp.exp(glast) + bd * _bdot(kdec_s[rows, sl].T, vnew)
        return carry

    lax.fori_loop(0, tt // (DN_CHUNK * PREP_CHUNKS), prepare, 0)
    lax.fori_loop(0, tt // DN_CHUNK, advance, 0)

    o = o_s[...]
    ms = _sdot(_split(o * o, 2), head_ones) * (1.0 / DN_DK)
    og = o * lax.rsqrt(ms + NORM_EPS) * onw_ref[...] * _silu(z_ref[...].astype(F32))
    y_dn = _bdot(og, wdn_ref[...])
    gates = g_ref[...].astype(F32)
    merged = jax.nn.sigmoid(gates[:, 0:D_MODEL]) * y_sc + jax.nn.sigmoid(gates[:, D_MODEL:]) * y_dn
    y = _bdot(merged, wmix_ref[...])
    gt1 = mod_ref[pl.ds(b, 1), :][:, 2 * D_MODEL:3 * D_MODEL]
    o_ref[...] = x_ref[...] + gt1 * _rms(y, pnw_ref[...])


def _mixer(parts, x2, mod, weights, batch, seq, tt):
    nt = seq // tt
    row = lambda b, t: (b * nt + t, 0)
    const2 = lambda b, t: (0, 0)
    sc, qkv, z, a, bb, g = parts
    data = [(sc, C_SC), (qkv, C_QKV), (z, C_Z), (a, C_A), (bb, C_B), (g, C_G), (x2, D_MODEL)]
    in_specs = [pl.BlockSpec((tt, w), row) for _, w in data]
    in_specs.append(pl.BlockSpec(mod.shape, const2))
    in_specs += [pl.BlockSpec(wt.shape, const2) for wt in weights]
    scratch = [
        pltpu.VMEM((tt + HALO, SC_WIDTH), F32),
        pltpu.VMEM((tt + HALO, C_QKV), F32),
        pltpu.VMEM((tt, DN_KEY), F32),
        pltpu.VMEM((tt, DN_KEY), F32),
        pltpu.VMEM((tt, DN_VAL), F32),
        pltpu.VMEM((tt, DN_KEY), F32),
        pltpu.VMEM((tt, DN_KEY), F32),
        pltpu.VMEM((tt, DN_VAL), F32),
    ] + [pltpu.VMEM((tt, DN_KEY), F32) for _ in range(6)] + [
        pltpu.VMEM((N_PAIRS, PAIR, PAIR), F32),
    ]
    return pl.pallas_call(
        functools.partial(_mixer_kernel, tt=tt),
        grid=(batch, nt),
        in_specs=in_specs,
        out_specs=pl.BlockSpec((tt, D_MODEL), row),
        out_shape=jax.ShapeDtypeStruct((batch * seq, D_MODEL), F32),
        scratch_shapes=scratch,
        compiler_params=pltpu.CompilerParams(
            dimension_semantics=("arbitrary", "arbitrary"), vmem_limit_bytes=VMEM_MIXER_LIMIT),
        name="mixer",
    )(*[d for d, _ in data], mod, *weights)


def _qproj_kernel(x_ref, mod_ref, nw_ref, w_ref, h_ref, q_ref):
    b = pl.program_id(0)
    mod = mod_ref[pl.ds(b, 1), :]
    sh2 = mod[:, 3 * D_MODEL:4 * D_MODEL]
    sc2 = mod[:, 4 * D_MODEL:5 * D_MODEL]
    h = _rms(x_ref[...], nw_ref[...]) * (1.0 + sc2) + sh2
    h_ref[...] = h
    q_ref[...] = _bdot(h, w_ref[...])


def _qproj(x1, mod, nw, wq, batch, seq, tm):
    nt = seq // tm
    row = lambda b, t: (b * nt + t, 0)
    const = lambda b, t: (0, 0)
    n = batch * seq
    nq = wq.shape[1]
    return pl.pallas_call(
        _qproj_kernel,
        grid=(batch, nt),
        in_specs=[
            pl.BlockSpec((tm, D_MODEL), row),
            pl.BlockSpec(mod.shape, const),
            pl.BlockSpec((1, D_MODEL), const),
            pl.BlockSpec(wq.shape, const),
        ],
        out_specs=[pl.BlockSpec((tm, D_MODEL), row), pl.BlockSpec((tm, nq), row)],
        out_shape=[jax.ShapeDtypeStruct((n, D_MODEL), F32), jax.ShapeDtypeStruct((n, nq), F32)],
        compiler_params=pltpu.CompilerParams(dimension_semantics=("arbitrary", "arbitrary")),
        name="qproj",
    )(x1, mod, nw, wq)


def _topk_rows(vals, idx_payload, k, v_out, i_out, row0):
    n_rows = vals.shape[0]
    pos_iota = lax.broadcasted_iota(I32, vals.shape, 0)
    for it in range(k):
        m = jnp.max(vals, axis=0, keepdims=True)
        pos = jnp.min(jnp.where(vals == m, pos_iota, n_rows), axis=0, keepdims=True)
        hit = pos_iota == pos
        v_out[row0 + it:row0 + it + 1, :] = m
        if idx_payload is None:
            i_out[row0 + it:row0 + it + 1, :] = pos
        else:
            i_out[row0 + it:row0 + it + 1, :] = jnp.sum(jnp.where(hit, idx_payload, 0), axis=0, keepdims=True)
        vals = jnp.where(hit, -jnp.inf, vals)


def _topk_kernel(q_ref, keys_ref, e_ref, g_ref, v1_s, i1_s, v2_s, i2_s, ts_s, ei_s, gate_s, *, tt):
    k = PEER_TOPK
    for h in range(PEER_HEADS):
        for p, (v_s, i_s) in enumerate(((v1_s, i1_s), (v2_s, i2_s))):
            col = (2 * h + p) * PEER_HALF
            st = _sdot(_split(keys_ref[2 * h + p], 2), _split(q_ref[:, col:col + PEER_HALF], 2), NT_DIMS)
            _topk_rows(st, None, k, v_s, i_s, 0)
        v2h = v2_s[0:SUBLANES, :]
        i2h = i2_s[0:SUBLANES, :]
        cv = jnp.concatenate([v1_s[0:1, :] + v2_s[...]] + [v1_s[a:a + 1, :] + v2h for a in range(1, k)], axis=0)
        ce = jnp.concatenate([i1_s[0:1, :] * PEER_NKEYS + i2_s[...]]
                             + [i1_s[a:a + 1, :] * PEER_NKEYS + i2h for a in range(1, k)], axis=0)
        _topk_rows(cv, ce, k, ts_s, ei_s, h * k)
        ts = ts_s[h * k:(h + 1) * k, :]
        e = jnp.exp(ts - jnp.max(ts, axis=0, keepdims=True))
        gate_s[h * k:(h + 1) * k, :] = e / jnp.sum(e, axis=0, keepdims=True)
    e_ref[...] = ei_s[...].T * HALF_ROWS
    g_ref[...] = gate_s[...].T


def _topk(q, keys, n, tt):
    row = lambda i: (i, 0)
    return pl.pallas_call(
        functools.partial(_topk_kernel, tt=tt),
        grid=(n // tt,),
        in_specs=[
            pl.BlockSpec((tt, q.shape[1]), row),
            pl.BlockSpec(keys.shape, lambda i: (0, 0, 0)),
        ],
        out_specs=[pl.BlockSpec((tt, PEER_E), row)] * 2,
        out_shape=[jax.ShapeDtypeStruct((n, PEER_E), I32), jax.ShapeDtypeStruct((n, PEER_E), F32)],
        scratch_shapes=[
            pltpu.VMEM((PEER_TOPK, tt), F32), pltpu.VMEM((PEER_TOPK, tt), I32),
            pltpu.VMEM((PEER_TOPK, tt), F32), pltpu.VMEM((PEER_TOPK, tt), I32),
            pltpu.VMEM((PEER_E, tt), F32), pltpu.VMEM((PEER_E, tt), I32),
            pltpu.VMEM((PEER_E, tt), F32),
        ],
        compiler_params=pltpu.CompilerParams(dimension_semantics=("arbitrary",)),
        name="topk",
    )(q, keys)


def _staged_tokens(e_ref, idx_s, sem, tt, body):
    n_groups = tt // STAGE_TOKENS
    assert n_groups % 2 == 0

    def copy(grp, slot):
        return pltpu.make_async_copy(e_ref.at[pl.ds(grp * STAGE_TOKENS, STAGE_TOKENS)], idx_s.at[slot], sem.at[slot])

    copy(0, 0).start()

    def two_groups(i, carry):
        for slot in range(2):
            grp = 2 * i + slot
            copy(jnp.minimum(grp + 1, n_groups - 1), 1 - slot).start()
            copy(grp, slot).wait()
            for tl in range(STAGE_TOKENS):
                body(grp * STAGE_TOKENS + tl, tl, lambda k, slot=slot, tl=tl: idx_s[slot, tl, k])
        return carry

    lax.fori_loop(0, n_groups // 2, two_groups, 0)
    copy(n_groups - 1, 0).wait()


def _stage_scratch():
    return [pltpu.SMEM((2, STAGE_TOKENS, PEER_E), I32), pltpu.SemaphoreType.DMA((2,))]


def _gather_group(tab_ref, idx, g, scr):
    ref = scr[g % N_SCR]
    base = (g // N_SCR) * GROUP_ROWS
    for j in range(GROUP):
        e4 = pl.multiple_of(idx(g * GROUP + j), HALF_ROWS)
        ref[pl.ds(base + j, HALF_ROWS, stride=GROUP), :] = tab_ref[pl.ds(e4, HALF_ROWS), :]
    return [ref[base + s * GROUP:base + (s + 1) * GROUP, :] for s in range(HALF_ROWS)]


def _unpack(w):
    return pltpu.bitcast(w << 16, F32), pltpu.bitcast(w & jnp.uint32(0xFFFF0000), F32)


def _row_sums_on_lanes(m, eye, ones):
    hi = m.astype(BF16)
    lo = (m - hi.astype(F32)).astype(BF16)
    sums = jnp.dot(hi, ones, preferred_element_type=F32) + jnp.dot(lo, ones, preferred_element_type=F32)
    return jnp.sum(sums * eye, axis=0, keepdims=True)


def _peer_u_kernel(e_ref, h_ref, g_ref, tab_ref, o_ref, part_s, idx_s, sem, *scr, tt):
    eye = (lax.broadcasted_iota(I32, (PEER_E, LANES), 0) == lax.broadcasted_iota(I32, (PEER_E, LANES), 1)).astype(F32)
    ones = jnp.ones((LANES, LANES), BF16)

    def finish(t, slot):
        act = _row_sums_on_lanes(part_s[slot], eye, ones)
        o_ref[pl.ds(t, 1), :] = g_ref[pl.ds(t, 1), :] * _gelu(act)

    part_s[1] = jnp.zeros((PEER_E, LANES), F32)

    def tok(t, tl, idx):
        finish(jnp.maximum(t - 1, 0), (tl + 1) & 1)
        h_lo = [jnp.broadcast_to(h_ref[t, s:s + 1, :], (GROUP, LANES)) for s in range(HALF_ROWS)]
        h_hi = [jnp.broadcast_to(h_ref[t, HALF_ROWS + s:HALF_ROWS + s + 1, :], (GROUP, LANES))
                for s in range(HALF_ROWS)]
        for g in range(N_GROUPS):
            r = None
            for s, tile in enumerate(_gather_group(tab_ref, idx, g, scr)):
                lo, hi = _unpack(tile)
                p = lo * h_lo[s] + hi * h_hi[s]
                r = p if r is None else r + p
            part_s[tl & 1, g * GROUP:(g + 1) * GROUP, :] = r

    _staged_tokens(e_ref, idx_s, sem, tt, tok)
    finish(tt - 1, (tt - 1) & 1)


def _peer_u(e4, h3, gates, tab, n, tt):
    row = lambda i: (i, 0)
    return pl.pallas_call(
        functools.partial(_peer_u_kernel, tt=tt),
        grid=(n // tt,),
        in_specs=[
            pl.BlockSpec((tt, PEER_E), row),
            pl.BlockSpec((tt, SUBLANES, LANES), lambda i: (i, 0, 0)),
            pl.BlockSpec((tt, PEER_E), row),
            pl.BlockSpec(tab.shape, lambda i: (0, 0), pipeline_mode=pl.Buffered(1)),
        ],
        out_specs=pl.BlockSpec((tt, PEER_E), row),
        out_shape=jax.ShapeDtypeStruct((n, PEER_E), F32),
        scratch_shapes=[pltpu.VMEM((2, PEER_E, LANES), F32)] + _stage_scratch()
        + [pltpu.VMEM((N_GROUPS // N_SCR * GROUP_ROWS, LANES), U32) for _ in range(N_SCR)],
        compiler_params=pltpu.CompilerParams(
            dimension_semantics=("arbitrary",), vmem_limit_bytes=VMEM_TABLE_LIMIT),
        name="peer_u",
    )(e4, h3, gates, tab)


def _peer_v_kernel(e_ref, a_ref, tab_ref, o_ref, w_s, idx_s, sem, *scr, tt):
    eye = (lax.broadcasted_iota(I32, (PEER_E, LANES), 0) == lax.broadcasted_iota(I32, (PEER_E, LANES), 1)).astype(F32)
    ones = jnp.ones((LANES, LANES), BF16)

    def spread(t):
        m = eye * a_ref[pl.ds(t, 1), :]
        hi = m.astype(BF16)
        lo = (m - hi.astype(F32)).astype(BF16)
        return jnp.dot(hi, ones, preferred_element_type=F32) + jnp.dot(lo, ones, preferred_element_type=F32)

    w_s[0] = spread(0)

    def tok(t, tl, idx):
        w_next = spread(jnp.minimum(t + 1, tt - 1))
        acc_lo = [None] * HALF_ROWS
        acc_hi = [None] * HALF_ROWS
        for g in range(N_GROUPS):
            wg = w_s[tl & 1, g * GROUP:(g + 1) * GROUP, :]
            for s, tile in enumerate(_gather_group(tab_ref, idx, g, scr)):
                lo, hi = _unpack(tile)
                acc_lo[s] = lo * wg if g == 0 else acc_lo[s] + lo * wg
                acc_hi[s] = hi * wg if g == 0 else acc_hi[s] + hi * wg
        for s in range(HALF_ROWS):
            o_ref[t, s:s + 1, :] = jnp.sum(acc_lo[s], axis=0, keepdims=True)
            o_ref[t, HALF_ROWS + s:HALF_ROWS + s + 1, :] = jnp.sum(acc_hi[s], axis=0, keepdims=True)
        w_s[(tl + 1) & 1] = w_next

    _staged_tokens(e_ref, idx_s, sem, tt, tok)


def _peer_v(e4, act, tab, n, tt):
    row = lambda i: (i, 0)
    return pl.pallas_call(
        functools.partial(_peer_v_kernel, tt=tt),
        grid=(n // tt,),
        in_specs=[
            pl.BlockSpec((tt, PEER_E), row),
            pl.BlockSpec((tt, PEER_E), row),
            pl.BlockSpec(tab.shape, lambda i: (0, 0), pipeline_mode=pl.Buffered(1)),
        ],
        out_specs=pl.BlockSpec((tt, SUBLANES, LANES), lambda i: (i, 0, 0)),
        out_shape=jax.ShapeDtypeStruct((n, SUBLANES, LANES), F32),
        scratch_shapes=[pltpu.VMEM((2, PEER_E, LANES), F32)] + _stage_scratch()
        + [pltpu.VMEM((N_GROUPS // N_SCR * GROUP_ROWS, LANES), U32) for _ in range(N_SCR)],
        compiler_params=pltpu.CompilerParams(
            dimension_semantics=("arbitrary",), vmem_limit_bytes=VMEM_TABLE_LIMIT),
        name="peer_v",
    )(e4, act, tab)


def _final_kernel(x_ref, y_ref, mod_ref, nw_ref, o_ref):
    b = pl.program_id(0)
    gt2 = mod_ref[pl.ds(b, 1), :][:, 5 * D_MODEL:6 * D_MODEL]
    o_ref[...] = x_ref[...] + gt2 * _rms(y_ref[...], nw_ref[...])


def _final(x1, y2, mod, nw, batch, seq, tm):
    nt = seq // tm
    row = lambda b, t: (b * nt + t, 0)
    const = lambda b, t: (0, 0)
    return pl.pallas_call(
        _final_kernel,
        grid=(batch, nt),
        in_specs=[
            pl.BlockSpec((tm, D_MODEL), row),
            pl.BlockSpec((tm, D_MODEL), row),
            pl.BlockSpec(mod.shape, const),
            pl.BlockSpec((1, D_MODEL), const),
        ],
        out_specs=pl.BlockSpec((tm, D_MODEL), row),
        out_shape=jax.ShapeDtypeStruct((batch * seq, D_MODEL), F32),
        compiler_params=pltpu.CompilerParams(dimension_semantics=("arbitrary", "arbitrary")),
        name="final",
    )(x1, y2, mod, nw)


def _pack_table(tab):
    e, d = tab.shape
    bits = lax.bitcast_convert_type(tab.astype(BF16), jnp.uint16).astype(U32)
    return (bits[:, :d // 2] | (bits[:, d // 2:] << 16)).reshape(e * HALF_ROWS, LANES)


def _tile(seq, cap):
    t = min(seq, cap)
    assert seq % t == 0
    return t


def kernel(x, c, w_ada, b_ada, norm_pre_mix, norm_post_mix, w_in_mix, sc_conv_w, dn_conv_w, dn_a_log,
           dn_dt_bias, dn_out_norm, w_sc_out, w_dn_out, w_mix_out, norm_pre_ffn, norm_post_ffn, peer_w_q,
           peer_sub_keys, peer_u, peer_v):
    batch, seq, d = x.shape
    assert d == D_MODEL and batch <= SUBLANES and seq % DN_CHUNK == 0
    depth = w_ada.shape[0]
    n = batch * seq
    x2 = x.reshape(n, d)
    c_pad = jnp.pad(c, ((0, SUBLANES - batch), (0, 0)))
    tab_u = _pack_table(peer_u)
    tab_v = _pack_table(peer_v)
    rep = lambda v: jnp.repeat(v, DN_DK, axis=-1)
    for l in range(depth):
        mod = _ada(c_pad, w_ada[l], b_ada[l][None, :])
        w = w_in_mix[l]
        o_z = C_SC + C_QKV
        o_a = o_z + C_Z
        w_pad = jnp.concatenate(
            [w[:, :o_a], rep(w[:, o_a:o_a + DN_HEADS]), rep(w[:, o_a + DN_HEADS:o_a + 2 * DN_HEADS]),
             w[:, o_a + 2 * DN_HEADS:]], axis=1).astype(BF16)
        tm = _tile(seq, 512)
        parts = _inproj(x2, mod, norm_pre_mix[l][None, :], w_pad, batch, seq, tm)
        weights = [
            sc_conv_w[l], dn_conv_w[l], rep(dn_a_log[l])[None, :], rep(dn_dt_bias[l])[None, :],
            jnp.tile(dn_out_norm[l], DN_HEADS)[None, :], w_sc_out[l].astype(BF16), w_dn_out[l].astype(BF16),
            w_mix_out[l].astype(BF16), norm_post_mix[l][None, :],
        ]
        x1 = _mixer(parts, x2, mod, weights, batch, seq, _tile(seq, 512))
        h2, q = _qproj(x1, mod, norm_pre_ffn[l][None, :], peer_w_q[l].astype(BF16), batch, seq, tm)
        keys = peer_sub_keys[l].reshape(2 * PEER_HEADS, PEER_NKEYS, PEER_HALF)
        e4, gates = _topk(q, keys, n, _tile(n, LANES))
        tt = _tile(n, 64)
        act = _peer_u(e4, h2.reshape(n, SUBLANES, LANES), gates, tab_u, n, tt)
        y2 = _peer_v(e4, act, tab_v, n, tt).reshape(n, d)
        x2 = _final(x1, y2, mod, norm_post_ffn[l][None, :], batch, seq, tm)
    return x2.reshape(batch, seq, d)
```

```python
import functools

import jax
import jax.numpy as jnp
from jax import lax
from jax.experimental import pallas as pl
from jax.experimental.pallas import tpu as pltpu

F32 = jnp.float32
BF16 = jnp.bfloat16
I32 = jnp.int32
U32 = jnp.uint32

D_MODEL = 1024
SC_WIDTH = 512
SC_KERNEL = 3
DN_HEADS = 8
DN_DK = 64
DN_KEY = DN_HEADS * DN_DK
DN_VAL = DN_KEY
DN_CONV = 4
DN_CHUNK = 64
PEER_HEADS = 8
PEER_NKEYS = 128
PEER_HALF = 128
PEER_QDIM = 256
PEER_TOPK = 16
PEER_E = PEER_HEADS * PEER_TOPK
NORM_EPS = 1e-6

LANES = 128
SUBLANES = 8
HALO = SUBLANES
PAIR = 2 * DN_DK
N_PAIRS = DN_HEADS // 2
INV_TERMS = 2
PREP_CHUNKS = 2
HALF_ROWS = SUBLANES // 2
GROUP = SUBLANES
GROUP_ROWS = HALF_ROWS * GROUP
N_GROUPS = PEER_E // GROUP
N_SCR = 4
STAGE_TOKENS = 8
VMEM_TABLE_LIMIT = 56 * 1024 * 1024
VMEM_MIXER_LIMIT = 56 * 1024 * 1024

C_SC = 3 * SC_WIDTH
C_QKV = 2 * DN_KEY + DN_VAL
C_Z = DN_VAL
C_A = DN_KEY
C_B = DN_KEY
C_G = 2 * D_MODEL
C_ALL = C_SC + C_QKV + C_Z + C_A + C_B + C_G


def _silu(x):
    return x * jax.nn.sigmoid(x)


def _softplus(x):
    return jnp.maximum(x, 0.0) + jnp.log(1.0 + jnp.exp(-jnp.abs(x)))


def _gelu(x):
    return 0.5 * x * (1.0 + lax.erf(x * (2.0 ** -0.5)))


def _rms(x, w):
    return x * lax.rsqrt(jnp.mean(x * x, axis=-1, keepdims=True) + NORM_EPS) * w


def _bdot(a, b):
    return jnp.dot(a.astype(BF16), b.astype(BF16), preferred_element_type=F32)


def _xdot(a, b):
    return jnp.dot(a, b, preferred_element_type=F32, precision=lax.Precision.HIGHEST)


def _split(x, n):
    terms = []
    for i in range(n):
        t = x.astype(BF16)
        terms.append(t)
        if i + 1 < n:
            x = x - t.astype(F32)
    return terms


NN_DIMS = (((1,), (0,)), ((), ()))
NT_DIMS = (((1,), (1,)), ((), ()))


def _sdot(a_terms, b_terms, dims=NN_DIMS):
    order = max(len(a_terms), len(b_terms))
    out = None
    for i, a in enumerate(a_terms):
        for j, b in enumerate(b_terms):
            if i + j < order:
                p = lax.dot_general(a, b, dims, preferred_element_type=F32)
                out = p if out is None else out + p
    return out


def _ada_kernel(c_ref, w_ref, b_ref, o_ref):
    o_ref[...] = _bdot(_silu(c_ref[...]), w_ref[...]) + b_ref[...]


def _ada(c_pad, w_ada, b_ada):
    n_out = w_ada.shape[1]
    tn = 1024
    return pl.pallas_call(
        _ada_kernel,
        grid=(n_out // tn,),
        in_specs=[
            pl.BlockSpec((SUBLANES, D_MODEL), lambda j: (0, 0)),
            pl.BlockSpec((D_MODEL, tn), lambda j: (0, j)),
            pl.BlockSpec((1, tn), lambda j: (0, j)),
        ],
        out_specs=pl.BlockSpec((SUBLANES, tn), lambda j: (0, j)),
        out_shape=jax.ShapeDtypeStruct((SUBLANES, n_out), F32),
        name="ada",
    )(c_pad, w_ada, b_ada)


def _inproj_kernel(x_ref, mod_ref, nw_ref, w_ref, sc_ref, qkv_ref, z_ref, a_ref, b_ref, g_ref):
    b = pl.program_id(0)
    mod = mod_ref[pl.ds(b, 1), :]
    sh1 = mod[:, 0:D_MODEL]
    sc1 = mod[:, D_MODEL:2 * D_MODEL]
    h = (_rms(x_ref[...], nw_ref[...]) * (1.0 + sc1) + sh1).astype(BF16)
    col = 0
    for ref, width in ((sc_ref, C_SC), (qkv_ref, C_QKV), (z_ref, C_Z), (a_ref, C_A), (b_ref, C_B), (g_ref, C_G)):
        for j in range(0, width, 512):
            ref[:, j:j + 512] = jnp.dot(h, w_ref[:, col + j:col + j + 512],
                                        preferred_element_type=F32).astype(ref.dtype)
        col += width


def _inproj(x2, mod, nw, w_pad, batch, seq, tm):
    nt = seq // tm
    row = lambda b, t: (b * nt + t, 0)
    const = lambda b, t: (0, 0)
    n = batch * seq
    outs = [(C_SC, BF16), (C_QKV, BF16), (C_Z, BF16), (C_A, F32), (C_B, F32), (C_G, BF16)]
    return pl.pallas_call(
        _inproj_kernel,
        grid=(batch, nt),
        in_specs=[
            pl.BlockSpec((tm, D_MODEL), row),
            pl.BlockSpec(mod.shape, const),
            pl.BlockSpec((1, D_MODEL), const),
            pl.BlockSpec((D_MODEL, C_ALL), const, pipeline_mode=pl.Buffered(1)),
        ],
        out_specs=[pl.BlockSpec((tm, w), row) for w, _ in outs],
        out_shape=[jax.ShapeDtypeStruct((n, w), dt) for w, dt in outs],
        compiler_params=pltpu.CompilerParams(
            dimension_semantics=("arbitrary", "arbitrary"), vmem_limit_bytes=VMEM_MIXER_LIMIT),
        name="inproj",
    )(x2, mod, nw, w_pad)


def _pair_consts():
    r = lax.broadcasted_iota(I32, (PAIR, PAIR), 0)
    c = lax.broadcasted_iota(I32, (PAIR, PAIR), 1)
    bd = ((r >> 6) == (c >> 6)).astype(F32)
    i = lax.broadcasted_iota(I32, (DN_CHUNK, PAIR), 0)
    j = lax.broadcasted_iota(I32, (DN_CHUNK, PAIR), 1) & (DN_DK - 1)
    return bd, i >= j, i > j, (i == j).astype(F32)


def _stack_bd(y, bd):
    return jnp.concatenate([y, y], axis=0) * bd


def _mixer_kernel(sc_ref, qkv_ref, z_ref, a_ref, b_ref, g_ref, x_ref, mod_ref,
                  scw_ref, dnw_ref, alog_ref, dtb_ref, onw_ref, wsc_ref, wdn_ref, wmix_ref, pnw_ref,
                  o_ref,
                  scx_s, qkvx_s, q_s, k_s, v_s, g_s, beta_s, o_s, gcum_s, aqk_s, u_s, wk_s, qg_s, kdec_s,
                  state_s, *, tt):
    b = pl.program_id(0)
    t = pl.program_id(1)

    @pl.when(t == 0)
    def _():
        scx_s[0:HALO, :] = jnp.zeros((HALO, SC_WIDTH), F32)
        qkvx_s[0:HALO, :] = jnp.zeros((HALO, C_QKV), F32)
        state_s[...] = jnp.zeros(state_s.shape, F32)

    sc = sc_ref[...].astype(F32)
    scx_s[HALO:HALO + tt, :] = sc[:, SC_WIDTH:2 * SC_WIDTH] * sc[:, 2 * SC_WIDTH:]
    conv = scw_ref[0:1, :] * scx_s[pl.ds(HALO - (SC_KERNEL - 1), tt), :]
    for kk in range(1, SC_KERNEL):
        conv = conv + scw_ref[kk:kk + 1, :] * scx_s[pl.ds(HALO - (SC_KERNEL - 1) + kk, tt), :]
    y_sc = _bdot(sc[:, 0:SC_WIDTH] * conv, wsc_ref[...])
    scx_s[0:HALO, :] = scx_s[tt:tt + HALO, :]

    qkvx_s[HALO:HALO + tt, :] = qkv_ref[...].astype(F32)
    cq = dnw_ref[0:1, :] * qkvx_s[pl.ds(HALO - (DN_CONV - 1), tt), :]
    for kk in range(1, DN_CONV):
        cq = cq + dnw_ref[kk:kk + 1, :] * qkvx_s[pl.ds(HALO - (DN_CONV - 1) + kk, tt), :]
    qkvx_s[0:HALO, :] = qkvx_s[tt:tt + HALO, :]
    cq = _silu(cq)
    r512 = lax.broadcasted_iota(I32, (DN_KEY, DN_KEY), 0) >> 6
    c512 = lax.broadcasted_iota(I32, (DN_KEY, DN_KEY), 1) >> 6
    head_ones = [(r512 == c512).astype(BF16)]
    q = cq[:, 0:DN_KEY]
    k = cq[:, DN_KEY:2 * DN_KEY]
    q_s[...] = q * lax.rsqrt(_sdot(_split(q * q, 2), head_ones) + NORM_EPS) * (DN_DK ** -0.5)
    k_s[...] = k * lax.rsqrt(_sdot(_split(k * k, 2), head_ones) + NORM_EPS)
    v_s[...] = cq[:, 2 * DN_KEY:]
    beta_s[...] = jax.nn.sigmoid(b_ref[...])
    g_s[...] = -jnp.exp(alog_ref[...]) * _softplus(a_ref[...] + dtb_ref[...])

    bd, incl, strict, eye = _pair_consts()
    bd_b = bd.astype(BF16)
    ri = lax.broadcasted_iota(I32, (DN_CHUNK, DN_CHUNK), 0)
    ci = lax.broadcasted_iota(I32, (DN_CHUNK, DN_CHUNK), 1)
    lt = [(ci <= ri).astype(BF16)]
    nt = (((1,), (1,)), ((), ()))

    def stack_terms(y):
        return [jnp.concatenate([t_, t_], axis=0) * bd_b for t_ in _split(y, INV_TERMS)]

    pairs = [slice(p * PAIR, (p + 1) * PAIR) for p in range(N_PAIRS)]

    def prepare(c, carry):
        prob = []
        for cc in range(PREP_CHUNKS):
            rows = pl.ds(pl.multiple_of((c * PREP_CHUNKS + cc) * DN_CHUNK, DN_CHUNK), DN_CHUNK)
            gcum_all = _sdot(lt, _split(g_s[rows, :], 3))
            gcum_s[rows, :] = gcum_all
            prob += [(rows, sl, gcum_all[:, sl]) for sl in pairs]
        pws, invs, rest = [], [], []
        for rows, sl, gc in prob:
            qp = q_s[rows, sl]
            kp = k_s[rows, sl]
            bp = beta_s[rows, sl]
            grow = jnp.sum(gc * eye, axis=0, keepdims=True)
            dec = jnp.where(incl, jnp.exp(jnp.where(incl, gc - grow, 0.0)), 0.0)
            eg = jnp.exp(gc)
            kb = kp * bp
            kbig = _stack_bd(kp, bd).astype(BF16)
            kk_s = lax.dot_general(kb.astype(BF16), kbig, nt, preferred_element_type=F32)
            qk_s = lax.dot_general(qp.astype(BF16), kbig, nt, preferred_element_type=F32)
            aqk_s[rows, sl] = qk_s * dec
            qg_s[rows, sl] = qp * eg
            kdec_s[rows, sl] = kp * jnp.exp(gc[DN_CHUNK - 1:DN_CHUNK, :] - gc)
            pw = -(kk_s * jnp.where(strict, dec, 0.0))
            pws.append(pw)
            invs.append(eye + pw)
            rest.append((v_s[rows, sl] * bp, kb * eg))
        for _ in range(5):
            pws = [_sdot(_split(pw, INV_TERMS), stack_terms(pw)) for pw in pws]
            invs = [inv + _sdot(_split(inv, INV_TERMS), stack_terms(pw)) for inv, pw in zip(invs, pws)]
        for (rows, sl, _), inv, (vb, kbg) in zip(prob, invs, rest):
            u_s[rows, sl] = _bdot(inv, _stack_bd(vb, bd))
            wk_s[rows, sl] = _bdot(inv, _stack_bd(kbg, bd))
        return carry

    def advance(c, carry):
        rows = pl.ds(pl.multiple_of(c * DN_CHUNK, DN_CHUNK), DN_CHUNK)
        tail = pl.ds(pl.multiple_of(c * DN_CHUNK + DN_CHUNK - SUBLANES, SUBLANES), SUBLANES)
        sts = [state_s[p] for p in range(N_PAIRS)]
        wss = [_bdot(jnp.concatenate([wk_s[rows, sl], qg_s[rows, sl]], axis=0), st) for sl, st in zip(pairs, sts)]
        vnews = [u_s[rows, sl] - ws[0:DN_CHUNK] for sl, ws in zip(pairs, wss)]
        for p, (sl, st, ws, vnew) in enumerate(zip(pairs, sts, wss, vnews)):
            o_s[rows, sl] = ws[DN_CHUNK:] + _bdot(aqk_s[rows, sl], _stack_bd(vnew, bd))
            glast = gcum_s[tail, sl][SUBLANES - 1:SUBLANES, :]
            state_s[p] = st * jnp.exp(glast) + bd * _bdot(kdec_s[rows, sl].T, vnew)
        return carry

    lax.fori_loop(0, tt // (DN_CHUNK * PREP_CHUNKS), prepare, 0)
    lax.fori_loop(0, tt // DN_CHUNK, advance, 0)

    o = o_s[...]
    ms = _sdot(_split(o * o, 2), head_ones) * (1.0 / DN_DK)
    og = o * lax.rsqrt(ms + NORM_EPS) * onw_ref[...] * _silu(z_ref[...].astype(F32))
    y_dn = _bdot(og, wdn_ref[...])
    gates = g_ref[...].astype(F32)
    merged = jax.nn.sigmoid(gates[:, 0:D_MODEL]) * y_sc + jax.nn.sigmoid(gates[:, D_MODEL:]) * y_dn
    y = _bdot(merged, wmix_ref[...])
    gt1 = mod_ref[pl.ds(b, 1), :][:, 2 * D_MODEL:3 * D_MODEL]
    o_ref[...] = x_ref[...] + gt1 * _rms(y, pnw_ref[...])


def _mixer(parts, x2, mod, weights, batch, seq, tt):
    nt = seq // tt
    row = lambda b, t: (b * nt + t, 0)
    const2 = lambda b, t: (0, 0)
    sc, qkv, z, a, bb, g = parts
    data = [(sc, C_SC), (qkv, C_QKV), (z, C_Z), (a, C_A), (bb, C_B), (g, C_G), (x2, D_MODEL)]
    in_specs = [pl.BlockSpec((tt, w), row) for _, w in data]
    in_specs.append(pl.BlockSpec(mod.shape, const2))
    in_specs += [pl.BlockSpec(wt.shape, const2) for wt in weights]
    scratch = [
        pltpu.VMEM((tt + HALO, SC_WIDTH), F32),
        pltpu.VMEM((tt + HALO, C_QKV), F32),
        pltpu.VMEM((tt, DN_KEY), F32),
        pltpu.VMEM((tt, DN_KEY), F32),
        pltpu.VMEM((tt, DN_VAL), F32),
        pltpu.VMEM((tt, DN_KEY), F32),
        pltpu.VMEM((tt, DN_KEY), F32),
        pltpu.VMEM((tt, DN_VAL), F32),
    ] + [pltpu.VMEM((tt, DN_KEY), F32) for _ in range(6)] + [
        pltpu.VMEM((N_PAIRS, PAIR, PAIR), F32),
    ]
    return pl.pallas_call(
        functools.partial(_mixer_kernel, tt=tt),
        grid=(batch, nt),
        in_specs=in_specs,
        out_specs=pl.BlockSpec((tt, D_MODEL), row),
        out_shape=jax.ShapeDtypeStruct((batch * seq, D_MODEL), F32),
        scratch_shapes=scratch,
        compiler_params=pltpu.CompilerParams(
            dimension_semantics=("arbitrary", "arbitrary"), vmem_limit_bytes=VMEM_MIXER_LIMIT),
        name="mixer",
    )(*[d for d, _ in data], mod, *weights)


def _qproj_kernel(x_ref, mod_ref, nw_ref, w_ref, h_ref, q_ref):
    b = pl.program_id(0)
    mod = mod_ref[pl.ds(b, 1), :]
    sh2 = mod[:, 3 * D_MODEL:4 * D_MODEL]
    sc2 = mod[:, 4 * D_MODEL:5 * D_MODEL]
    h = _rms(x_ref[...], nw_ref[...]) * (1.0 + sc2) + sh2
    h_ref[...] = h
    q_ref[...] = _bdot(h, w_ref[...])


def _qproj(x1, mod, nw, wq, batch, seq, tm):
    nt = seq // tm
    row = lambda b, t: (b * nt + t, 0)
    const = lambda b, t: (0, 0)
    n = batch * seq
    nq = wq.shape[1]
    return pl.pallas_call(
        _qproj_kernel,
        grid=(batch, nt),
        in_specs=[
            pl.BlockSpec((tm, D_MODEL), row),
            pl.BlockSpec(mod.shape, const),
            pl.BlockSpec((1, D_MODEL), const),
            pl.BlockSpec(wq.shape, const),
        ],
        out_specs=[pl.BlockSpec((tm, D_MODEL), row), pl.BlockSpec((tm, nq), row)],
        out_shape=[jax.ShapeDtypeStruct((n, D_MODEL), F32), jax.ShapeDtypeStruct((n, nq), F32)],
        compiler_params=pltpu.CompilerParams(dimension_semantics=("arbitrary", "arbitrary")),
        name="qproj",
    )(x1, mod, nw, wq)


def _topk_rows(vals, idx_payload, k, v_out, i_out, row0):
    n_rows = vals.shape[0]
    pos_iota = lax.broadcasted_iota(I32, vals.shape, 0).astype(F32)
    for it in range(k):
        m = jnp.max(vals, axis=0, keepdims=True)
        pos = jnp.min(jnp.where(vals == m, pos_iota, float(n_rows)), axis=0, keepdims=True)
        hit = pos_iota == pos
        v_out[row0 + it:row0 + it + 1, :] = m
        if idx_payload is None:
            i_out[row0 + it:row0 + it + 1, :] = pos.astype(I32)
        else:
            i_out[row0 + it:row0 + it + 1, :] = jnp.sum(jnp.where(hit, idx_payload, 0), axis=0, keepdims=True)
        vals = jnp.where(hit, -jnp.inf, vals)


def _topk_kernel(q_ref, keys_ref, e_ref, g_ref, v1_s, i1_s, v2_s, i2_s, ts_s, ei_s, gate_s, *, tt):
    k = PEER_TOPK
    for h in range(PEER_HEADS):
        for p, (v_s, i_s) in enumerate(((v1_s, i1_s), (v2_s, i2_s))):
            col = (2 * h + p) * PEER_HALF
            st = _sdot(_split(keys_ref[2 * h + p], 2), _split(q_ref[:, col:col + PEER_HALF], 2), NT_DIMS)
            _topk_rows(st, None, k, v_s, i_s, 0)
        v2h = v2_s[0:SUBLANES, :]
        i2h = i2_s[0:SUBLANES, :]
        cv = jnp.concatenate([v1_s[0:1, :] + v2_s[...]] + [v1_s[a:a + 1, :] + v2h for a in range(1, k)], axis=0)
        ce = jnp.concatenate([i1_s[0:1, :] * PEER_NKEYS + i2_s[...]]
                             + [i1_s[a:a + 1, :] * PEER_NKEYS + i2h for a in range(1, k)], axis=0)
        _topk_rows(cv, ce, k, ts_s, ei_s, h * k)
        ts = ts_s[h * k:(h + 1) * k, :]
        e = jnp.exp(ts - jnp.max(ts, axis=0, keepdims=True))
        gate_s[h * k:(h + 1) * k, :] = e / jnp.sum(e, axis=0, keepdims=True)
    e_ref[...] = ei_s[...].T * HALF_ROWS
    g_ref[...] = gate_s[...].T


def _topk(q, keys, n, tt):
    row = lambda i: (i, 0)
    return pl.pallas_call(
        functools.partial(_topk_kernel, tt=tt),
        grid=(n // tt,),
        in_specs=[
            pl.BlockSpec((tt, q.shape[1]), row),
            pl.BlockSpec(keys.shape, lambda i: (0, 0, 0)),
        ],
        out_specs=[pl.BlockSpec((tt, PEER_E), row)] * 2,
        out_shape=[jax.ShapeDtypeStruct((n, PEER_E), I32), jax.ShapeDtypeStruct((n, PEER_E), F32)],
        scratch_shapes=[
            pltpu.VMEM((PEER_TOPK, tt), F32), pltpu.VMEM((PEER_TOPK, tt), I32),
            pltpu.VMEM((PEER_TOPK, tt), F32), pltpu.VMEM((PEER_TOPK, tt), I32),
            pltpu.VMEM((PEER_E, tt), F32), pltpu.VMEM((PEER_E, tt), I32),
            pltpu.VMEM((PEER_E, tt), F32),
        ],
        compiler_params=pltpu.CompilerParams(dimension_semantics=("arbitrary",)),
        name="topk",
    )(q, keys)


def _staged_tokens(e_ref, idx_s, sem, tt, body):
    n_groups = tt // STAGE_TOKENS
    assert n_groups % 2 == 0

    def copy(grp, slot):
        return pltpu.make_async_copy(e_ref.at[pl.ds(grp * STAGE_TOKENS, STAGE_TOKENS)], idx_s.at[slot], sem.at[slot])

    copy(0, 0).start()

    def two_groups(i, carry):
        for slot in range(2):
            grp = 2 * i + slot
            copy(jnp.minimum(grp + 1, n_groups - 1), 1 - slot).start()
            copy(grp, slot).wait()
            for tl in range(STAGE_TOKENS):
                body(grp * STAGE_TOKENS + tl, tl, lambda k, slot=slot, tl=tl: idx_s[slot, tl, k])
        return carry

    lax.fori_loop(0, n_groups // 2, two_groups, 0)
    copy(n_groups - 1, 0).wait()


def _stage_scratch():
    return [pltpu.SMEM((2, STAGE_TOKENS, PEER_E), I32), pltpu.SemaphoreType.DMA((2,))]


def _gather_group(tab_ref, idx, g, scr):
    ref = scr[g % N_SCR]
    base = (g // N_SCR) * GROUP_ROWS
    for j in range(GROUP):
        e4 = pl.multiple_of(idx(g * GROUP + j), HALF_ROWS)
        ref[pl.ds(base + j, HALF_ROWS, stride=GROUP), :] = tab_ref[pl.ds(e4, HALF_ROWS), :]
    return [ref[base + s * GROUP:base + (s + 1) * GROUP, :] for s in range(HALF_ROWS)]


def _unpack(w):
    return pltpu.bitcast(w << 16, F32), pltpu.bitcast(w & jnp.uint32(0xFFFF0000), F32)


def _row_sums_on_lanes(m, eye, ones):
    hi = m.astype(BF16)
    lo = (m - hi.astype(F32)).astype(BF16)
    sums = jnp.dot(hi, ones, preferred_element_type=F32) + jnp.dot(lo, ones, preferred_element_type=F32)
    return jnp.sum(sums * eye, axis=0, keepdims=True)


def _peer_u_kernel(e_ref, h_ref, g_ref, tab_ref, o_ref, part_s, idx_s, sem, *scr, tt):
    eye = (lax.broadcasted_iota(I32, (PEER_E, LANES), 0) == lax.broadcasted_iota(I32, (PEER_E, LANES), 1)).astype(F32)
    ones = jnp.ones((LANES, LANES), BF16)

    def finish(t, slot):
        act = _row_sums_on_lanes(part_s[slot], eye, ones)
        o_ref[pl.ds(t, 1), :] = g_ref[pl.ds(t, 1), :] * _gelu(act)

    part_s[1] = jnp.zeros((PEER_E, LANES), F32)

    def tok(t, tl, idx):
        finish(jnp.maximum(t - 1, 0), (tl + 1) & 1)
        h_lo = [jnp.broadcast_to(h_ref[t, s:s + 1, :], (GROUP, LANES)) for s in range(HALF_ROWS)]
        h_hi = [jnp.broadcast_to(h_ref[t, HALF_ROWS + s:HALF_ROWS + s + 1, :], (GROUP, LANES))
                for s in range(HALF_ROWS)]
        for g in range(N_GROUPS):
            r = None
            for s, tile in enumerate(_gather_group(tab_ref, idx, g, scr)):
                lo, hi = _unpack(tile)
                p = lo * h_lo[s] + hi * h_hi[s]
                r = p if r is None else r + p
            part_s[tl & 1, g * GROUP:(g + 1) * GROUP, :] = r

    _staged_tokens(e_ref, idx_s, sem, tt, tok)
    finish(tt - 1, (tt - 1) & 1)


def _peer_u(e4, h3, gates, tab, n, tt):
    row = lambda i: (i, 0)
    return pl.pallas_call(
        functools.partial(_peer_u_kernel, tt=tt),
        grid=(n // tt,),
        in_specs=[
            pl.BlockSpec((tt, PEER_E), row),
            pl.BlockSpec((tt, SUBLANES, LANES), lambda i: (i, 0, 0)),
            pl.BlockSpec((tt, PEER_E), row),
            pl.BlockSpec(tab.shape, lambda i: (0, 0), pipeline_mode=pl.Buffered(1)),
        ],
        out_specs=pl.BlockSpec((tt, PEER_E), row),
        out_shape=jax.ShapeDtypeStruct((n, PEER_E), F32),
        scratch_shapes=[pltpu.VMEM((2, PEER_E, LANES), F32)] + _stage_scratch()
        + [pltpu.VMEM((N_GROUPS // N_SCR * GROUP_ROWS, LANES), U32) for _ in range(N_SCR)],
        compiler_params=pltpu.CompilerParams(
            dimension_semantics=("arbitrary",), vmem_limit_bytes=VMEM_TABLE_LIMIT),
        name="peer_u",
    )(e4, h3, gates, tab)


def _peer_v_kernel(e_ref, a_ref, tab_ref, o_ref, w_s, idx_s, sem, *scr, tt):
    eye = (lax.broadcasted_iota(I32, (PEER_E, LANES), 0) == lax.broadcasted_iota(I32, (PEER_E, LANES), 1)).astype(F32)
    ones = jnp.ones((LANES, LANES), BF16)

    def spread(t):
        m = eye * a_ref[pl.ds(t, 1), :]
        hi = m.astype(BF16)
        lo = (m - hi.astype(F32)).astype(BF16)
        return jnp.dot(hi, ones, preferred_element_type=F32) + jnp.dot(lo, ones, preferred_element_type=F32)

    w_s[0] = spread(0)

    def tok(t, tl, idx):
        w_next = spread(jnp.minimum(t + 1, tt - 1))
        acc_lo = [None] * HALF_ROWS
        acc_hi = [None] * HALF_ROWS
        for g in range(N_GROUPS):
            wg = w_s[tl & 1, g * GROUP:(g + 1) * GROUP, :]
            for s, tile in enumerate(_gather_group(tab_ref, idx, g, scr)):
                lo, hi = _unpack(tile)
                acc_lo[s] = lo * wg if g == 0 else acc_lo[s] + lo * wg
                acc_hi[s] = hi * wg if g == 0 else acc_hi[s] + hi * wg
        for s in range(HALF_ROWS):
            o_ref[t, s:s + 1, :] = jnp.sum(acc_lo[s], axis=0, keepdims=True)
            o_ref[t, HALF_ROWS + s:HALF_ROWS + s + 1, :] = jnp.sum(acc_hi[s], axis=0, keepdims=True)
        w_s[(tl + 1) & 1] = w_next

    _staged_tokens(e_ref, idx_s, sem, tt, tok)


def _peer_v(e4, act, tab, n, tt):
    row = lambda i: (i, 0)
    return pl.pallas_call(
        functools.partial(_peer_v_kernel, tt=tt),
        grid=(n // tt,),
        in_specs=[
            pl.BlockSpec((tt, PEER_E), row),
            pl.BlockSpec((tt, PEER_E), row),
            pl.BlockSpec(tab.shape, lambda i: (0, 0), pipeline_mode=pl.Buffered(1)),
        ],
        out_specs=pl.BlockSpec((tt, SUBLANES, LANES), lambda i: (i, 0, 0)),
        out_shape=jax.ShapeDtypeStruct((n, SUBLANES, LANES), F32),
        scratch_shapes=[pltpu.VMEM((2, PEER_E, LANES), F32)] + _stage_scratch()
        + [pltpu.VMEM((N_GROUPS // N_SCR * GROUP_ROWS, LANES), U32) for _ in range(N_SCR)],
        compiler_params=pltpu.CompilerParams(
            dimension_semantics=("arbitrary",), vmem_limit_bytes=VMEM_TABLE_LIMIT),
        name="peer_v",
    )(e4, act, tab)


def _final_kernel(x_ref, y_ref, mod_ref, nw_ref, o_ref):
    b = pl.program_id(0)
    gt2 = mod_ref[pl.ds(b, 1), :][:, 5 * D_MODEL:6 * D_MODEL]
    o_ref[...] = x_ref[...] + gt2 * _rms(y_ref[...], nw_ref[...])


def _final(x1, y2, mod, nw, batch, seq, tm):
    nt = seq // tm
    row = lambda b, t: (b * nt + t, 0)
    const = lambda b, t: (0, 0)
    return pl.pallas_call(
        _final_kernel,
        grid=(batch, nt),
        in_specs=[
            pl.BlockSpec((tm, D_MODEL), row),
            pl.BlockSpec((tm, D_MODEL), row),
            pl.BlockSpec(mod.shape, const),
            pl.BlockSpec((1, D_MODEL), const),
        ],
        out_specs=pl.BlockSpec((tm, D_MODEL), row),
        out_shape=jax.ShapeDtypeStruct((batch * seq, D_MODEL), F32),
        compiler_params=pltpu.CompilerParams(dimension_semantics=("arbitrary", "arbitrary")),
        name="final",
    )(x1, y2, mod, nw)


def _pack_kernel(x_ref, o_ref):
    half = D_MODEL // 2
    bits = pltpu.bitcast(x_ref[...].astype(BF16).astype(F32), U32)
    w = (bits[:, :half] >> 16) | bits[:, half:]
    for s in range(HALF_ROWS):
        o_ref[pl.ds(s, x_ref.shape[0], stride=HALF_ROWS), :] = w[:, s * LANES:(s + 1) * LANES]


def _pack_table(tab):
    e, d = tab.shape
    te = 512
    assert d == D_MODEL and e % te == 0
    return pl.pallas_call(
        _pack_kernel,
        grid=(e // te,),
        in_specs=[pl.BlockSpec((te, d), lambda i: (i, 0))],
        out_specs=pl.BlockSpec((te * HALF_ROWS, LANES), lambda i: (i, 0)),
        out_shape=jax.ShapeDtypeStruct((e * HALF_ROWS, LANES), U32),
        compiler_params=pltpu.CompilerParams(dimension_semantics=("arbitrary",)),
        name="pack",
    )(tab)


def _tile(seq, cap):
    t = min(seq, cap)
    assert seq % t == 0
    return t


def kernel(x, c, w_ada, b_ada, norm_pre_mix, norm_post_mix, w_in_mix, sc_conv_w, dn_conv_w, dn_a_log,
           dn_dt_bias, dn_out_norm, w_sc_out, w_dn_out, w_mix_out, norm_pre_ffn, norm_post_ffn, peer_w_q,
           peer_sub_keys, peer_u, peer_v):
    batch, seq, d = x.shape
    assert d == D_MODEL and batch <= SUBLANES and seq % DN_CHUNK == 0
    depth = w_ada.shape[0]
    n = batch * seq
    x2 = x.reshape(n, d)
    c_pad = jnp.pad(c, ((0, SUBLANES - batch), (0, 0)))
    tab_u = _pack_table(peer_u)
    tab_v = _pack_table(peer_v)
    rep = lambda v: jnp.repeat(v, DN_DK, axis=-1)
    for l in range(depth):
        mod = _ada(c_pad, w_ada[l], b_ada[l][None, :])
        w = w_in_mix[l]
        o_z = C_SC + C_QKV
        o_a = o_z + C_Z
        w_pad = jnp.concatenate(
            [w[:, :o_a], rep(w[:, o_a:o_a + DN_HEADS]), rep(w[:, o_a + DN_HEADS:o_a + 2 * DN_HEADS]),
             w[:, o_a + 2 * DN_HEADS:]], axis=1).astype(BF16)
        tm = _tile(seq, 512)
        parts = _inproj(x2, mod, norm_pre_mix[l][None, :], w_pad, batch, seq, tm)
        weights = [
            sc_conv_w[l], dn_conv_w[l], rep(dn_a_log[l])[None, :], rep(dn_dt_bias[l])[None, :],
            jnp.tile(dn_out_norm[l], DN_HEADS)[None, :], w_sc_out[l].astype(BF16), w_dn_out[l].astype(BF16),
            w_mix_out[l].astype(BF16), norm_post_mix[l][None, :],
        ]
        x1 = _mixer(parts, x2, mod, weights, batch, seq, _tile(seq, 512))
        h2, q = _qproj(x1, mod, norm_pre_ffn[l][None, :], peer_w_q[l].astype(BF16), batch, seq, tm)
        keys = peer_sub_keys[l].reshape(2 * PEER_HEADS, PEER_NKEYS, PEER_HALF)
        e4, gates = _topk(q, keys, n, _tile(n, LANES))
        tt = _tile(n, 64)
        act = _peer_u(e4, h2.reshape(n, SUBLANES, LANES), gates, tab_u, n, tt)
        y2 = _peer_v(e4, act, tab_v, n, tt).reshape(n, d)
        x2 = _final(x1, y2, mod, norm_post_ffn[l][None, :], batch, seq, tm)
    return x2.reshape(batch, seq, d)
```

```python
import functools

import jax
import jax.numpy as jnp
from jax import lax
from jax.experimental import pallas as pl
from jax.experimental.pallas import tpu as pltpu

F32 = jnp.float32
BF16 = jnp.bfloat16
I32 = jnp.int32
U32 = jnp.uint32

D_MODEL = 1024
SC_WIDTH = 512
SC_KERNEL = 3
DN_HEADS = 8
DN_DK = 64
DN_KEY = DN_HEADS * DN_DK
DN_VAL = DN_KEY
DN_CONV = 4
DN_CHUNK = 64
PEER_HEADS = 8
PEER_NKEYS = 128
PEER_HALF = 128
PEER_QDIM = 256
PEER_TOPK = 16
PEER_E = PEER_HEADS * PEER_TOPK
NORM_EPS = 1e-6

LANES = 128
SUBLANES = 8
HALO = SUBLANES
PAIR = 2 * DN_DK
N_PAIRS = DN_HEADS // 2
INV_TERMS = 2
PREP_CHUNKS = 2
HALF_ROWS = SUBLANES // 2
GROUP = SUBLANES
GROUP_ROWS = HALF_ROWS * GROUP
N_GROUPS = PEER_E // GROUP
N_SCR = 4
STAGE_TOKENS = 4
STAGE_BUFS = 4
VMEM_TABLE_LIMIT = 56 * 1024 * 1024
VMEM_MIXER_LIMIT = 56 * 1024 * 1024

C_SC = 3 * SC_WIDTH
C_QKV = 2 * DN_KEY + DN_VAL
C_Z = DN_VAL
C_A = DN_KEY
C_B = DN_KEY
C_G = 2 * D_MODEL
C_ALL = C_SC + C_QKV + C_Z + C_A + C_B + C_G


def _silu(x):
    return x * jax.nn.sigmoid(x)


def _softplus(x):
    return jnp.maximum(x, 0.0) + jnp.log(1.0 + jnp.exp(-jnp.abs(x)))


def _gelu(x):
    return 0.5 * x * (1.0 + lax.erf(x * (2.0 ** -0.5)))


def _rms(x, w):
    return x * lax.rsqrt(jnp.mean(x * x, axis=-1, keepdims=True) + NORM_EPS) * w


def _bdot(a, b):
    return jnp.dot(a.astype(BF16), b.astype(BF16), preferred_element_type=F32)


def _xdot(a, b):
    return jnp.dot(a, b, preferred_element_type=F32, precision=lax.Precision.HIGHEST)


def _split(x, n):
    terms = []
    for i in range(n):
        t = x.astype(BF16)
        terms.append(t)
        if i + 1 < n:
            x = x - t.astype(F32)
    return terms


NN_DIMS = (((1,), (0,)), ((), ()))
NT_DIMS = (((1,), (1,)), ((), ()))


def _sdot(a_terms, b_terms, dims=NN_DIMS):
    order = max(len(a_terms), len(b_terms))
    out = None
    for i, a in enumerate(a_terms):
        for j, b in enumerate(b_terms):
            if i + j < order:
                p = lax.dot_general(a, b, dims, preferred_element_type=F32)
                out = p if out is None else out + p
    return out


def _ada_kernel(c_ref, w_ref, b_ref, o_ref):
    o_ref[...] = _bdot(_silu(c_ref[...]), w_ref[...]) + b_ref[...]


def _ada(c_pad, w_ada, b_ada):
    n_out = w_ada.shape[1]
    tn = 1024
    return pl.pallas_call(
        _ada_kernel,
        grid=(n_out // tn,),
        in_specs=[
            pl.BlockSpec((SUBLANES, D_MODEL), lambda j: (0, 0)),
            pl.BlockSpec((D_MODEL, tn), lambda j: (0, j)),
            pl.BlockSpec((1, tn), lambda j: (0, j)),
        ],
        out_specs=pl.BlockSpec((SUBLANES, tn), lambda j: (0, j)),
        out_shape=jax.ShapeDtypeStruct((SUBLANES, n_out), F32),
        name="ada",
    )(c_pad, w_ada, b_ada)


def _inproj_kernel(x_ref, mod_ref, nw_ref, w_ref, sc_ref, qkv_ref, z_ref, a_ref, b_ref, g_ref):
    b = pl.program_id(0)
    mod = mod_ref[pl.ds(b, 1), :]
    sh1 = mod[:, 0:D_MODEL]
    sc1 = mod[:, D_MODEL:2 * D_MODEL]
    h = (_rms(x_ref[...], nw_ref[...]) * (1.0 + sc1) + sh1).astype(BF16)
    col = 0
    for ref, width in ((sc_ref, C_SC), (qkv_ref, C_QKV), (z_ref, C_Z), (a_ref, C_A), (b_ref, C_B), (g_ref, C_G)):
        for j in range(0, width, 512):
            ref[:, j:j + 512] = jnp.dot(h, w_ref[:, col + j:col + j + 512],
                                        preferred_element_type=F32).astype(ref.dtype)
        col += width


def _inproj(x2, mod, nw, w_pad, batch, seq, tm):
    nt = seq // tm
    row = lambda b, t: (b * nt + t, 0)
    const = lambda b, t: (0, 0)
    n = batch * seq
    outs = [(C_SC, BF16), (C_QKV, BF16), (C_Z, BF16), (C_A, F32), (C_B, F32), (C_G, BF16)]
    return pl.pallas_call(
        _inproj_kernel,
        grid=(batch, nt),
        in_specs=[
            pl.BlockSpec((tm, D_MODEL), row),
            pl.BlockSpec(mod.shape, const),
            pl.BlockSpec((1, D_MODEL), const),
            pl.BlockSpec((D_MODEL, C_ALL), const, pipeline_mode=pl.Buffered(1)),
        ],
        out_specs=[pl.BlockSpec((tm, w), row) for w, _ in outs],
        out_shape=[jax.ShapeDtypeStruct((n, w), dt) for w, dt in outs],
        compiler_params=pltpu.CompilerParams(
            dimension_semantics=("arbitrary", "arbitrary"), vmem_limit_bytes=VMEM_MIXER_LIMIT),
        name="inproj",
    )(x2, mod, nw, w_pad)


def _pair_consts():
    r = lax.broadcasted_iota(I32, (PAIR, PAIR), 0)
    c = lax.broadcasted_iota(I32, (PAIR, PAIR), 1)
    bd = ((r >> 6) == (c >> 6)).astype(F32)
    i = lax.broadcasted_iota(I32, (DN_CHUNK, PAIR), 0)
    j = lax.broadcasted_iota(I32, (DN_CHUNK, PAIR), 1) & (DN_DK - 1)
    return bd, i >= j, i > j, (i == j).astype(F32)


def _stack_bd(y, bd):
    return jnp.concatenate([y, y], axis=0) * bd


def _mixer_kernel(sc_ref, qkv_ref, z_ref, a_ref, b_ref, g_ref, x_ref, mod_ref,
                  scw_ref, dnw_ref, alog_ref, dtb_ref, onw_ref, wsc_ref, wdn_ref, wmix_ref, pnw_ref,
                  o_ref,
                  scx_s, qkvx_s, q_s, k_s, v_s, g_s, beta_s, o_s, gcum_s, aqk_s, u_s, wk_s, qg_s, kdec_s,
                  state_s, *, tt):
    b = pl.program_id(0)
    t = pl.program_id(1)

    @pl.when(t == 0)
    def _():
        scx_s[0:HALO, :] = jnp.zeros((HALO, SC_WIDTH), F32)
        qkvx_s[0:HALO, :] = jnp.zeros((HALO, C_QKV), F32)
        state_s[...] = jnp.zeros(state_s.shape, F32)

    sc = sc_ref[...].astype(F32)
    scx_s[HALO:HALO + tt, :] = sc[:, SC_WIDTH:2 * SC_WIDTH] * sc[:, 2 * SC_WIDTH:]
    conv = scw_ref[0:1, :] * scx_s[pl.ds(HALO - (SC_KERNEL - 1), tt), :]
    for kk in range(1, SC_KERNEL):
        conv = conv + scw_ref[kk:kk + 1, :] * scx_s[pl.ds(HALO - (SC_KERNEL - 1) + kk, tt), :]
    y_sc = _bdot(sc[:, 0:SC_WIDTH] * conv, wsc_ref[...])
    scx_s[0:HALO, :] = scx_s[tt:tt + HALO, :]

    qkvx_s[HALO:HALO + tt, :] = qkv_ref[...].astype(F32)
    cq = dnw_ref[0:1, :] * qkvx_s[pl.ds(HALO - (DN_CONV - 1), tt), :]
    for kk in range(1, DN_CONV):
        cq = cq + dnw_ref[kk:kk + 1, :] * qkvx_s[pl.ds(HALO - (DN_CONV - 1) + kk, tt), :]
    qkvx_s[0:HALO, :] = qkvx_s[tt:tt + HALO, :]
    cq = _silu(cq)
    r512 = lax.broadcasted_iota(I32, (DN_KEY, DN_KEY), 0) >> 6
    c512 = lax.broadcasted_iota(I32, (DN_KEY, DN_KEY), 1) >> 6
    head_ones = [(r512 == c512).astype(BF16)]
    q = cq[:, 0:DN_KEY]
    k = cq[:, DN_KEY:2 * DN_KEY]
    q_s[...] = q * lax.rsqrt(_sdot(_split(q * q, 2), head_ones) + NORM_EPS) * (DN_DK ** -0.5)
    k_s[...] = k * lax.rsqrt(_sdot(_split(k * k, 2), head_ones) + NORM_EPS)
    v_s[...] = cq[:, 2 * DN_KEY:]
    beta_s[...] = jax.nn.sigmoid(b_ref[...])
    g_s[...] = -jnp.exp(alog_ref[...]) * _softplus(a_ref[...] + dtb_ref[...])

    bd, incl, strict, eye = _pair_consts()
    bd_b = bd.astype(BF16)
    ri = lax.broadcasted_iota(I32, (DN_CHUNK, DN_CHUNK), 0)
    ci = lax.broadcasted_iota(I32, (DN_CHUNK, DN_CHUNK), 1)
    lt = [(ci <= ri).astype(BF16)]
    nt = (((1,), (1,)), ((), ()))

    def stack_terms(y):
        return [jnp.concatenate([t_, t_], axis=0) * bd_b for t_ in _split(y, INV_TERMS)]

    pairs = [slice(p * PAIR, (p + 1) * PAIR) for p in range(N_PAIRS)]

    def prepare(c, carry):
        prob = []
        for cc in range(PREP_CHUNKS):
            rows = pl.ds(pl.multiple_of((c * PREP_CHUNKS + cc) * DN_CHUNK, DN_CHUNK), DN_CHUNK)
            gcum_all = _sdot(lt, _split(g_s[rows, :], 3))
            gcum_s[rows, :] = gcum_all
            prob += [(rows, sl, gcum_all[:, sl]) for sl in pairs]
        pws, invs, rest = [], [], []
        for rows, sl, gc in prob:
            qp = q_s[rows, sl]
            kp = k_s[rows, sl]
            bp = beta_s[rows, sl]
            grow = jnp.sum(gc * eye, axis=0, keepdims=True)
            dec = jnp.where(incl, jnp.exp(jnp.where(incl, gc - grow, 0.0)), 0.0)
            eg = jnp.exp(gc)
            kb = kp * bp
            kbig = _stack_bd(kp, bd).astype(BF16)
            kk_s = lax.dot_general(kb.astype(BF16), kbig, nt, preferred_element_type=F32)
            qk_s = lax.dot_general(qp.astype(BF16), kbig, nt, preferred_element_type=F32)
            aqk_s[rows, sl] = qk_s * dec
            qg_s[rows, sl] = qp * eg
            kdec_s[rows, sl] = kp * jnp.exp(gc[DN_CHUNK - 1:DN_CHUNK, :] - gc)
            pw = -(kk_s * jnp.where(strict, dec, 0.0))
            pws.append(pw)
            invs.append(eye + pw)
            rest.append((v_s[rows, sl] * bp, kb * eg))
        for _ in range(5):
            pws = [_sdot(_split(pw, INV_TERMS), stack_terms(pw)) for pw in pws]
            invs = [inv + _sdot(_split(inv, INV_TERMS), stack_terms(pw)) for inv, pw in zip(invs, pws)]
        for (rows, sl, _), inv, (vb, kbg) in zip(prob, invs, rest):
            u_s[rows, sl] = _bdot(inv, _stack_bd(vb, bd))
            wk_s[rows, sl] = _bdot(inv, _stack_bd(kbg, bd))
        return carry

    def advance(c, carry):
        rows = pl.ds(pl.multiple_of(c * DN_CHUNK, DN_CHUNK), DN_CHUNK)
        tail = pl.ds(pl.multiple_of(c * DN_CHUNK + DN_CHUNK - SUBLANES, SUBLANES), SUBLANES)
        sts = [state_s[p] for p in range(N_PAIRS)]
        wss = [_bdot(jnp.concatenate([wk_s[rows, sl], qg_s[rows, sl]], axis=0), st) for sl, st in zip(pairs, sts)]
        vnews = [u_s[rows, sl] - ws[0:DN_CHUNK] for sl, ws in zip(pairs, wss)]
        for p, (sl, st, ws, vnew) in enumerate(zip(pairs, sts, wss, vnews)):
            o_s[rows, sl] = ws[DN_CHUNK:] + _bdot(aqk_s[rows, sl], _stack_bd(vnew, bd))
            glast = gcum_s[tail, sl][SUBLANES - 1:SUBLANES, :]
            state_s[p] = st * jnp.exp(glast) + bd * _bdot(kdec_s[rows, sl].T, vnew)
        return carry

    lax.fori_loop(0, tt // (DN_CHUNK * PREP_CHUNKS), prepare, 0)
    lax.fori_loop(0, tt // DN_CHUNK, advance, 0)

    o = o_s[...]
    ms = _sdot(_split(o * o, 2), head_ones) * (1.0 / DN_DK)
    og = o * lax.rsqrt(ms + NORM_EPS) * onw_ref[...] * _silu(z_ref[...].astype(F32))
    y_dn = _bdot(og, wdn_ref[...])
    gates = g_ref[...].astype(F32)
    merged = jax.nn.sigmoid(gates[:, 0:D_MODEL]) * y_sc + jax.nn.sigmoid(gates[:, D_MODEL:]) * y_dn
    y = _bdot(merged, wmix_ref[...])
    gt1 = mod_ref[pl.ds(b, 1), :][:, 2 * D_MODEL:3 * D_MODEL]
    o_ref[...] = x_ref[...] + gt1 * _rms(y, pnw_ref[...])


def _mixer(parts, x2, mod, weights, batch, seq, tt):
    nt = seq // tt
    row = lambda b, t: (b * nt + t, 0)
    const2 = lambda b, t: (0, 0)
    sc, qkv, z, a, bb, g = parts
    data = [(sc, C_SC), (qkv, C_QKV), (z, C_Z), (a, C_A), (bb, C_B), (g, C_G), (x2, D_MODEL)]
    in_specs = [pl.BlockSpec((tt, w), row) for _, w in data]
    in_specs.append(pl.BlockSpec(mod.shape, const2))
    in_specs += [pl.BlockSpec(wt.shape, const2) for wt in weights]
    scratch = [
        pltpu.VMEM((tt + HALO, SC_WIDTH), F32),
        pltpu.VMEM((tt + HALO, C_QKV), F32),
        pltpu.VMEM((tt, DN_KEY), F32),
        pltpu.VMEM((tt, DN_KEY), F32),
        pltpu.VMEM((tt, DN_VAL), F32),
        pltpu.VMEM((tt, DN_KEY), F32),
        pltpu.VMEM((tt, DN_KEY), F32),
        pltpu.VMEM((tt, DN_VAL), F32),
    ] + [pltpu.VMEM((tt, DN_KEY), F32) for _ in range(6)] + [
        pltpu.VMEM((N_PAIRS, PAIR, PAIR), F32),
    ]
    return pl.pallas_call(
        functools.partial(_mixer_kernel, tt=tt),
        grid=(batch, nt),
        in_specs=in_specs,
        out_specs=pl.BlockSpec((tt, D_MODEL), row),
        out_shape=jax.ShapeDtypeStruct((batch * seq, D_MODEL), F32),
        scratch_shapes=scratch,
        compiler_params=pltpu.CompilerParams(
            dimension_semantics=("arbitrary", "arbitrary"), vmem_limit_bytes=VMEM_MIXER_LIMIT),
        name="mixer",
    )(*[d for d, _ in data], mod, *weights)


def _qproj_kernel(x_ref, mod_ref, nw_ref, w_ref, h_ref, q_ref):
    b = pl.program_id(0)
    mod = mod_ref[pl.ds(b, 1), :]
    sh2 = mod[:, 3 * D_MODEL:4 * D_MODEL]
    sc2 = mod[:, 4 * D_MODEL:5 * D_MODEL]
    h = _rms(x_ref[...], nw_ref[...]) * (1.0 + sc2) + sh2
    h_ref[...] = h
    q_ref[...] = _bdot(h, w_ref[...])


def _qproj(x1, mod, nw, wq, batch, seq, tm):
    nt = seq // tm
    row = lambda b, t: (b * nt + t, 0)
    const = lambda b, t: (0, 0)
    n = batch * seq
    nq = wq.shape[1]
    return pl.pallas_call(
        _qproj_kernel,
        grid=(batch, nt),
        in_specs=[
            pl.BlockSpec((tm, D_MODEL), row),
            pl.BlockSpec(mod.shape, const),
            pl.BlockSpec((1, D_MODEL), const),
            pl.BlockSpec(wq.shape, const),
        ],
        out_specs=[pl.BlockSpec((tm, D_MODEL), row), pl.BlockSpec((tm, nq), row)],
        out_shape=[jax.ShapeDtypeStruct((n, D_MODEL), F32), jax.ShapeDtypeStruct((n, nq), F32)],
        compiler_params=pltpu.CompilerParams(dimension_semantics=("arbitrary", "arbitrary")),
        name="qproj",
    )(x1, mod, nw, wq)


def _topk_rows(vals, idx_payload, k, v_out, i_out, row0):
    n_rows = vals.shape[0]
    pos_iota = lax.broadcasted_iota(I32, vals.shape, 0).astype(F32)
    for it in range(k):
        m = jnp.max(vals, axis=0, keepdims=True)
        pos = jnp.min(jnp.where(vals == m, pos_iota, float(n_rows)), axis=0, keepdims=True)
        hit = pos_iota == pos
        v_out[row0 + it:row0 + it + 1, :] = m
        if idx_payload is None:
            i_out[row0 + it:row0 + it + 1, :] = pos.astype(I32)
        else:
            i_out[row0 + it:row0 + it + 1, :] = jnp.sum(jnp.where(hit, idx_payload, 0), axis=0, keepdims=True)
        vals = jnp.where(hit, -jnp.inf, vals)


def _topk_kernel(q_ref, keys_ref, e_ref, g_ref, v1_s, i1_s, v2_s, i2_s, ts_s, ei_s, gate_s, *, tt):
    k = PEER_TOPK
    for h in range(PEER_HEADS):
        for p, (v_s, i_s) in enumerate(((v1_s, i1_s), (v2_s, i2_s))):
            col = (2 * h + p) * PEER_HALF
            st = _sdot(_split(keys_ref[2 * h + p], 2), _split(q_ref[:, col:col + PEER_HALF], 2), NT_DIMS)
            _topk_rows(st, None, k, v_s, i_s, 0)
        v2h = v2_s[0:SUBLANES, :]
        i2h = i2_s[0:SUBLANES, :]
        cv = jnp.concatenate([v1_s[0:1, :] + v2_s[...]] + [v1_s[a:a + 1, :] + v2h for a in range(1, k)], axis=0)
        ce = jnp.concatenate([i1_s[0:1, :] * PEER_NKEYS + i2_s[...]]
                             + [i1_s[a:a + 1, :] * PEER_NKEYS + i2h for a in range(1, k)], axis=0)
        _topk_rows(cv, ce, k, ts_s, ei_s, h * k)
        ts = ts_s[h * k:(h + 1) * k, :]
        e = jnp.exp(ts - jnp.max(ts, axis=0, keepdims=True))
        gate_s[h * k:(h + 1) * k, :] = e / jnp.sum(e, axis=0, keepdims=True)
    e_ref[...] = ei_s[...].T * HALF_ROWS
    g_ref[...] = gate_s[...].T


def _topk(q, keys, n, tt):
    row = lambda i: (i, 0)
    return pl.pallas_call(
        functools.partial(_topk_kernel, tt=tt),
        grid=(n // tt,),
        in_specs=[
            pl.BlockSpec((tt, q.shape[1]), row),
            pl.BlockSpec(keys.shape, lambda i: (0, 0, 0)),
        ],
        out_specs=[pl.BlockSpec((tt, PEER_E), row)] * 2,
        out_shape=[jax.ShapeDtypeStruct((n, PEER_E), I32), jax.ShapeDtypeStruct((n, PEER_E), F32)],
        scratch_shapes=[
            pltpu.VMEM((PEER_TOPK, tt), F32), pltpu.VMEM((PEER_TOPK, tt), I32),
            pltpu.VMEM((PEER_TOPK, tt), F32), pltpu.VMEM((PEER_TOPK, tt), I32),
            pltpu.VMEM((PEER_E, tt), F32), pltpu.VMEM((PEER_E, tt), I32),
            pltpu.VMEM((PEER_E, tt), F32),
        ],
        compiler_params=pltpu.CompilerParams(dimension_semantics=("arbitrary",)),
        name="topk",
    )(q, keys)


def _staged_tokens(e_ref, idx_s, sem, tt, body):
    n_groups = tt // STAGE_TOKENS
    nb = STAGE_BUFS
    assert n_groups % nb == 0 and STAGE_TOKENS % 2 == 0

    def copy(grp, slot):
        return pltpu.make_async_copy(e_ref.at[pl.ds(grp * STAGE_TOKENS, STAGE_TOKENS)], idx_s.at[slot], sem.at[slot])

    for s in range(nb - 1):
        copy(s, s).start()

    def round_of_groups(i, carry):
        for slot in range(nb):
            grp = nb * i + slot
            copy(jnp.minimum(grp + nb - 1, n_groups - 1), (slot + nb - 1) % nb).start()
            copy(grp, slot).wait()
            for tl in range(STAGE_TOKENS):
                body(grp * STAGE_TOKENS + tl, tl, lambda k, slot=slot, tl=tl: idx_s[slot, tl, k])
        return carry

    lax.fori_loop(0, n_groups // nb, round_of_groups, 0)
    for s in range(nb - 1):
        copy(n_groups - 1, (n_groups + s) % nb).wait()


def _stage_scratch():
    return [pltpu.SMEM((STAGE_BUFS, STAGE_TOKENS, PEER_E), I32), pltpu.SemaphoreType.DMA((STAGE_BUFS,))]


def _gather_rows(tab_ref, idx, g, scr):
    ref = scr[g % N_SCR]
    base = (g // N_SCR) * GROUP_ROWS
    for j in range(GROUP):
        e4 = pl.multiple_of(idx(g * GROUP + j), HALF_ROWS)
        ref[pl.ds(base + j, HALF_ROWS, stride=GROUP), :] = tab_ref[pl.ds(e4, HALF_ROWS), :]


def _group_tiles(g, scr):
    ref = scr[g % N_SCR]
    base = (g // N_SCR) * GROUP_ROWS
    return [ref[base + s * GROUP:base + (s + 1) * GROUP, :] for s in range(HALF_ROWS)]


def _unpack(w):
    return pltpu.bitcast(w << 16, F32), pltpu.bitcast(w & jnp.uint32(0xFFFF0000), F32)


def _row_sums_on_lanes(m, eye, ones):
    hi = m.astype(BF16)
    lo = (m - hi.astype(F32)).astype(BF16)
    sums = jnp.dot(hi, ones, preferred_element_type=F32) + jnp.dot(lo, ones, preferred_element_type=F32)
    return jnp.sum(sums * eye, axis=0, keepdims=True)


def _peer_u_kernel(e_ref, h_ref, g_ref, tab_ref, o_ref, part_s, idx_s, sem, *scr, tt):
    eye = (lax.broadcasted_iota(I32, (PEER_E, LANES), 0) == lax.broadcasted_iota(I32, (PEER_E, LANES), 1)).astype(F32)
    ones = jnp.ones((LANES, LANES), BF16)

    def finish(t, slot):
        act = _row_sums_on_lanes(part_s[slot], eye, ones)
        o_ref[pl.ds(t, 1), :] = g_ref[pl.ds(t, 1), :] * _gelu(act)

    part_s[1] = jnp.zeros((PEER_E, LANES), F32)

    def tok(t, tl, idx):
        finish(jnp.maximum(t - 1, 0), (tl + 1) & 1)
        h_lo = [jnp.broadcast_to(h_ref[t, s:s + 1, :], (GROUP, LANES)) for s in range(HALF_ROWS)]
        h_hi = [jnp.broadcast_to(h_ref[t, HALF_ROWS + s:HALF_ROWS + s + 1, :], (GROUP, LANES))
                for s in range(HALF_ROWS)]
        _gather_rows(tab_ref, idx, 0, scr)
        for g in range(N_GROUPS):
            if g + 1 < N_GROUPS:
                _gather_rows(tab_ref, idx, g + 1, scr)
            r = None
            for s, tile in enumerate(_group_tiles(g, scr)):
                lo, hi = _unpack(tile)
                p = lo * h_lo[s] + hi * h_hi[s]
                r = p if r is None else r + p
            part_s[tl & 1, g * GROUP:(g + 1) * GROUP, :] = r

    _staged_tokens(e_ref, idx_s, sem, tt, tok)
    finish(tt - 1, (tt - 1) & 1)


def _peer_u(e4, h3, gates, tab, n, tt):
    row = lambda i: (i, 0)
    return pl.pallas_call(
        functools.partial(_peer_u_kernel, tt=tt),
        grid=(n // tt,),
        in_specs=[
            pl.BlockSpec((tt, PEER_E), row),
            pl.BlockSpec((tt, SUBLANES, LANES), lambda i: (i, 0, 0)),
            pl.BlockSpec((tt, PEER_E), row),
            pl.BlockSpec(tab.shape, lambda i: (0, 0), pipeline_mode=pl.Buffered(1)),
        ],
        out_specs=pl.BlockSpec((tt, PEER_E), row),
        out_shape=jax.ShapeDtypeStruct((n, PEER_E), F32),
        scratch_shapes=[pltpu.VMEM((2, PEER_E, LANES), F32)] + _stage_scratch()
        + [pltpu.VMEM((N_GROUPS // N_SCR * GROUP_ROWS, LANES), U32) for _ in range(N_SCR)],
        compiler_params=pltpu.CompilerParams(
            dimension_semantics=("arbitrary",), vmem_limit_bytes=VMEM_TABLE_LIMIT),
        name="peer_u",
    )(e4, h3, gates, tab)


def _peer_v_kernel(e_ref, a_ref, tab_ref, o_ref, w_s, idx_s, sem, *scr, tt):
    eye = (lax.broadcasted_iota(I32, (PEER_E, LANES), 0) == lax.broadcasted_iota(I32, (PEER_E, LANES), 1)).astype(F32)
    ones = jnp.ones((LANES, LANES), BF16)

    def spread(t):
        m = eye * a_ref[pl.ds(t, 1), :]
        hi = m.astype(BF16)
        lo = (m - hi.astype(F32)).astype(BF16)
        return jnp.dot(hi, ones, preferred_element_type=F32) + jnp.dot(lo, ones, preferred_element_type=F32)

    w_s[0] = spread(0)

    def tok(t, tl, idx):
        w_next = spread(jnp.minimum(t + 1, tt - 1))
        acc_lo = [None] * HALF_ROWS
        acc_hi = [None] * HALF_ROWS
        _gather_rows(tab_ref, idx, 0, scr)
        for g in range(N_GROUPS):
            if g + 1 < N_GROUPS:
                _gather_rows(tab_ref, idx, g + 1, scr)
            wg = w_s[tl & 1, g * GROUP:(g + 1) * GROUP, :]
            for s, tile in enumerate(_group_tiles(g, scr)):
                lo, hi = _unpack(tile)
                acc_lo[s] = lo * wg if g == 0 else acc_lo[s] + lo * wg
                acc_hi[s] = hi * wg if g == 0 else acc_hi[s] + hi * wg
        for s in range(HALF_ROWS):
            o_ref[t, s:s + 1, :] = jnp.sum(acc_lo[s], axis=0, keepdims=True)
            o_ref[t, HALF_ROWS + s:HALF_ROWS + s + 1, :] = jnp.sum(acc_hi[s], axis=0, keepdims=True)
        w_s[(tl + 1) & 1] = w_next

    _staged_tokens(e_ref, idx_s, sem, tt, tok)


def _peer_v(e4, act, tab, n, tt):
    row = lambda i: (i, 0)
    return pl.pallas_call(
        functools.partial(_peer_v_kernel, tt=tt),
        grid=(n // tt,),
        in_specs=[
            pl.BlockSpec((tt, PEER_E), row),
            pl.BlockSpec((tt, PEER_E), row),
            pl.BlockSpec(tab.shape, lambda i: (0, 0), pipeline_mode=pl.Buffered(1)),
        ],
        out_specs=pl.BlockSpec((tt, SUBLANES, LANES), lambda i: (i, 0, 0)),
        out_shape=jax.ShapeDtypeStruct((n, SUBLANES, LANES), F32),
        scratch_shapes=[pltpu.VMEM((2, PEER_E, LANES), F32)] + _stage_scratch()
        + [pltpu.VMEM((N_GROUPS // N_SCR * GROUP_ROWS, LANES), U32) for _ in range(N_SCR)],
        compiler_params=pltpu.CompilerParams(
            dimension_semantics=("arbitrary",), vmem_limit_bytes=VMEM_TABLE_LIMIT),
        name="peer_v",
    )(e4, act, tab)


def _final_kernel(x_ref, y_ref, mod_ref, nw_ref, o_ref):
    b = pl.program_id(0)
    gt2 = mod_ref[pl.ds(b, 1), :][:, 5 * D_MODEL:6 * D_MODEL]
    o_ref[...] = x_ref[...] + gt2 * _rms(y_ref[...], nw_ref[...])


def _final(x1, y2, mod, nw, batch, seq, tm):
    nt = seq // tm
    row = lambda b, t: (b * nt + t, 0)
    const = lambda b, t: (0, 0)
    return pl.pallas_call(
        _final_kernel,
        grid=(batch, nt),
        in_specs=[
            pl.BlockSpec((tm, D_MODEL), row),
            pl.BlockSpec((tm, D_MODEL), row),
            pl.BlockSpec(mod.shape, const),
            pl.BlockSpec((1, D_MODEL), const),
        ],
        out_specs=pl.BlockSpec((tm, D_MODEL), row),
        out_shape=jax.ShapeDtypeStruct((batch * seq, D_MODEL), F32),
        compiler_params=pltpu.CompilerParams(dimension_semantics=("arbitrary", "arbitrary")),
        name="final",
    )(x1, y2, mod, nw)


def _pack_kernel(x_ref, o_ref):
    half = D_MODEL // 2
    bits = pltpu.bitcast(x_ref[...].astype(BF16).astype(F32), U32)
    w = (bits[:, :half] >> 16) | bits[:, half:]
    for s in range(HALF_ROWS):
        o_ref[pl.ds(s, x_ref.shape[0], stride=HALF_ROWS), :] = w[:, s * LANES:(s + 1) * LANES]


def _pack_table(tab):
    e, d = tab.shape
    te = 512
    assert d == D_MODEL and e % te == 0
    return pl.pallas_call(
        _pack_kernel,
        grid=(e // te,),
        in_specs=[pl.BlockSpec((te, d), lambda i: (i, 0))],
        out_specs=pl.BlockSpec((te * HALF_ROWS, LANES), lambda i: (i, 0)),
        out_shape=jax.ShapeDtypeStruct((e * HALF_ROWS, LANES), U32),
        compiler_params=pltpu.CompilerParams(dimension_semantics=("arbitrary",)),
        name="pack",
    )(tab)


def _tile(seq, cap):
    t = min(seq, cap)
    assert seq % t == 0
    return t


def kernel(x, c, w_ada, b_ada, norm_pre_mix, norm_post_mix, w_in_mix, sc_conv_w, dn_conv_w, dn_a_log,
           dn_dt_bias, dn_out_norm, w_sc_out, w_dn_out, w_mix_out, norm_pre_ffn, norm_post_ffn, peer_w_q,
           peer_sub_keys, peer_u, peer_v):
    batch, seq, d = x.shape
    assert d == D_MODEL and batch <= SUBLANES and seq % DN_CHUNK == 0
    depth = w_ada.shape[0]
    n = batch * seq
    x2 = x.reshape(n, d)
    c_pad = jnp.pad(c, ((0, SUBLANES - batch), (0, 0)))
    tab_u = _pack_table(peer_u)
    tab_v = _pack_table(peer_v)
    rep = lambda v: jnp.repeat(v, DN_DK, axis=-1)
    for l in range(depth):
        mod = _ada(c_pad, w_ada[l], b_ada[l][None, :])
        w = w_in_mix[l]
        o_z = C_SC + C_QKV
        o_a = o_z + C_Z
        w_pad = jnp.concatenate(
            [w[:, :o_a], rep(w[:, o_a:o_a + DN_HEADS]), rep(w[:, o_a + DN_HEADS:o_a + 2 * DN_HEADS]),
             w[:, o_a + 2 * DN_HEADS:]], axis=1).astype(BF16)
        tm = _tile(seq, 512)
        parts = _inproj(x2, mod, norm_pre_mix[l][None, :], w_pad, batch, seq, tm)
        weights = [
            sc_conv_w[l], dn_conv_w[l], rep(dn_a_log[l])[None, :], rep(dn_dt_bias[l])[None, :],
            jnp.tile(dn_out_norm[l], DN_HEADS)[None, :], w_sc_out[l].astype(BF16), w_dn_out[l].astype(BF16),
            w_mix_out[l].astype(BF16), norm_post_mix[l][None, :],
        ]
        x1 = _mixer(parts, x2, mod, weights, batch, seq, _tile(seq, 512))
        h2, q = _qproj(x1, mod, norm_pre_ffn[l][None, :], peer_w_q[l].astype(BF16), batch, seq, tm)
        keys = peer_sub_keys[l].reshape(2 * PEER_HEADS, PEER_NKEYS, PEER_HALF)
        e4, gates = _topk(q, keys, n, _tile(n, LANES))
        tt = _tile(n, 64)
        act = _peer_u(e4, h2.reshape(n, SUBLANES, LANES), gates, tab_u, n, tt)
        y2 = _peer_v(e4, act, tab_v, n, tt).reshape(n, d)
        x2 = _final(x1, y2, mod, norm_post_ffn[l][None, :], batch, seq, tm)
    return x2.reshape(batch, seq, d)
```

```python
import functools

import jax
import jax.numpy as jnp
from jax import lax
from jax.experimental import pallas as pl
from jax.experimental.pallas import tpu as pltpu

F32 = jnp.float32
BF16 = jnp.bfloat16
I32 = jnp.int32
U32 = jnp.uint32

D_MODEL = 1024
SC_WIDTH = 512
SC_KERNEL = 3
DN_HEADS = 8
DN_DK = 64
DN_KEY = DN_HEADS * DN_DK
DN_VAL = DN_KEY
DN_CONV = 4
DN_CHUNK = 64
PEER_HEADS = 8
PEER_NKEYS = 128
PEER_HALF = 128
PEER_QDIM = 256
PEER_TOPK = 16
PEER_E = PEER_HEADS * PEER_TOPK
NORM_EPS = 1e-6

LANES = 128
SUBLANES = 8
HALO = SUBLANES
PAIR = 2 * DN_DK
N_PAIRS = DN_HEADS // 2
INV_TERMS = 2
PREP_CHUNKS = 2
HALF_ROWS = SUBLANES // 2
GROUP = SUBLANES
GROUP_ROWS = HALF_ROWS * GROUP
N_GROUPS = PEER_E // GROUP
N_SCR = 4
STAGE_TOKENS = 8
STAGE_BUFS = 2
VMEM_TABLE_LIMIT = 56 * 1024 * 1024
VMEM_MIXER_LIMIT = 56 * 1024 * 1024

C_SC = 3 * SC_WIDTH
C_QKV = 2 * DN_KEY + DN_VAL
C_Z = DN_VAL
C_A = DN_KEY
C_B = DN_KEY
C_G = 2 * D_MODEL
C_ALL = C_SC + C_QKV + C_Z + C_A + C_B + C_G


def _silu(x):
    return x * jax.nn.sigmoid(x)


def _softplus(x):
    return jnp.maximum(x, 0.0) + jnp.log(1.0 + jnp.exp(-jnp.abs(x)))


def _gelu(x):
    return 0.5 * x * (1.0 + lax.erf(x * (2.0 ** -0.5)))


def _rms(x, w):
    return x * lax.rsqrt(jnp.mean(x * x, axis=-1, keepdims=True) + NORM_EPS) * w


def _bdot(a, b):
    return jnp.dot(a.astype(BF16), b.astype(BF16), preferred_element_type=F32)


def _xdot(a, b):
    return jnp.dot(a, b, preferred_element_type=F32, precision=lax.Precision.HIGHEST)


def _split(x, n):
    terms = []
    for i in range(n):
        t = x.astype(BF16)
        terms.append(t)
        if i + 1 < n:
            x = x - t.astype(F32)
    return terms


NN_DIMS = (((1,), (0,)), ((), ()))
NT_DIMS = (((1,), (1,)), ((), ()))


def _sdot(a_terms, b_terms, dims=NN_DIMS):
    order = max(len(a_terms), len(b_terms))
    out = None
    for i, a in enumerate(a_terms):
        for j, b in enumerate(b_terms):
            if i + j < order:
                p = lax.dot_general(a, b, dims, preferred_element_type=F32)
                out = p if out is None else out + p
    return out


def _ada_kernel(c_ref, w_ref, b_ref, o_ref):
    o_ref[...] = _bdot(_silu(c_ref[...]), w_ref[...]) + b_ref[...]


def _ada(c_pad, w_ada, b_ada):
    n_out = w_ada.shape[1]
    tn = 1024
    return pl.pallas_call(
        _ada_kernel,
        grid=(n_out // tn,),
        in_specs=[
            pl.BlockSpec((SUBLANES, D_MODEL), lambda j: (0, 0)),
            pl.BlockSpec((D_MODEL, tn), lambda j: (0, j)),
            pl.BlockSpec((1, tn), lambda j: (0, j)),
        ],
        out_specs=pl.BlockSpec((SUBLANES, tn), lambda j: (0, j)),
        out_shape=jax.ShapeDtypeStruct((SUBLANES, n_out), F32),
        name="ada",
    )(c_pad, w_ada, b_ada)


def _inproj_kernel(x_ref, mod_ref, nw_ref, w_ref, sc_ref, qkv_ref, z_ref, a_ref, b_ref, g_ref):
    b = pl.program_id(0)
    mod = mod_ref[pl.ds(b, 1), :]
    sh1 = mod[:, 0:D_MODEL]
    sc1 = mod[:, D_MODEL:2 * D_MODEL]
    h = (_rms(x_ref[...], nw_ref[...]) * (1.0 + sc1) + sh1).astype(BF16)
    col = 0
    for ref, width in ((sc_ref, C_SC), (qkv_ref, C_QKV), (z_ref, C_Z), (a_ref, C_A), (b_ref, C_B), (g_ref, C_G)):
        for j in range(0, width, 512):
            ref[:, j:j + 512] = jnp.dot(h, w_ref[:, col + j:col + j + 512],
                                        preferred_element_type=F32).astype(ref.dtype)
        col += width


def _inproj(x2, mod, nw, w_pad, batch, seq, tm):
    nt = seq // tm
    row = lambda b, t: (b * nt + t, 0)
    const = lambda b, t: (0, 0)
    n = batch * seq
    outs = [(C_SC, BF16), (C_QKV, BF16), (C_Z, BF16), (C_A, F32), (C_B, F32), (C_G, BF16)]
    return pl.pallas_call(
        _inproj_kernel,
        grid=(batch, nt),
        in_specs=[
            pl.BlockSpec((tm, D_MODEL), row),
            pl.BlockSpec(mod.shape, const),
            pl.BlockSpec((1, D_MODEL), const),
            pl.BlockSpec((D_MODEL, C_ALL), const, pipeline_mode=pl.Buffered(1)),
        ],
        out_specs=[pl.BlockSpec((tm, w), row) for w, _ in outs],
        out_shape=[jax.ShapeDtypeStruct((n, w), dt) for w, dt in outs],
        compiler_params=pltpu.CompilerParams(
            dimension_semantics=("arbitrary", "arbitrary"), vmem_limit_bytes=VMEM_MIXER_LIMIT),
        name="inproj",
    )(x2, mod, nw, w_pad)


def _pair_consts():
    r = lax.broadcasted_iota(I32, (PAIR, PAIR), 0)
    c = lax.broadcasted_iota(I32, (PAIR, PAIR), 1)
    bd = ((r >> 6) == (c >> 6)).astype(F32)
    i = lax.broadcasted_iota(I32, (DN_CHUNK, PAIR), 0)
    j = lax.broadcasted_iota(I32, (DN_CHUNK, PAIR), 1) & (DN_DK - 1)
    return bd, i >= j, i > j, (i == j).astype(F32)


def _stack_bd(y, bd):
    return jnp.concatenate([y, y], axis=0) * bd


def _mixer_kernel(sc_ref, qkv_ref, z_ref, a_ref, b_ref, g_ref, x_ref, mod_ref,
                  scw_ref, dnw_ref, alog_ref, dtb_ref, onw_ref, wsc_ref, wdn_ref, wmix_ref, pnw_ref,
                  o_ref,
                  scx_s, qkvx_s, q_s, k_s, v_s, g_s, beta_s, o_s, gcum_s, aqk_s, u_s, wk_s, qg_s, kdec_s,
                  state_s, *, tt):
    b = pl.program_id(0)
    t = pl.program_id(1)

    @pl.when(t == 0)
    def _():
        scx_s[0:HALO, :] = jnp.zeros((HALO, SC_WIDTH), F32)
        qkvx_s[0:HALO, :] = jnp.zeros((HALO, C_QKV), F32)
        state_s[...] = jnp.zeros(state_s.shape, F32)

    sc = sc_ref[...].astype(F32)
    scx_s[HALO:HALO + tt, :] = sc[:, SC_WIDTH:2 * SC_WIDTH] * sc[:, 2 * SC_WIDTH:]
    conv = scw_ref[0:1, :] * scx_s[pl.ds(HALO - (SC_KERNEL - 1), tt), :]
    for kk in range(1, SC_KERNEL):
        conv = conv + scw_ref[kk:kk + 1, :] * scx_s[pl.ds(HALO - (SC_KERNEL - 1) + kk, tt), :]
    y_sc = _bdot(sc[:, 0:SC_WIDTH] * conv, wsc_ref[...])
    scx_s[0:HALO, :] = scx_s[tt:tt + HALO, :]

    qkvx_s[HALO:HALO + tt, :] = qkv_ref[...].astype(F32)
    cq = dnw_ref[0:1, :] * qkvx_s[pl.ds(HALO - (DN_CONV - 1), tt), :]
    for kk in range(1, DN_CONV):
        cq = cq + dnw_ref[kk:kk + 1, :] * qkvx_s[pl.ds(HALO - (DN_CONV - 1) + kk, tt), :]
    qkvx_s[0:HALO, :] = qkvx_s[tt:tt + HALO, :]
    cq = _silu(cq)
    r512 = lax.broadcasted_iota(I32, (DN_KEY, DN_KEY), 0) >> 6
    c512 = lax.broadcasted_iota(I32, (DN_KEY, DN_KEY), 1) >> 6
    head_ones = [(r512 == c512).astype(BF16)]
    q = cq[:, 0:DN_KEY]
    k = cq[:, DN_KEY:2 * DN_KEY]
    q_s[...] = q * lax.rsqrt(_sdot(_split(q * q, 2), head_ones) + NORM_EPS) * (DN_DK ** -0.5)
    k_s[...] = k * lax.rsqrt(_sdot(_split(k * k, 2), head_ones) + NORM_EPS)
    v_s[...] = cq[:, 2 * DN_KEY:]
    beta_s[...] = jax.nn.sigmoid(b_ref[...])
    g_s[...] = -jnp.exp(alog_ref[...]) * _softplus(a_ref[...] + dtb_ref[...])

    bd, incl, strict, eye = _pair_consts()
    bd_b = bd.astype(BF16)
    ri = lax.broadcasted_iota(I32, (DN_CHUNK, DN_CHUNK), 0)
    ci = lax.broadcasted_iota(I32, (DN_CHUNK, DN_CHUNK), 1)
    lt = [(ci <= ri).astype(BF16)]
    nt = (((1,), (1,)), ((), ()))

    def stack_terms(y):
        return [jnp.concatenate([t_, t_], axis=0) * bd_b for t_ in _split(y, INV_TERMS)]

    pairs = [slice(p * PAIR, (p + 1) * PAIR) for p in range(N_PAIRS)]

    def prepare(c, carry):
        prob = []
        for cc in range(PREP_CHUNKS):
            rows = pl.ds(pl.multiple_of((c * PREP_CHUNKS + cc) * DN_CHUNK, DN_CHUNK), DN_CHUNK)
            gcum_all = _sdot(lt, _split(g_s[rows, :], 3))
            gcum_s[rows, :] = gcum_all
            prob += [(rows, sl, gcum_all[:, sl]) for sl in pairs]
        pws, invs, rest = [], [], []
        for rows, sl, gc in prob:
            qp = q_s[rows, sl]
            kp = k_s[rows, sl]
            bp = beta_s[rows, sl]
            grow = jnp.sum(gc * eye, axis=0, keepdims=True)
            dec = jnp.where(incl, jnp.exp(jnp.where(incl, gc - grow, 0.0)), 0.0)
            eg = jnp.exp(gc)
            kb = kp * bp
            kbig = _stack_bd(kp, bd).astype(BF16)
            kk_s = lax.dot_general(kb.astype(BF16), kbig, nt, preferred_element_type=F32)
            qk_s = lax.dot_general(qp.astype(BF16), kbig, nt, preferred_element_type=F32)
            aqk_s[rows, sl] = qk_s * dec
            qg_s[rows, sl] = qp * eg
            kdec_s[rows, sl] = kp * jnp.exp(gc[DN_CHUNK - 1:DN_CHUNK, :] - gc)
            pw = -(kk_s * jnp.where(strict, dec, 0.0))
            pws.append(pw)
            invs.append(eye + pw)
            rest.append((v_s[rows, sl] * bp, kb * eg))
        for _ in range(5):
            pws = [_sdot(_split(pw, INV_TERMS), stack_terms(pw)) for pw in pws]
            invs = [inv + _sdot(_split(inv, INV_TERMS), stack_terms(pw)) for inv, pw in zip(invs, pws)]
        for (rows, sl, _), inv, (vb, kbg) in zip(prob, invs, rest):
            u_s[rows, sl] = _bdot(inv, _stack_bd(vb, bd))
            wk_s[rows, sl] = _bdot(inv, _stack_bd(kbg, bd))
        return carry

    def advance(c, carry):
        rows = pl.ds(pl.multiple_of(c * DN_CHUNK, DN_CHUNK), DN_CHUNK)
        tail = pl.ds(pl.multiple_of(c * DN_CHUNK + DN_CHUNK - SUBLANES, SUBLANES), SUBLANES)
        sts = [state_s[p] for p in range(N_PAIRS)]
        wss = [_bdot(jnp.concatenate([wk_s[rows, sl], qg_s[rows, sl]], axis=0), st) for sl, st in zip(pairs, sts)]
        vnews = [u_s[rows, sl] - ws[0:DN_CHUNK] for sl, ws in zip(pairs, wss)]
        for p, (sl, st, ws, vnew) in enumerate(zip(pairs, sts, wss, vnews)):
            o_s[rows, sl] = ws[DN_CHUNK:] + _bdot(aqk_s[rows, sl], _stack_bd(vnew, bd))
            glast = gcum_s[tail, sl][SUBLANES - 1:SUBLANES, :]
            state_s[p] = st * jnp.exp(glast) + bd * _bdot(kdec_s[rows, sl].T, vnew)
        return carry

    lax.fori_loop(0, tt // (DN_CHUNK * PREP_CHUNKS), prepare, 0)
    lax.fori_loop(0, tt // DN_CHUNK, advance, 0)

    o = o_s[...]
    ms = _sdot(_split(o * o, 2), head_ones) * (1.0 / DN_DK)
    og = o * lax.rsqrt(ms + NORM_EPS) * onw_ref[...] * _silu(z_ref[...].astype(F32))
    y_dn = _bdot(og, wdn_ref[...])
    gates = g_ref[...].astype(F32)
    merged = jax.nn.sigmoid(gates[:, 0:D_MODEL]) * y_sc + jax.nn.sigmoid(gates[:, D_MODEL:]) * y_dn
    y = _bdot(merged, wmix_ref[...])
    gt1 = mod_ref[pl.ds(b, 1), :][:, 2 * D_MODEL:3 * D_MODEL]
    o_ref[...] = x_ref[...] + gt1 * _rms(y, pnw_ref[...])


def _mixer(parts, x2, mod, weights, batch, seq, tt):
    nt = seq // tt
    row = lambda b, t: (b * nt + t, 0)
    const2 = lambda b, t: (0, 0)
    sc, qkv, z, a, bb, g = parts
    data = [(sc, C_SC), (qkv, C_QKV), (z, C_Z), (a, C_A), (bb, C_B), (g, C_G), (x2, D_MODEL)]
    in_specs = [pl.BlockSpec((tt, w), row) for _, w in data]
    in_specs.append(pl.BlockSpec(mod.shape, const2))
    in_specs += [pl.BlockSpec(wt.shape, const2) for wt in weights]
    scratch = [
        pltpu.VMEM((tt + HALO, SC_WIDTH), F32),
        pltpu.VMEM((tt + HALO, C_QKV), F32),
        pltpu.VMEM((tt, DN_KEY), F32),
        pltpu.VMEM((tt, DN_KEY), F32),
        pltpu.VMEM((tt, DN_VAL), F32),
        pltpu.VMEM((tt, DN_KEY), F32),
        pltpu.VMEM((tt, DN_KEY), F32),
        pltpu.VMEM((tt, DN_VAL), F32),
    ] + [pltpu.VMEM((tt, DN_KEY), F32) for _ in range(6)] + [
        pltpu.VMEM((N_PAIRS, PAIR, PAIR), F32),
    ]
    return pl.pallas_call(
        functools.partial(_mixer_kernel, tt=tt),
        grid=(batch, nt),
        in_specs=in_specs,
        out_specs=pl.BlockSpec((tt, D_MODEL), row),
        out_shape=jax.ShapeDtypeStruct((batch * seq, D_MODEL), F32),
        scratch_shapes=scratch,
        compiler_params=pltpu.CompilerParams(
            dimension_semantics=("arbitrary", "arbitrary"), vmem_limit_bytes=VMEM_MIXER_LIMIT),
        name="mixer",
    )(*[d for d, _ in data], mod, *weights)


def _qproj_kernel(x_ref, mod_ref, nw_ref, w_ref, h_ref, q_ref):
    b = pl.program_id(0)
    mod = mod_ref[pl.ds(b, 1), :]
    sh2 = mod[:, 3 * D_MODEL:4 * D_MODEL]
    sc2 = mod[:, 4 * D_MODEL:5 * D_MODEL]
    h = _rms(x_ref[...], nw_ref[...]) * (1.0 + sc2) + sh2
    h_ref[...] = h
    q_ref[...] = _bdot(h, w_ref[...])


def _qproj(x1, mod, nw, wq, batch, seq, tm):
    nt = seq // tm
    row = lambda b, t: (b * nt + t, 0)
    const = lambda b, t: (0, 0)
    n = batch * seq
    nq = wq.shape[1]
    return pl.pallas_call(
        _qproj_kernel,
        grid=(batch, nt),
        in_specs=[
            pl.BlockSpec((tm, D_MODEL), row),
            pl.BlockSpec(mod.shape, const),
            pl.BlockSpec((1, D_MODEL), const),
            pl.BlockSpec(wq.shape, const),
        ],
        out_specs=[pl.BlockSpec((tm, D_MODEL), row), pl.BlockSpec((tm, nq), row)],
        out_shape=[jax.ShapeDtypeStruct((n, D_MODEL), F32), jax.ShapeDtypeStruct((n, nq), F32)],
        compiler_params=pltpu.CompilerParams(dimension_semantics=("arbitrary", "arbitrary")),
        name="qproj",
    )(x1, mod, nw, wq)


def _sort16_network():
    n, pairs, p = 16, [], 1
    while p < n:
        k = p
        while k >= 1:
            for j in range(k % p, n - k, 2 * k):
                for i in range(min(k, n - j - k)):
                    if (i + j) // (2 * p) == (i + j + k) // (2 * p):
                        pairs.append((i + j, i + j + k))
            k //= 2
        p *= 2
    return pairs


def _top16_of_128(st, v_out, i_out):
    k = PEER_TOPK
    vs = [st[SUBLANES * i:SUBLANES * (i + 1), :] for i in range(k)]
    sub = lax.broadcasted_iota(I32, vs[0].shape, 0).astype(F32)
    ix = [sub + float(SUBLANES * i) for i in range(k)]
    for a, b in _sort16_network():
        swap = (vs[b] > vs[a]) | ((vs[b] == vs[a]) & (ix[b] < ix[a]))
        vs[a], vs[b] = jnp.where(swap, vs[b], vs[a]), jnp.where(swap, vs[a], vs[b])
        ix[a], ix[b] = jnp.where(swap, ix[b], ix[a]), jnp.where(swap, ix[a], ix[b])
    for it in range(k):
        m = jnp.max(vs[0], axis=0, keepdims=True)
        pos = jnp.min(jnp.where(vs[0] == m, ix[0], float(PEER_NKEYS)), axis=0, keepdims=True)
        v_out[it:it + 1, :] = m
        i_out[it:it + 1, :] = pos
        hit = ix[0] == pos
        depth = k - 1 - it
        for d in range(depth):
            vs[d] = jnp.where(hit, vs[d + 1], vs[d])
            ix[d] = jnp.where(hit, ix[d + 1], ix[d])


def _top16_of_pairs(v1, i1, v2, i2, v_out, e_out, row0):
    k = PEER_TOPK
    rows = lax.broadcasted_iota(I32, v1.shape, 0).astype(F32)
    cand = v1 + v2[0:1, :]
    taken = jnp.zeros_like(v1)
    for it in range(k):
        m = jnp.max(cand, axis=0, keepdims=True)
        a_win = jnp.min(jnp.where(cand == m, rows, float(k)), axis=0, keepdims=True)
        hit = rows == a_win
        b_win = jnp.sum(jnp.where(hit, taken, 0.0), axis=0, keepdims=True)
        i1_win = jnp.sum(jnp.where(hit, i1, 0.0), axis=0, keepdims=True)
        i2_win = jnp.sum(jnp.where(rows == b_win, i2, 0.0), axis=0, keepdims=True)
        v_out[row0 + it:row0 + it + 1, :] = m
        e_out[row0 + it:row0 + it + 1, :] = i1_win * float(PEER_NKEYS) + i2_win
        if it + 1 < k:
            v1_win = jnp.sum(jnp.where(hit, v1, 0.0), axis=0, keepdims=True)
            nxt = rows == b_win + 1.0
            v2_nxt = jnp.sum(jnp.where(nxt, v2, 0.0), axis=0, keepdims=True)
            new = jnp.where(b_win + 1.0 < float(k), v1_win + v2_nxt, -jnp.inf)
            cand = jnp.where(hit, new, cand)
            taken = jnp.where(hit, b_win + 1.0, taken)


def _topk_kernel(q_ref, keys_ref, e_ref, g_ref, v1_s, i1_s, v2_s, i2_s, ts_s, ei_s, gate_s, *, tt):
    k = PEER_TOPK
    for h in range(PEER_HEADS):
        for p, (v_s, i_s) in enumerate(((v1_s, i1_s), (v2_s, i2_s))):
            col = (2 * h + p) * PEER_HALF
            st = _sdot(_split(keys_ref[2 * h + p], 2), _split(q_ref[:, col:col + PEER_HALF], 2), NT_DIMS)
            _top16_of_128(st, v_s, i_s)
        _top16_of_pairs(v1_s[...], i1_s[...], v2_s[...], i2_s[...], ts_s, ei_s, h * k)
        ts = ts_s[h * k:(h + 1) * k, :]
        e = jnp.exp(ts - jnp.max(ts, axis=0, keepdims=True))
        gate_s[h * k:(h + 1) * k, :] = e / jnp.sum(e, axis=0, keepdims=True)
    e_ref[...] = ei_s[...].T.astype(I32) * HALF_ROWS
    g_ref[...] = gate_s[...].T


def _topk(q, keys, n, tt):
    row = lambda i: (i, 0)
    return pl.pallas_call(
        functools.partial(_topk_kernel, tt=tt),
        grid=(n // tt,),
        in_specs=[
            pl.BlockSpec((tt, q.shape[1]), row),
            pl.BlockSpec(keys.shape, lambda i: (0, 0, 0)),
        ],
        out_specs=[pl.BlockSpec((tt, PEER_E), row)] * 2,
        out_shape=[jax.ShapeDtypeStruct((n, PEER_E), I32), jax.ShapeDtypeStruct((n, PEER_E), F32)],
        scratch_shapes=[
            pltpu.VMEM((PEER_TOPK, tt), F32), pltpu.VMEM((PEER_TOPK, tt), F32),
            pltpu.VMEM((PEER_TOPK, tt), F32), pltpu.VMEM((PEER_TOPK, tt), F32),
            pltpu.VMEM((PEER_E, tt), F32), pltpu.VMEM((PEER_E, tt), F32),
            pltpu.VMEM((PEER_E, tt), F32),
        ],
        compiler_params=pltpu.CompilerParams(dimension_semantics=("arbitrary",)),
        name="topk",
    )(q, keys)


def _staged_tokens(e_ref, idx_s, sem, tt, body):
    n_groups = tt // STAGE_TOKENS
    nb = STAGE_BUFS
    assert n_groups % nb == 0 and STAGE_TOKENS % 2 == 0

    def copy(grp, slot):
        return pltpu.make_async_copy(e_ref.at[pl.ds(grp * STAGE_TOKENS, STAGE_TOKENS)], idx_s.at[slot], sem.at[slot])

    for s in range(nb - 1):
        copy(s, s).start()

    def round_of_groups(i, carry):
        for slot in range(nb):
            grp = nb * i + slot
            copy(jnp.minimum(grp + nb - 1, n_groups - 1), (slot + nb - 1) % nb).start()
            copy(grp, slot).wait()
            for tl in range(STAGE_TOKENS):
                body(grp * STAGE_TOKENS + tl, tl, lambda k, slot=slot, tl=tl: idx_s[slot, tl, k])
        return carry

    lax.fori_loop(0, n_groups // nb, round_of_groups, 0)
    for s in range(nb - 1):
        copy(n_groups - 1, (n_groups + s) % nb).wait()


def _stage_scratch():
    return [pltpu.SMEM((STAGE_BUFS, STAGE_TOKENS, PEER_E), I32), pltpu.SemaphoreType.DMA((STAGE_BUFS,))]


def _gather_rows(tab_ref, idx, g, scr):
    ref = scr[g % N_SCR]
    base = (g // N_SCR) * GROUP_ROWS
    for j in range(GROUP):
        e4 = pl.multiple_of(idx(g * GROUP + j), HALF_ROWS)
        ref[pl.ds(base + j, HALF_ROWS, stride=GROUP), :] = tab_ref[pl.ds(e4, HALF_ROWS), :]


def _group_tiles(g, scr):
    ref = scr[g % N_SCR]
    base = (g // N_SCR) * GROUP_ROWS
    return [ref[base + s * GROUP:base + (s + 1) * GROUP, :] for s in range(HALF_ROWS)]


def _unpack(w):
    return pltpu.bitcast(w << 16, F32), pltpu.bitcast(w & jnp.uint32(0xFFFF0000), F32)


def _row_sums_on_lanes(m, eye, ones):
    hi = m.astype(BF16)
    lo = (m - hi.astype(F32)).astype(BF16)
    sums = jnp.dot(hi, ones, preferred_element_type=F32) + jnp.dot(lo, ones, preferred_element_type=F32)
    return jnp.sum(sums * eye, axis=0, keepdims=True)


def _peer_u_kernel(e_ref, h_ref, g_ref, tab_ref, o_ref, part_s, idx_s, sem, *scr, tt):
    eye = (lax.broadcasted_iota(I32, (PEER_E, LANES), 0) == lax.broadcasted_iota(I32, (PEER_E, LANES), 1)).astype(F32)
    ones = jnp.ones((LANES, LANES), BF16)

    def finish(t, slot):
        act = _row_sums_on_lanes(part_s[slot], eye, ones)
        o_ref[pl.ds(t, 1), :] = g_ref[pl.ds(t, 1), :] * _gelu(act)

    part_s[1] = jnp.zeros((PEER_E, LANES), F32)

    def tok(t, tl, idx):
        finish(jnp.maximum(t - 1, 0), (tl + 1) & 1)
        h_lo = [jnp.broadcast_to(h_ref[t, s:s + 1, :], (GROUP, LANES)) for s in range(HALF_ROWS)]
        h_hi = [jnp.broadcast_to(h_ref[t, HALF_ROWS + s:HALF_ROWS + s + 1, :], (GROUP, LANES))
                for s in range(HALF_ROWS)]
        _gather_rows(tab_ref, idx, 0, scr)
        for g in range(N_GROUPS):
            if g + 1 < N_GROUPS:
                _gather_rows(tab_ref, idx, g + 1, scr)
            r = None
            for s, tile in enumerate(_group_tiles(g, scr)):
                lo, hi = _unpack(tile)
                p = lo * h_lo[s] + hi * h_hi[s]
                r = p if r is None else r + p
            part_s[tl & 1, g * GROUP:(g + 1) * GROUP, :] = r

    _staged_tokens(e_ref, idx_s, sem, tt, tok)
    finish(tt - 1, (tt - 1) & 1)


def _peer_u(e4, h3, gates, tab, n, tt):
    row = lambda i: (i, 0)
    return pl.pallas_call(
        functools.partial(_peer_u_kernel, tt=tt),
        grid=(n // tt,),
        in_specs=[
            pl.BlockSpec((tt, PEER_E), row),
            pl.BlockSpec((tt, SUBLANES, LANES), lambda i: (i, 0, 0)),
            pl.BlockSpec((tt, PEER_E), row),
            pl.BlockSpec(tab.shape, lambda i: (0, 0), pipeline_mode=pl.Buffered(1)),
        ],
        out_specs=pl.BlockSpec((tt, PEER_E), row),
        out_shape=jax.ShapeDtypeStruct((n, PEER_E), F32),
        scratch_shapes=[pltpu.VMEM((2, PEER_E, LANES), F32)] + _stage_scratch()
        + [pltpu.VMEM((N_GROUPS // N_SCR * GROUP_ROWS, LANES), U32) for _ in range(N_SCR)],
        compiler_params=pltpu.CompilerParams(
            dimension_semantics=("arbitrary",), vmem_limit_bytes=VMEM_TABLE_LIMIT),
        name="peer_u",
    )(e4, h3, gates, tab)


def _peer_v_kernel(e_ref, a_ref, tab_ref, o_ref, w_s, idx_s, sem, *scr, tt):
    eye = (lax.broadcasted_iota(I32, (PEER_E, LANES), 0) == lax.broadcasted_iota(I32, (PEER_E, LANES), 1)).astype(F32)
    ones = jnp.ones((LANES, LANES), BF16)

    def spread(t):
        m = eye * a_ref[pl.ds(t, 1), :]
        hi = m.astype(BF16)
        lo = (m - hi.astype(F32)).astype(BF16)
        return jnp.dot(hi, ones, preferred_element_type=F32) + jnp.dot(lo, ones, preferred_element_type=F32)

    w_s[0] = spread(0)

    def tok(t, tl, idx):
        w_next = spread(jnp.minimum(t + 1, tt - 1))
        acc_lo = [None] * HALF_ROWS
        acc_hi = [None] * HALF_ROWS
        _gather_rows(tab_ref, idx, 0, scr)
        for g in range(N_GROUPS):
            if g + 1 < N_GROUPS:
                _gather_rows(tab_ref, idx, g + 1, scr)
            wg = w_s[tl & 1, g * GROUP:(g + 1) * GROUP, :]
            for s, tile in enumerate(_group_tiles(g, scr)):
                lo, hi = _unpack(tile)
                acc_lo[s] = lo * wg if g == 0 else acc_lo[s] + lo * wg
                acc_hi[s] = hi * wg if g == 0 else acc_hi[s] + hi * wg
        for s in range(HALF_ROWS):
            o_ref[t, s:s + 1, :] = jnp.sum(acc_lo[s], axis=0, keepdims=True)
            o_ref[t, HALF_ROWS + s:HALF_ROWS + s + 1, :] = jnp.sum(acc_hi[s], axis=0, keepdims=True)
        w_s[(tl + 1) & 1] = w_next

    _staged_tokens(e_ref, idx_s, sem, tt, tok)


def _peer_v(e4, act, tab, n, tt):
    row = lambda i: (i, 0)
    return pl.pallas_call(
        functools.partial(_peer_v_kernel, tt=tt),
        grid=(n // tt,),
        in_specs=[
            pl.BlockSpec((tt, PEER_E), row),
            pl.BlockSpec((tt, PEER_E), row),
            pl.BlockSpec(tab.shape, lambda i: (0, 0), pipeline_mode=pl.Buffered(1)),
        ],
        out_specs=pl.BlockSpec((tt, SUBLANES, LANES), lambda i: (i, 0, 0)),
        out_shape=jax.ShapeDtypeStruct((n, SUBLANES, LANES), F32),
        scratch_shapes=[pltpu.VMEM((2, PEER_E, LANES), F32)] + _stage_scratch()
        + [pltpu.VMEM((N_GROUPS // N_SCR * GROUP_ROWS, LANES), U32) for _ in range(N_SCR)],
        compiler_params=pltpu.CompilerParams(
            dimension_semantics=("arbitrary",), vmem_limit_bytes=VMEM_TABLE_LIMIT),
        name="peer_v",
    )(e4, act, tab)


def _final_kernel(x_ref, y_ref, mod_ref, nw_ref, o_ref):
    b = pl.program_id(0)
    gt2 = mod_ref[pl.ds(b, 1), :][:, 5 * D_MODEL:6 * D_MODEL]
    o_ref[...] = x_ref[...] + gt2 * _rms(y_ref[...], nw_ref[...])


def _final(x1, y2, mod, nw, batch, seq, tm):
    nt = seq // tm
    row = lambda b, t: (b * nt + t, 0)
    const = lambda b, t: (0, 0)
    return pl.pallas_call(
        _final_kernel,
        grid=(batch, nt),
        in_specs=[
            pl.BlockSpec((tm, D_MODEL), row),
            pl.BlockSpec((tm, D_MODEL), row),
            pl.BlockSpec(mod.shape, const),
            pl.BlockSpec((1, D_MODEL), const),
        ],
        out_specs=pl.BlockSpec((tm, D_MODEL), row),
        out_shape=jax.ShapeDtypeStruct((batch * seq, D_MODEL), F32),
        compiler_params=pltpu.CompilerParams(dimension_semantics=("arbitrary", "arbitrary")),
        name="final",
    )(x1, y2, mod, nw)


def _pack_kernel(x_ref, o_ref):
    half = D_MODEL // 2
    bits = pltpu.bitcast(x_ref[...].astype(BF16).astype(F32), U32)
    w = (bits[:, :half] >> 16) | bits[:, half:]
    for s in range(HALF_ROWS):
        o_ref[pl.ds(s, x_ref.shape[0], stride=HALF_ROWS), :] = w[:, s * LANES:(s + 1) * LANES]


def _pack_table(tab):
    e, d = tab.shape
    te = 512
    assert d == D_MODEL and e % te == 0
    return pl.pallas_call(
        _pack_kernel,
        grid=(e // te,),
        in_specs=[pl.BlockSpec((te, d), lambda i: (i, 0))],
        out_specs=pl.BlockSpec((te * HALF_ROWS, LANES), lambda i: (i, 0)),
        out_shape=jax.ShapeDtypeStruct((e * HALF_ROWS, LANES), U32),
        compiler_params=pltpu.CompilerParams(dimension_semantics=("arbitrary",)),
        name="pack",
    )(tab)


def _tile(seq, cap):
    t = min(seq, cap)
    assert seq % t == 0
    return t


def kernel(x, c, w_ada, b_ada, norm_pre_mix, norm_post_mix, w_in_mix, sc_conv_w, dn_conv_w, dn_a_log,
           dn_dt_bias, dn_out_norm, w_sc_out, w_dn_out, w_mix_out, norm_pre_ffn, norm_post_ffn, peer_w_q,
           peer_sub_keys, peer_u, peer_v):
    batch, seq, d = x.shape
    assert d == D_MODEL and batch <= SUBLANES and seq % DN_CHUNK == 0
    depth = w_ada.shape[0]
    n = batch * seq
    x2 = x.reshape(n, d)
    c_pad = jnp.pad(c, ((0, SUBLANES - batch), (0, 0)))
    tab_u = _pack_table(peer_u)
    tab_v = _pack_table(peer_v)
    rep = lambda v: jnp.repeat(v, DN_DK, axis=-1)
    for l in range(depth):
        mod = _ada(c_pad, w_ada[l], b_ada[l][None, :])
        w = w_in_mix[l]
        o_z = C_SC + C_QKV
        o_a = o_z + C_Z
        w_pad = jnp.concatenate(
            [w[:, :o_a], rep(w[:, o_a:o_a + DN_HEADS]), rep(w[:, o_a + DN_HEADS:o_a + 2 * DN_HEADS]),
             w[:, o_a + 2 * DN_HEADS:]], axis=1).astype(BF16)
        tm = _tile(seq, 512)
        parts = _inproj(x2, mod, norm_pre_mix[l][None, :], w_pad, batch, seq, tm)
        weights = [
            sc_conv_w[l], dn_conv_w[l], rep(dn_a_log[l])[None, :], rep(dn_dt_bias[l])[None, :],
            jnp.tile(dn_out_norm[l], DN_HEADS)[None, :], w_sc_out[l].astype(BF16), w_dn_out[l].astype(BF16),
            w_mix_out[l].astype(BF16), norm_post_mix[l][None, :],
        ]
        x1 = _mixer(parts, x2, mod, weights, batch, seq, _tile(seq, 512))
        h2, q = _qproj(x1, mod, norm_pre_ffn[l][None, :], peer_w_q[l].astype(BF16), batch, seq, tm)
        keys = peer_sub_keys[l].reshape(2 * PEER_HEADS, PEER_NKEYS, PEER_HALF)
        e4, gates = _topk(q, keys, n, _tile(n, LANES))
        tt = _tile(n, 64)
        act = _peer_u(e4, h2.reshape(n, SUBLANES, LANES), gates, tab_u, n, tt)
        y2 = _peer_v(e4, act, tab_v, n, tt).reshape(n, d)
        x2 = _final(x1, y2, mod, norm_post_ffn[l][None, :], batch, seq, tm)
    return x2.reshape(batch, seq, d)
```

```python
import functools

import jax
import jax.numpy as jnp
from jax import lax
from jax.experimental import pallas as pl
from jax.experimental.pallas import tpu as pltpu

F32 = jnp.float32
BF16 = jnp.bfloat16
I32 = jnp.int32
U32 = jnp.uint32

D_MODEL = 1024
SC_WIDTH = 512
SC_KERNEL = 3
DN_HEADS = 8
DN_DK = 64
DN_KEY = DN_HEADS * DN_DK
DN_VAL = DN_KEY
DN_CONV = 4
DN_CHUNK = 64
PEER_HEADS = 8
PEER_NKEYS = 128
PEER_HALF = 128
PEER_QDIM = 256
PEER_TOPK = 16
PEER_E = PEER_HEADS * PEER_TOPK
NORM_EPS = 1e-6

LANES = 128
SUBLANES = 8
HALO = SUBLANES
PAIR = 2 * DN_DK
N_PAIRS = DN_HEADS // 2
INV_TERMS = 2
PREP_CHUNKS = 2
HALF_ROWS = SUBLANES // 2
GROUP = SUBLANES
GROUP_ROWS = HALF_ROWS * GROUP
N_GROUPS = PEER_E // GROUP
N_SCR = 4
STAGE_TOKENS = 8
STAGE_BUFS = 2
VMEM_TABLE_LIMIT = 56 * 1024 * 1024
VMEM_MIXER_LIMIT = 56 * 1024 * 1024

C_SC = 3 * SC_WIDTH
C_QKV = 2 * DN_KEY + DN_VAL
C_Z = DN_VAL
C_A = DN_KEY
C_B = DN_KEY
C_G = 2 * D_MODEL
C_ALL = C_SC + C_QKV + C_Z + C_A + C_B + C_G


def _silu(x):
    return x * jax.nn.sigmoid(x)


def _softplus(x):
    return jnp.maximum(x, 0.0) + jnp.log(1.0 + jnp.exp(-jnp.abs(x)))


def _gelu(x):
    return 0.5 * x * (1.0 + lax.erf(x * (2.0 ** -0.5)))


def _rms(x, w):
    return x * lax.rsqrt(jnp.mean(x * x, axis=-1, keepdims=True) + NORM_EPS) * w


def _bdot(a, b):
    return jnp.dot(a.astype(BF16), b.astype(BF16), preferred_element_type=F32)


def _xdot(a, b):
    return jnp.dot(a, b, preferred_element_type=F32, precision=lax.Precision.HIGHEST)


def _split(x, n):
    terms = []
    for i in range(n):
        t = x.astype(BF16)
        terms.append(t)
        if i + 1 < n:
            x = x - t.astype(F32)
    return terms


NN_DIMS = (((1,), (0,)), ((), ()))
NT_DIMS = (((1,), (1,)), ((), ()))


def _sdot(a_terms, b_terms, dims=NN_DIMS):
    order = max(len(a_terms), len(b_terms))
    out = None
    for i, a in enumerate(a_terms):
        for j, b in enumerate(b_terms):
            if i + j < order:
                p = lax.dot_general(a, b, dims, preferred_element_type=F32)
                out = p if out is None else out + p
    return out


def _ada_kernel(c_ref, w_ref, b_ref, o_ref):
    o_ref[...] = _bdot(_silu(c_ref[...]), w_ref[...]) + b_ref[...]


def _ada(c_pad, w_ada, b_ada):
    n_out = w_ada.shape[1]
    tn = 1024
    return pl.pallas_call(
        _ada_kernel,
        grid=(n_out // tn,),
        in_specs=[
            pl.BlockSpec((SUBLANES, D_MODEL), lambda j: (0, 0)),
            pl.BlockSpec((D_MODEL, tn), lambda j: (0, j)),
            pl.BlockSpec((1, tn), lambda j: (0, j)),
        ],
        out_specs=pl.BlockSpec((SUBLANES, tn), lambda j: (0, j)),
        out_shape=jax.ShapeDtypeStruct((SUBLANES, n_out), F32),
        name="ada",
    )(c_pad, w_ada, b_ada)


def _inproj_kernel(x_ref, mod_ref, nw_ref, w_ref, sc_ref, qkv_ref, z_ref, a_ref, b_ref, g_ref):
    b = pl.program_id(0)
    mod = mod_ref[pl.ds(b, 1), :]
    sh1 = mod[:, 0:D_MODEL]
    sc1 = mod[:, D_MODEL:2 * D_MODEL]
    h = (_rms(x_ref[...], nw_ref[...]) * (1.0 + sc1) + sh1).astype(BF16)
    col = 0
    for ref, width in ((sc_ref, C_SC), (qkv_ref, C_QKV), (z_ref, C_Z), (a_ref, C_A), (b_ref, C_B), (g_ref, C_G)):
        for j in range(0, width, 512):
            ref[:, j:j + 512] = jnp.dot(h, w_ref[:, col + j:col + j + 512],
                                        preferred_element_type=F32).astype(ref.dtype)
        col += width


def _inproj(x2, mod, nw, w_pad, batch, seq, tm):
    nt = seq // tm
    row = lambda b, t: (b * nt + t, 0)
    const = lambda b, t: (0, 0)
    n = batch * seq
    outs = [(C_SC, BF16), (C_QKV, BF16), (C_Z, BF16), (C_A, F32), (C_B, F32), (C_G, BF16)]
    return pl.pallas_call(
        _inproj_kernel,
        grid=(batch, nt),
        in_specs=[
            pl.BlockSpec((tm, D_MODEL), row),
            pl.BlockSpec(mod.shape, const),
            pl.BlockSpec((1, D_MODEL), const),
            pl.BlockSpec((D_MODEL, C_ALL), const, pipeline_mode=pl.Buffered(1)),
        ],
        out_specs=[pl.BlockSpec((tm, w), row) for w, _ in outs],
        out_shape=[jax.ShapeDtypeStruct((n, w), dt) for w, dt in outs],
        compiler_params=pltpu.CompilerParams(
            dimension_semantics=("arbitrary", "arbitrary"), vmem_limit_bytes=VMEM_MIXER_LIMIT),
        name="inproj",
    )(x2, mod, nw, w_pad)


def _pair_consts():
    r = lax.broadcasted_iota(I32, (PAIR, PAIR), 0)
    c = lax.broadcasted_iota(I32, (PAIR, PAIR), 1)
    bd = ((r >> 6) == (c >> 6)).astype(F32)
    i = lax.broadcasted_iota(I32, (DN_CHUNK, PAIR), 0)
    j = lax.broadcasted_iota(I32, (DN_CHUNK, PAIR), 1) & (DN_DK - 1)
    return bd, i >= j, i > j, (i == j).astype(F32)


def _stack_bd(y, bd):
    return jnp.concatenate([y, y], axis=0) * bd


def _mixer_kernel(sc_ref, qkv_ref, z_ref, a_ref, b_ref, g_ref, x_ref, mod_ref,
                  scw_ref, dnw_ref, alog_ref, dtb_ref, onw_ref, wsc_ref, wdn_ref, wmix_ref, pnw_ref,
                  o_ref,
                  scx_s, qkvx_s, q_s, k_s, v_s, g_s, beta_s, o_s, gcum_s, aqk_s, u_s, wk_s, qg_s, kdec_s,
                  state_s, *, tt):
    b = pl.program_id(0)
    t = pl.program_id(1)

    @pl.when(t == 0)
    def _():
        scx_s[0:HALO, :] = jnp.zeros((HALO, SC_WIDTH), F32)
        qkvx_s[0:HALO, :] = jnp.zeros((HALO, C_QKV), F32)
        state_s[...] = jnp.zeros(state_s.shape, F32)

    sc = sc_ref[...].astype(F32)
    scx_s[HALO:HALO + tt, :] = sc[:, SC_WIDTH:2 * SC_WIDTH] * sc[:, 2 * SC_WIDTH:]
    conv = scw_ref[0:1, :] * scx_s[pl.ds(HALO - (SC_KERNEL - 1), tt), :]
    for kk in range(1, SC_KERNEL):
        conv = conv + scw_ref[kk:kk + 1, :] * scx_s[pl.ds(HALO - (SC_KERNEL - 1) + kk, tt), :]
    y_sc = _bdot(sc[:, 0:SC_WIDTH] * conv, wsc_ref[...])
    scx_s[0:HALO, :] = scx_s[tt:tt + HALO, :]

    qkvx_s[HALO:HALO + tt, :] = qkv_ref[...].astype(F32)
    cq = dnw_ref[0:1, :] * qkvx_s[pl.ds(HALO - (DN_CONV - 1), tt), :]
    for kk in range(1, DN_CONV):
        cq = cq + dnw_ref[kk:kk + 1, :] * qkvx_s[pl.ds(HALO - (DN_CONV - 1) + kk, tt), :]
    qkvx_s[0:HALO, :] = qkvx_s[tt:tt + HALO, :]
    cq = _silu(cq)
    r512 = lax.broadcasted_iota(I32, (DN_KEY, DN_KEY), 0) >> 6
    c512 = lax.broadcasted_iota(I32, (DN_KEY, DN_KEY), 1) >> 6
    head_ones = [(r512 == c512).astype(BF16)]
    q = cq[:, 0:DN_KEY]
    k = cq[:, DN_KEY:2 * DN_KEY]
    q_s[...] = q * lax.rsqrt(_sdot(_split(q * q, 2), head_ones) + NORM_EPS) * (DN_DK ** -0.5)
    k_s[...] = k * lax.rsqrt(_sdot(_split(k * k, 2), head_ones) + NORM_EPS)
    v_s[...] = cq[:, 2 * DN_KEY:]
    beta_s[...] = jax.nn.sigmoid(b_ref[...])
    g_s[...] = -jnp.exp(alog_ref[...]) * _softplus(a_ref[...] + dtb_ref[...])

    bd, incl, strict, eye = _pair_consts()
    bd_b = bd.astype(BF16)
    ri = lax.broadcasted_iota(I32, (DN_CHUNK, DN_CHUNK), 0)
    ci = lax.broadcasted_iota(I32, (DN_CHUNK, DN_CHUNK), 1)
    lt = [(ci <= ri).astype(BF16)]
    nt = (((1,), (1,)), ((), ()))

    def stack_terms(y):
        return [jnp.concatenate([t_, t_], axis=0) * bd_b for t_ in _split(y, INV_TERMS)]

    pairs = [slice(p * PAIR, (p + 1) * PAIR) for p in range(N_PAIRS)]

    def prepare(c, carry):
        prob = []
        for cc in range(PREP_CHUNKS):
            rows = pl.ds(pl.multiple_of((c * PREP_CHUNKS + cc) * DN_CHUNK, DN_CHUNK), DN_CHUNK)
            gcum_all = _sdot(lt, _split(g_s[rows, :], 3))
            gcum_s[rows, :] = gcum_all
            prob += [(rows, sl, gcum_all[:, sl]) for sl in pairs]
        pws, invs, rest = [], [], []
        for rows, sl, gc in prob:
            qp = q_s[rows, sl]
            kp = k_s[rows, sl]
            bp = beta_s[rows, sl]
            grow = jnp.sum(gc * eye, axis=0, keepdims=True)
            dec = jnp.where(incl, jnp.exp(jnp.where(incl, gc - grow, 0.0)), 0.0)
            eg = jnp.exp(gc)
            kb = kp * bp
            kbig = _stack_bd(kp, bd).astype(BF16)
            kk_s = lax.dot_general(kb.astype(BF16), kbig, nt, preferred_element_type=F32)
            qk_s = lax.dot_general(qp.astype(BF16), kbig, nt, preferred_element_type=F32)
            aqk_s[rows, sl] = qk_s * dec
            qg_s[rows, sl] = qp * eg
            kdec_s[rows, sl] = kp * jnp.exp(gc[DN_CHUNK - 1:DN_CHUNK, :] - gc)
            pw = -(kk_s * jnp.where(strict, dec, 0.0))
            pws.append(pw)
            invs.append(eye + pw)
            rest.append((v_s[rows, sl] * bp, kb * eg))
        for _ in range(5):
            pws = [_sdot(_split(pw, INV_TERMS), stack_terms(pw)) for pw in pws]
            invs = [inv + _sdot(_split(inv, INV_TERMS), stack_terms(pw)) for inv, pw in zip(invs, pws)]
        for (rows, sl, _), inv, (vb, kbg) in zip(prob, invs, rest):
            u_s[rows, sl] = _bdot(inv, _stack_bd(vb, bd))
            wk_s[rows, sl] = _bdot(inv, _stack_bd(kbg, bd))
        return carry

    def advance(c, carry):
        rows = pl.ds(pl.multiple_of(c * DN_CHUNK, DN_CHUNK), DN_CHUNK)
        tail = pl.ds(pl.multiple_of(c * DN_CHUNK + DN_CHUNK - SUBLANES, SUBLANES), SUBLANES)
        sts = [state_s[p] for p in range(N_PAIRS)]
        wss = [_bdot(jnp.concatenate([wk_s[rows, sl], qg_s[rows, sl]], axis=0), st) for sl, st in zip(pairs, sts)]
        vnews = [u_s[rows, sl] - ws[0:DN_CHUNK] for sl, ws in zip(pairs, wss)]
        for p, (sl, st, ws, vnew) in enumerate(zip(pairs, sts, wss, vnews)):
            o_s[rows, sl] = ws[DN_CHUNK:] + _bdot(aqk_s[rows, sl], _stack_bd(vnew, bd))
            glast = gcum_s[tail, sl][SUBLANES - 1:SUBLANES, :]
            state_s[p] = st * jnp.exp(glast) + bd * _bdot(kdec_s[rows, sl].T, vnew)
        return carry

    lax.fori_loop(0, tt // (DN_CHUNK * PREP_CHUNKS), prepare, 0)
    lax.fori_loop(0, tt // DN_CHUNK, advance, 0)

    o = o_s[...]
    ms = _sdot(_split(o * o, 2), head_ones) * (1.0 / DN_DK)
    og = o * lax.rsqrt(ms + NORM_EPS) * onw_ref[...] * _silu(z_ref[...].astype(F32))
    y_dn = _bdot(og, wdn_ref[...])
    gates = g_ref[...].astype(F32)
    merged = jax.nn.sigmoid(gates[:, 0:D_MODEL]) * y_sc + jax.nn.sigmoid(gates[:, D_MODEL:]) * y_dn
    y = _bdot(merged, wmix_ref[...])
    gt1 = mod_ref[pl.ds(b, 1), :][:, 2 * D_MODEL:3 * D_MODEL]
    o_ref[...] = x_ref[...] + gt1 * _rms(y, pnw_ref[...])


def _mixer(parts, x2, mod, weights, batch, seq, tt):
    nt = seq // tt
    row = lambda b, t: (b * nt + t, 0)
    const2 = lambda b, t: (0, 0)
    sc, qkv, z, a, bb, g = parts
    data = [(sc, C_SC), (qkv, C_QKV), (z, C_Z), (a, C_A), (bb, C_B), (g, C_G), (x2, D_MODEL)]
    in_specs = [pl.BlockSpec((tt, w), row) for _, w in data]
    in_specs.append(pl.BlockSpec(mod.shape, const2))
    in_specs += [pl.BlockSpec(wt.shape, const2) for wt in weights]
    scratch = [
        pltpu.VMEM((tt + HALO, SC_WIDTH), F32),
        pltpu.VMEM((tt + HALO, C_QKV), F32),
        pltpu.VMEM((tt, DN_KEY), F32),
        pltpu.VMEM((tt, DN_KEY), F32),
        pltpu.VMEM((tt, DN_VAL), F32),
        pltpu.VMEM((tt, DN_KEY), F32),
        pltpu.VMEM((tt, DN_KEY), F32),
        pltpu.VMEM((tt, DN_VAL), F32),
    ] + [pltpu.VMEM((tt, DN_KEY), F32) for _ in range(6)] + [
        pltpu.VMEM((N_PAIRS, PAIR, PAIR), F32),
    ]
    return pl.pallas_call(
        functools.partial(_mixer_kernel, tt=tt),
        grid=(batch, nt),
        in_specs=in_specs,
        out_specs=pl.BlockSpec((tt, D_MODEL), row),
        out_shape=jax.ShapeDtypeStruct((batch * seq, D_MODEL), F32),
        scratch_shapes=scratch,
        compiler_params=pltpu.CompilerParams(
            dimension_semantics=("arbitrary", "arbitrary"), vmem_limit_bytes=VMEM_MIXER_LIMIT),
        name="mixer",
    )(*[d for d, _ in data], mod, *weights)


def _qproj_kernel(x_ref, mod_ref, nw_ref, w_ref, h_ref, q_ref):
    b = pl.program_id(0)
    mod = mod_ref[pl.ds(b, 1), :]
    sh2 = mod[:, 3 * D_MODEL:4 * D_MODEL]
    sc2 = mod[:, 4 * D_MODEL:5 * D_MODEL]
    h = _rms(x_ref[...], nw_ref[...]) * (1.0 + sc2) + sh2
    h_ref[...] = h
    q_ref[...] = _bdot(h, w_ref[...])


def _qproj(x1, mod, nw, wq, batch, seq, tm):
    nt = seq // tm
    row = lambda b, t: (b * nt + t, 0)
    const = lambda b, t: (0, 0)
    n = batch * seq
    nq = wq.shape[1]
    return pl.pallas_call(
        _qproj_kernel,
        grid=(batch, nt),
        in_specs=[
            pl.BlockSpec((tm, D_MODEL), row),
            pl.BlockSpec(mod.shape, const),
            pl.BlockSpec((1, D_MODEL), const),
            pl.BlockSpec(wq.shape, const),
        ],
        out_specs=[pl.BlockSpec((tm, D_MODEL), row), pl.BlockSpec((tm, nq), row)],
        out_shape=[jax.ShapeDtypeStruct((n, D_MODEL), F32), jax.ShapeDtypeStruct((n, nq), F32)],
        compiler_params=pltpu.CompilerParams(dimension_semantics=("arbitrary", "arbitrary")),
        name="qproj",
    )(x1, mod, nw, wq)


def _sort16_network():
    n, pairs, p = 16, [], 1
    while p < n:
        k = p
        while k >= 1:
            for j in range(k % p, n - k, 2 * k):
                for i in range(min(k, n - j - k)):
                    if (i + j) // (2 * p) == (i + j + k) // (2 * p):
                        pairs.append((i + j, i + j + k))
            k //= 2
        p *= 2
    return pairs


def _top16_of_128(st, v_out, i_out):
    k = PEER_TOPK
    vs = [st[SUBLANES * i:SUBLANES * (i + 1), :] for i in range(k)]
    sub = lax.broadcasted_iota(I32, vs[0].shape, 0).astype(F32)
    ix = [sub + float(SUBLANES * i) for i in range(k)]
    for a, b in _sort16_network():
        swap = (vs[b] > vs[a]) | ((vs[b] == vs[a]) & (ix[b] < ix[a]))
        vs[a], vs[b] = jnp.where(swap, vs[b], vs[a]), jnp.where(swap, vs[a], vs[b])
        ix[a], ix[b] = jnp.where(swap, ix[b], ix[a]), jnp.where(swap, ix[a], ix[b])
    for it in range(k):
        m = jnp.max(vs[0], axis=0, keepdims=True)
        pos = jnp.min(jnp.where(vs[0] == m, ix[0], float(PEER_NKEYS)), axis=0, keepdims=True)
        v_out[it:it + 1, :] = m
        i_out[it:it + 1, :] = pos
        hit = ix[0] == pos
        depth = k - 1 - it
        for d in range(depth):
            vs[d] = jnp.where(hit, vs[d + 1], vs[d])
            ix[d] = jnp.where(hit, ix[d + 1], ix[d])


def _top16_of_pairs(v1, i1, v2, i2, v_out, e_out, row0):
    k = PEER_TOPK
    rows = lax.broadcasted_iota(I32, v1.shape, 0).astype(F32)
    cand = v1 + v2[0:1, :]
    taken = jnp.zeros_like(v1)
    for it in range(k):
        m = jnp.max(cand, axis=0, keepdims=True)
        a_win = jnp.min(jnp.where(cand == m, rows, float(k)), axis=0, keepdims=True)
        hit = rows == a_win
        b_win = jnp.sum(jnp.where(hit, taken, 0.0), axis=0, keepdims=True)
        i1_win = jnp.sum(jnp.where(hit, i1, 0.0), axis=0, keepdims=True)
        i2_win = jnp.sum(jnp.where(rows == b_win, i2, 0.0), axis=0, keepdims=True)
        v_out[row0 + it:row0 + it + 1, :] = m
        e_out[row0 + it:row0 + it + 1, :] = i1_win * float(PEER_NKEYS) + i2_win
        if it + 1 < k:
            v1_win = jnp.sum(jnp.where(hit, v1, 0.0), axis=0, keepdims=True)
            nxt = rows == b_win + 1.0
            v2_nxt = jnp.sum(jnp.where(nxt, v2, 0.0), axis=0, keepdims=True)
            new = jnp.where(b_win + 1.0 < float(k), v1_win + v2_nxt, -jnp.inf)
            cand = jnp.where(hit, new, cand)
            taken = jnp.where(hit, b_win + 1.0, taken)


def _topk_kernel(q_ref, keys_ref, e_ref, g_ref, v1_s, i1_s, v2_s, i2_s, ts_s, ei_s, gate_s, *, tt):
    k = PEER_TOPK
    for h in range(PEER_HEADS):
        for p, (v_s, i_s) in enumerate(((v1_s, i1_s), (v2_s, i2_s))):
            col = (2 * h + p) * PEER_HALF
            st = _sdot(_split(keys_ref[2 * h + p], 2), _split(q_ref[:, col:col + PEER_HALF], 2), NT_DIMS)
            _top16_of_128(st, v_s, i_s)
        _top16_of_pairs(v1_s[...], i1_s[...], v2_s[...], i2_s[...], ts_s, ei_s, h * k)
        ts = ts_s[h * k:(h + 1) * k, :]
        e = jnp.exp(ts - jnp.max(ts, axis=0, keepdims=True))
        gate_s[h * k:(h + 1) * k, :] = e / jnp.sum(e, axis=0, keepdims=True)
    e_ref[...] = ei_s[...].T.astype(I32) * HALF_ROWS
    g_ref[...] = gate_s[...].T


def _topk(q, keys, n, tt):
    row = lambda i: (i, 0)
    return pl.pallas_call(
        functools.partial(_topk_kernel, tt=tt),
        grid=(n // tt,),
        in_specs=[
            pl.BlockSpec((tt, q.shape[1]), row),
            pl.BlockSpec(keys.shape, lambda i: (0, 0, 0)),
        ],
        out_specs=[pl.BlockSpec((tt, PEER_E), row)] * 2,
        out_shape=[jax.ShapeDtypeStruct((n, PEER_E), I32), jax.ShapeDtypeStruct((n, PEER_E), F32)],
        scratch_shapes=[
            pltpu.VMEM((PEER_TOPK, tt), F32), pltpu.VMEM((PEER_TOPK, tt), F32),
            pltpu.VMEM((PEER_TOPK, tt), F32), pltpu.VMEM((PEER_TOPK, tt), F32),
            pltpu.VMEM((PEER_E, tt), F32), pltpu.VMEM((PEER_E, tt), F32),
            pltpu.VMEM((PEER_E, tt), F32),
        ],
        compiler_params=pltpu.CompilerParams(dimension_semantics=("arbitrary",)),
        name="topk",
    )(q, keys)


def _staged_tokens(srcs, bufs, sems, tt, body):
    n_groups = tt // STAGE_TOKENS
    nb = STAGE_BUFS
    assert n_groups % nb == 0 and STAGE_TOKENS % 2 == 0

    def copies(grp, slot):
        return [pltpu.make_async_copy(src.at[pl.ds(grp * STAGE_TOKENS, STAGE_TOKENS)], buf.at[slot], sem.at[slot])
                for src, buf, sem in zip(srcs, bufs, sems)]

    for s in range(nb - 1):
        for c in copies(s, s):
            c.start()

    def round_of_groups(i, carry):
        for slot in range(nb):
            grp = nb * i + slot
            for c in copies(jnp.minimum(grp + nb - 1, n_groups - 1), (slot + nb - 1) % nb):
                c.start()
            for c in copies(grp, slot):
                c.wait()
            for tl in range(STAGE_TOKENS):
                body(grp * STAGE_TOKENS + tl, tl,
                     *[lambda k, buf=buf, slot=slot, tl=tl: buf[slot, tl, k] for buf in bufs])
        return carry

    lax.fori_loop(0, n_groups // nb, round_of_groups, 0)
    for s in range(nb - 1):
        for c in copies(n_groups - 1, (n_groups + s) % nb):
            c.wait()


def _stage_scratch(dtype):
    return [pltpu.SMEM((STAGE_BUFS, STAGE_TOKENS, PEER_E), dtype), pltpu.SemaphoreType.DMA((STAGE_BUFS,))]


def _pair_rows(tab_ref, idx, m):
    wa = tab_ref[pl.ds(pl.multiple_of(idx(2 * m), HALF_ROWS), HALF_ROWS), :]
    wb = tab_ref[pl.ds(pl.multiple_of(idx(2 * m + 1), HALF_ROWS), HALF_ROWS), :]
    return jnp.concatenate([wa, wb], axis=0)


def _gather_rows(tab_ref, idx, g, scr):
    ref = scr[g % N_SCR]
    base = (g // N_SCR) * GROUP_ROWS
    for j in range(GROUP):
        e4 = pl.multiple_of(idx(g * GROUP + j), HALF_ROWS)
        ref[pl.ds(base + j, HALF_ROWS, stride=GROUP), :] = tab_ref[pl.ds(e4, HALF_ROWS), :]


def _group_tiles(g, scr):
    ref = scr[g % N_SCR]
    base = (g // N_SCR) * GROUP_ROWS
    return [ref[base + s * GROUP:base + (s + 1) * GROUP, :] for s in range(HALF_ROWS)]


def _unpack(w):
    return pltpu.bitcast(w << 16, F32), pltpu.bitcast(w & jnp.uint32(0xFFFF0000), F32)


def _row_sums_on_lanes(m, eye, ones):
    hi = m.astype(BF16)
    lo = (m - hi.astype(F32)).astype(BF16)
    sums = jnp.dot(hi, ones, preferred_element_type=F32) + jnp.dot(lo, ones, preferred_element_type=F32)
    return jnp.sum(sums * eye, axis=0, keepdims=True)


def _peer_u_kernel(e_ref, h_ref, g_ref, tab_ref, o_ref, part_s, idx_s, sem, *scr, tt):
    eye = (lax.broadcasted_iota(I32, (PEER_E, LANES), 0) == lax.broadcasted_iota(I32, (PEER_E, LANES), 1)).astype(F32)
    ones = jnp.ones((LANES, LANES), BF16)

    def finish(t, slot):
        act = _row_sums_on_lanes(part_s[slot], eye, ones)
        o_ref[pl.ds(t, 1), :] = g_ref[pl.ds(t, 1), :] * _gelu(act)

    part_s[1] = jnp.zeros((PEER_E, LANES), F32)

    def tok(t, tl, idx):
        finish(jnp.maximum(t - 1, 0), (tl + 1) & 1)
        h_lo = [jnp.broadcast_to(h_ref[t, s:s + 1, :], (GROUP, LANES)) for s in range(HALF_ROWS)]
        h_hi = [jnp.broadcast_to(h_ref[t, HALF_ROWS + s:HALF_ROWS + s + 1, :], (GROUP, LANES))
                for s in range(HALF_ROWS)]
        _gather_rows(tab_ref, idx, 0, scr)
        for g in range(N_GROUPS):
            if g + 1 < N_GROUPS:
                _gather_rows(tab_ref, idx, g + 1, scr)
            r = None
            for s, tile in enumerate(_group_tiles(g, scr)):
                lo, hi = _unpack(tile)
                p = lo * h_lo[s] + hi * h_hi[s]
                r = p if r is None else r + p
            part_s[tl & 1, g * GROUP:(g + 1) * GROUP, :] = r

    _staged_tokens([e_ref], [idx_s], [sem], tt, tok)
    finish(tt - 1, (tt - 1) & 1)


def _peer_u(e4, h3, gates, tab, n, tt):
    row = lambda i: (i, 0)
    return pl.pallas_call(
        functools.partial(_peer_u_kernel, tt=tt),
        grid=(n // tt,),
        in_specs=[
            pl.BlockSpec((tt, PEER_E), row),
            pl.BlockSpec((tt, SUBLANES, LANES), lambda i: (i, 0, 0)),
            pl.BlockSpec((tt, PEER_E), row),
            pl.BlockSpec(tab.shape, lambda i: (0, 0), pipeline_mode=pl.Buffered(1)),
        ],
        out_specs=pl.BlockSpec((tt, PEER_E), row),
        out_shape=jax.ShapeDtypeStruct((n, PEER_E), F32),
        scratch_shapes=[pltpu.VMEM((2, PEER_E, LANES), F32)] + _stage_scratch(I32)
        + [pltpu.VMEM((N_GROUPS // N_SCR * GROUP_ROWS, LANES), U32) for _ in range(N_SCR)],
        compiler_params=pltpu.CompilerParams(
            dimension_semantics=("arbitrary",), vmem_limit_bytes=VMEM_TABLE_LIMIT),
        name="peer_u",
    )(e4, h3, gates, tab)


def _peer_v_kernel(e_ref, a_ref, tab_ref, o_ref, idx_s, sem_i, wgt_s, sem_w, *, tt):
    top = lax.broadcasted_iota(I32, (SUBLANES, LANES), 0) < HALF_ROWS

    def tok(t, tl, idx, wgt):
        acc_lo = [jnp.zeros((SUBLANES, LANES), F32) for _ in range(2)]
        acc_hi = [jnp.zeros((SUBLANES, LANES), F32) for _ in range(2)]
        for m in range(PEER_E // 2):
            lo, hi = _unpack(_pair_rows(tab_ref, idx, m))
            w = jnp.where(top, wgt(2 * m), wgt(2 * m + 1))
            acc_lo[m % 2] = acc_lo[m % 2] + w * lo
            acc_hi[m % 2] = acc_hi[m % 2] + w * hi
        lo = acc_lo[0] + acc_lo[1]
        hi = acc_hi[0] + acc_hi[1]
        o_ref[t, 0:HALF_ROWS, :] = lo[0:HALF_ROWS] + lo[HALF_ROWS:]
        o_ref[t, HALF_ROWS:, :] = hi[0:HALF_ROWS] + hi[HALF_ROWS:]

    _staged_tokens([e_ref, a_ref], [idx_s, wgt_s], [sem_i, sem_w], tt, tok)


def _peer_v(e4, act, tab, n, tt):
    row = lambda i: (i, 0)
    return pl.pallas_call(
        functools.partial(_peer_v_kernel, tt=tt),
        grid=(n // tt,),
        in_specs=[
            pl.BlockSpec((tt, PEER_E), row),
            pl.BlockSpec((tt, PEER_E), row),
            pl.BlockSpec(tab.shape, lambda i: (0, 0), pipeline_mode=pl.Buffered(1)),
        ],
        out_specs=pl.BlockSpec((tt, SUBLANES, LANES), lambda i: (i, 0, 0)),
        out_shape=jax.ShapeDtypeStruct((n, SUBLANES, LANES), F32),
        scratch_shapes=_stage_scratch(I32) + _stage_scratch(F32),
        compiler_params=pltpu.CompilerParams(
            dimension_semantics=("arbitrary",), vmem_limit_bytes=VMEM_TABLE_LIMIT),
        name="peer_v",
    )(e4, act, tab)


def _final_kernel(x_ref, y_ref, mod_ref, nw_ref, o_ref):
    b = pl.program_id(0)
    gt2 = mod_ref[pl.ds(b, 1), :][:, 5 * D_MODEL:6 * D_MODEL]
    o_ref[...] = x_ref[...] + gt2 * _rms(y_ref[...], nw_ref[...])


def _final(x1, y2, mod, nw, batch, seq, tm):
    nt = seq // tm
    row = lambda b, t: (b * nt + t, 0)
    const = lambda b, t: (0, 0)
    return pl.pallas_call(
        _final_kernel,
        grid=(batch, nt),
        in_specs=[
            pl.BlockSpec((tm, D_MODEL), row),
            pl.BlockSpec((tm, D_MODEL), row),
            pl.BlockSpec(mod.shape, const),
            pl.BlockSpec((1, D_MODEL), const),
        ],
        out_specs=pl.BlockSpec((tm, D_MODEL), row),
        out_shape=jax.ShapeDtypeStruct((batch * seq, D_MODEL), F32),
        compiler_params=pltpu.CompilerParams(dimension_semantics=("arbitrary", "arbitrary")),
        name="final",
    )(x1, y2, mod, nw)


def _pack_kernel(x_ref, o_ref):
    half = D_MODEL // 2
    bits = pltpu.bitcast(x_ref[...].astype(BF16).astype(F32), U32)
    w = (bits[:, :half] >> 16) | bits[:, half:]
    for s in range(HALF_ROWS):
        o_ref[pl.ds(s, x_ref.shape[0], stride=HALF_ROWS), :] = w[:, s * LANES:(s + 1) * LANES]


def _pack_table(tab):
    e, d = tab.shape
    te = 512
    assert d == D_MODEL and e % te == 0
    return pl.pallas_call(
        _pack_kernel,
        grid=(e // te,),
        in_specs=[pl.BlockSpec((te, d), lambda i: (i, 0))],
        out_specs=pl.BlockSpec((te * HALF_ROWS, LANES), lambda i: (i, 0)),
        out_shape=jax.ShapeDtypeStruct((e * HALF_ROWS, LANES), U32),
        compiler_params=pltpu.CompilerParams(dimension_semantics=("arbitrary",)),
        name="pack",
    )(tab)


def _tile(seq, cap):
    t = min(seq, cap)
    assert seq % t == 0
    return t


def kernel(x, c, w_ada, b_ada, norm_pre_mix, norm_post_mix, w_in_mix, sc_conv_w, dn_conv_w, dn_a_log,
           dn_dt_bias, dn_out_norm, w_sc_out, w_dn_out, w_mix_out, norm_pre_ffn, norm_post_ffn, peer_w_q,
           peer_sub_keys, peer_u, peer_v):
    batch, seq, d = x.shape
    assert d == D_MODEL and batch <= SUBLANES and seq % DN_CHUNK == 0
    depth = w_ada.shape[0]
    n = batch * seq
    x2 = x.reshape(n, d)
    c_pad = jnp.pad(c, ((0, SUBLANES - batch), (0, 0)))
    tab_u = _pack_table(peer_u)
    tab_v = _pack_table(peer_v)
    rep = lambda v: jnp.repeat(v, DN_DK, axis=-1)
    for l in range(depth):
        mod = _ada(c_pad, w_ada[l], b_ada[l][None, :])
        w = w_in_mix[l]
        o_z = C_SC + C_QKV
        o_a = o_z + C_Z
        w_pad = jnp.concatenate(
            [w[:, :o_a], rep(w[:, o_a:o_a + DN_HEADS]), rep(w[:, o_a + DN_HEADS:o_a + 2 * DN_HEADS]),
             w[:, o_a + 2 * DN_HEADS:]], axis=1).astype(BF16)
        tm = _tile(seq, 512)
        parts = _inproj(x2, mod, norm_pre_mix[l][None, :], w_pad, batch, seq, tm)
        weights = [
            sc_conv_w[l], dn_conv_w[l], rep(dn_a_log[l])[None, :], rep(dn_dt_bias[l])[None, :],
            jnp.tile(dn_out_norm[l], DN_HEADS)[None, :], w_sc_out[l].astype(BF16), w_dn_out[l].astype(BF16),
            w_mix_out[l].astype(BF16), norm_post_mix[l][None, :],
        ]
        x1 = _mixer(parts, x2, mod, weights, batch, seq, _tile(seq, 512))
        h2, q = _qproj(x1, mod, norm_pre_ffn[l][None, :], peer_w_q[l].astype(BF16), batch, seq, tm)
        keys = peer_sub_keys[l].reshape(2 * PEER_HEADS, PEER_NKEYS, PEER_HALF)
        e4, gates = _topk(q, keys, n, _tile(n, LANES))
        tt = _tile(n, 64)
        act = _peer_u(e4, h2.reshape(n, SUBLANES, LANES), gates, tab_u, n, tt)
        y2 = _peer_v(e4, act, tab_v, n, tt).reshape(n, d)
        x2 = _final(x1, y2, mod, norm_post_ffn[l][None, :], batch, seq, tm)
    return x2.reshape(batch, seq, d)
```

```python
import functools

import jax
import jax.numpy as jnp
from jax import lax
from jax.experimental import pallas as pl
from jax.experimental.pallas import tpu as pltpu

F32 = jnp.float32
BF16 = jnp.bfloat16
I32 = jnp.int32
U32 = jnp.uint32

D_MODEL = 1024
SC_WIDTH = 512
SC_KERNEL = 3
DN_HEADS = 8
DN_DK = 64
DN_KEY = DN_HEADS * DN_DK
DN_VAL = DN_KEY
DN_CONV = 4
DN_CHUNK = 64
PEER_HEADS = 8
PEER_NKEYS = 128
PEER_HALF = 128
PEER_QDIM = 256
PEER_TOPK = 16
PEER_E = PEER_HEADS * PEER_TOPK
NORM_EPS = 1e-6

LANES = 128
SUBLANES = 8
HALO = SUBLANES
PAIR = 2 * DN_DK
N_PAIRS = DN_HEADS // 2
INV_TERMS = 2
PREP_CHUNKS = 2
HALF_ROWS = SUBLANES // 2
GROUP = SUBLANES
GROUP_ROWS = HALF_ROWS * GROUP
N_GROUPS = PEER_E // GROUP
N_SCR = 4
STAGE_TOKENS = 8
STAGE_BUFS = 2
VMEM_TABLE_LIMIT = 56 * 1024 * 1024
VMEM_MIXER_LIMIT = 56 * 1024 * 1024

C_SC = 3 * SC_WIDTH
C_QKV = 2 * DN_KEY + DN_VAL
C_Z = DN_VAL
C_A = DN_KEY
C_B = DN_KEY
C_G = 2 * D_MODEL
C_ALL = C_SC + C_QKV + C_Z + C_A + C_B + C_G


def _silu(x):
    return x * jax.nn.sigmoid(x)


def _softplus(x):
    return jnp.maximum(x, 0.0) + jnp.log(1.0 + jnp.exp(-jnp.abs(x)))


def _gelu(x):
    return 0.5 * x * (1.0 + lax.erf(x * (2.0 ** -0.5)))


def _rms(x, w):
    return x * lax.rsqrt(jnp.mean(x * x, axis=-1, keepdims=True) + NORM_EPS) * w


def _bdot(a, b):
    return jnp.dot(a.astype(BF16), b.astype(BF16), preferred_element_type=F32)


def _xdot(a, b):
    return jnp.dot(a, b, preferred_element_type=F32, precision=lax.Precision.HIGHEST)


def _split(x, n):
    terms = []
    for i in range(n):
        t = x.astype(BF16)
        terms.append(t)
        if i + 1 < n:
            x = x - t.astype(F32)
    return terms


NN_DIMS = (((1,), (0,)), ((), ()))
NT_DIMS = (((1,), (1,)), ((), ()))


def _sdot(a_terms, b_terms, dims=NN_DIMS):
    order = max(len(a_terms), len(b_terms))
    out = None
    for i, a in enumerate(a_terms):
        for j, b in enumerate(b_terms):
            if i + j < order:
                p = lax.dot_general(a, b, dims, preferred_element_type=F32)
                out = p if out is None else out + p
    return out


def _ada_kernel(c_ref, w_ref, b_ref, o_ref):
    o_ref[...] = _bdot(_silu(c_ref[...]), w_ref[...]) + b_ref[...]


def _ada(c_pad, w_ada, b_ada):
    n_out = w_ada.shape[1]
    tn = 1024
    return pl.pallas_call(
        _ada_kernel,
        grid=(n_out // tn,),
        in_specs=[
            pl.BlockSpec((SUBLANES, D_MODEL), lambda j: (0, 0)),
            pl.BlockSpec((D_MODEL, tn), lambda j: (0, j)),
            pl.BlockSpec((1, tn), lambda j: (0, j)),
        ],
        out_specs=pl.BlockSpec((SUBLANES, tn), lambda j: (0, j)),
        out_shape=jax.ShapeDtypeStruct((SUBLANES, n_out), F32),
        name="ada",
    )(c_pad, w_ada, b_ada)


def _inproj_kernel(x_ref, mod_ref, nw_ref, w_ref, sc_ref, qkv_ref, z_ref, a_ref, b_ref, g_ref):
    b = pl.program_id(0)
    mod = mod_ref[pl.ds(b, 1), :]
    sh1 = mod[:, 0:D_MODEL]
    sc1 = mod[:, D_MODEL:2 * D_MODEL]
    h = (_rms(x_ref[...], nw_ref[...]) * (1.0 + sc1) + sh1).astype(BF16)
    col = 0
    for ref, width in ((sc_ref, C_SC), (qkv_ref, C_QKV), (z_ref, C_Z), (a_ref, C_A), (b_ref, C_B), (g_ref, C_G)):
        for j in range(0, width, 512):
            ref[:, j:j + 512] = jnp.dot(h, w_ref[:, col + j:col + j + 512],
                                        preferred_element_type=F32).astype(ref.dtype)
        col += width


def _inproj(x2, mod, nw, w_pad, batch, seq, tm):
    nt = seq // tm
    row = lambda b, t: (b * nt + t, 0)
    const = lambda b, t: (0, 0)
    n = batch * seq
    outs = [(C_SC, BF16), (C_QKV, BF16), (C_Z, BF16), (C_A, F32), (C_B, F32), (C_G, BF16)]
    return pl.pallas_call(
        _inproj_kernel,
        grid=(batch, nt),
        in_specs=[
            pl.BlockSpec((tm, D_MODEL), row),
            pl.BlockSpec(mod.shape, const),
            pl.BlockSpec((1, D_MODEL), const),
            pl.BlockSpec((D_MODEL, C_ALL), const, pipeline_mode=pl.Buffered(1)),
        ],
        out_specs=[pl.BlockSpec((tm, w), row) for w, _ in outs],
        out_shape=[jax.ShapeDtypeStruct((n, w), dt) for w, dt in outs],
        compiler_params=pltpu.CompilerParams(
            dimension_semantics=("arbitrary", "arbitrary"), vmem_limit_bytes=VMEM_MIXER_LIMIT),
        name="inproj",
    )(x2, mod, nw, w_pad)


def _pair_consts():
    r = lax.broadcasted_iota(I32, (PAIR, PAIR), 0)
    c = lax.broadcasted_iota(I32, (PAIR, PAIR), 1)
    bd = ((r >> 6) == (c >> 6)).astype(F32)
    i = lax.broadcasted_iota(I32, (DN_CHUNK, PAIR), 0)
    j = lax.broadcasted_iota(I32, (DN_CHUNK, PAIR), 1) & (DN_DK - 1)
    return bd, i >= j, i > j, (i == j).astype(F32)


def _stack_bd(y, bd):
    return jnp.concatenate([y, y], axis=0) * bd


def _mixer_kernel(sc_ref, qkv_ref, z_ref, a_ref, b_ref, g_ref, x_ref, mod_ref,
                  scw_ref, dnw_ref, alog_ref, dtb_ref, onw_ref, wsc_ref, wdn_ref, wmix_ref, pnw_ref,
                  o_ref,
                  scx_s, qkvx_s, q_s, k_s, v_s, g_s, beta_s, o_s, gcum_s, aqk_s, u_s, wk_s, qg_s, kdec_s,
                  state_s, *, tt):
    b = pl.program_id(0)
    t = pl.program_id(1)

    @pl.when(t == 0)
    def _():
        scx_s[0:HALO, :] = jnp.zeros((HALO, SC_WIDTH), F32)
        qkvx_s[0:HALO, :] = jnp.zeros((HALO, C_QKV), F32)
        state_s[...] = jnp.zeros(state_s.shape, F32)

    sc = sc_ref[...].astype(F32)
    scx_s[HALO:HALO + tt, :] = sc[:, SC_WIDTH:2 * SC_WIDTH] * sc[:, 2 * SC_WIDTH:]
    conv = scw_ref[0:1, :] * scx_s[pl.ds(HALO - (SC_KERNEL - 1), tt), :]
    for kk in range(1, SC_KERNEL):
        conv = conv + scw_ref[kk:kk + 1, :] * scx_s[pl.ds(HALO - (SC_KERNEL - 1) + kk, tt), :]
    y_sc = _bdot(sc[:, 0:SC_WIDTH] * conv, wsc_ref[...])
    scx_s[0:HALO, :] = scx_s[tt:tt + HALO, :]

    qkvx_s[HALO:HALO + tt, :] = qkv_ref[...].astype(F32)
    cq = dnw_ref[0:1, :] * qkvx_s[pl.ds(HALO - (DN_CONV - 1), tt), :]
    for kk in range(1, DN_CONV):
        cq = cq + dnw_ref[kk:kk + 1, :] * qkvx_s[pl.ds(HALO - (DN_CONV - 1) + kk, tt), :]
    qkvx_s[0:HALO, :] = qkvx_s[tt:tt + HALO, :]
    cq = _silu(cq)
    r512 = lax.broadcasted_iota(I32, (DN_KEY, DN_KEY), 0) >> 6
    c512 = lax.broadcasted_iota(I32, (DN_KEY, DN_KEY), 1) >> 6
    head_ones = [(r512 == c512).astype(BF16)]
    q = cq[:, 0:DN_KEY]
    k = cq[:, DN_KEY:2 * DN_KEY]
    q_s[...] = q * lax.rsqrt(_sdot(_split(q * q, 2), head_ones) + NORM_EPS) * (DN_DK ** -0.5)
    k_s[...] = k * lax.rsqrt(_sdot(_split(k * k, 2), head_ones) + NORM_EPS)
    v_s[...] = cq[:, 2 * DN_KEY:]
    beta_s[...] = jax.nn.sigmoid(b_ref[...])
    g_s[...] = -jnp.exp(alog_ref[...]) * _softplus(a_ref[...] + dtb_ref[...])

    bd, incl, strict, eye = _pair_consts()
    bd_b = bd.astype(BF16)
    ri = lax.broadcasted_iota(I32, (DN_CHUNK, DN_CHUNK), 0)
    ci = lax.broadcasted_iota(I32, (DN_CHUNK, DN_CHUNK), 1)
    lt = [(ci <= ri).astype(BF16)]
    nt = (((1,), (1,)), ((), ()))

    def stack_terms(y):
        return [jnp.concatenate([t_, t_], axis=0) * bd_b for t_ in _split(y, INV_TERMS)]

    pairs = [slice(p * PAIR, (p + 1) * PAIR) for p in range(N_PAIRS)]

    def prepare(c, carry):
        prob = []
        for cc in range(PREP_CHUNKS):
            rows = pl.ds(pl.multiple_of((c * PREP_CHUNKS + cc) * DN_CHUNK, DN_CHUNK), DN_CHUNK)
            gcum_all = _sdot(lt, _split(g_s[rows, :], 3))
            gcum_s[rows, :] = gcum_all
            prob += [(rows, sl, gcum_all[:, sl]) for sl in pairs]
        pws, invs, rest = [], [], []
        for rows, sl, gc in prob:
            qp = q_s[rows, sl]
            kp = k_s[rows, sl]
            bp = beta_s[rows, sl]
            grow = jnp.sum(gc * eye, axis=0, keepdims=True)
            dec = jnp.where(incl, jnp.exp(jnp.where(incl, gc - grow, 0.0)), 0.0)
            eg = jnp.exp(gc)
            kb = kp * bp
            kbig = _stack_bd(kp, bd).astype(BF16)
            kk_s = lax.dot_general(kb.astype(BF16), kbig, nt, preferred_element_type=F32)
            qk_s = lax.dot_general(qp.astype(BF16), kbig, nt, preferred_element_type=F32)
            aqk_s[rows, sl] = qk_s * dec
            qg_s[rows, sl] = qp * eg
            kdec_s[rows, sl] = kp * jnp.exp(gc[DN_CHUNK - 1:DN_CHUNK, :] - gc)
            pw = -(kk_s * jnp.where(strict, dec, 0.0))
            pws.append(pw)
            invs.append(eye + pw)
            rest.append((v_s[rows, sl] * bp, kb * eg))
        for _ in range(5):
            pws = [_sdot(_split(pw, INV_TERMS), stack_terms(pw)) for pw in pws]
            invs = [inv + _sdot(_split(inv, INV_TERMS), stack_terms(pw)) for inv, pw in zip(invs, pws)]
        for (rows, sl, _), inv, (vb, kbg) in zip(prob, invs, rest):
            u_s[rows, sl] = _bdot(inv, _stack_bd(vb, bd))
            wk_s[rows, sl] = _bdot(inv, _stack_bd(kbg, bd))
        return carry

    def advance(c, carry):
        rows = pl.ds(pl.multiple_of(c * DN_CHUNK, DN_CHUNK), DN_CHUNK)
        tail = pl.ds(pl.multiple_of(c * DN_CHUNK + DN_CHUNK - SUBLANES, SUBLANES), SUBLANES)
        sts = [state_s[p] for p in range(N_PAIRS)]
        wss = [_bdot(jnp.concatenate([wk_s[rows, sl], qg_s[rows, sl]], axis=0), st) for sl, st in zip(pairs, sts)]
        vnews = [u_s[rows, sl] - ws[0:DN_CHUNK] for sl, ws in zip(pairs, wss)]
        for p, (sl, st, ws, vnew) in enumerate(zip(pairs, sts, wss, vnews)):
            o_s[rows, sl] = ws[DN_CHUNK:] + _bdot(aqk_s[rows, sl], _stack_bd(vnew, bd))
            glast = gcum_s[tail, sl][SUBLANES - 1:SUBLANES, :]
            state_s[p] = st * jnp.exp(glast) + bd * _bdot(kdec_s[rows, sl].T, vnew)
        return carry

    lax.fori_loop(0, tt // (DN_CHUNK * PREP_CHUNKS), prepare, 0)
    lax.fori_loop(0, tt // DN_CHUNK, advance, 0)

    o = o_s[...]
    ms = _sdot(_split(o * o, 2), head_ones) * (1.0 / DN_DK)
    og = o * lax.rsqrt(ms + NORM_EPS) * onw_ref[...] * _silu(z_ref[...].astype(F32))
    y_dn = _bdot(og, wdn_ref[...])
    gates = g_ref[...].astype(F32)
    merged = jax.nn.sigmoid(gates[:, 0:D_MODEL]) * y_sc + jax.nn.sigmoid(gates[:, D_MODEL:]) * y_dn
    y = _bdot(merged, wmix_ref[...])
    gt1 = mod_ref[pl.ds(b, 1), :][:, 2 * D_MODEL:3 * D_MODEL]
    o_ref[...] = x_ref[...] + gt1 * _rms(y, pnw_ref[...])


def _mixer(parts, x2, mod, weights, batch, seq, tt):
    nt = seq // tt
    row = lambda b, t: (b * nt + t, 0)
    const2 = lambda b, t: (0, 0)
    sc, qkv, z, a, bb, g = parts
    data = [(sc, C_SC), (qkv, C_QKV), (z, C_Z), (a, C_A), (bb, C_B), (g, C_G), (x2, D_MODEL)]
    in_specs = [pl.BlockSpec((tt, w), row) for _, w in data]
    in_specs.append(pl.BlockSpec(mod.shape, const2))
    in_specs += [pl.BlockSpec(wt.shape, const2) for wt in weights]
    scratch = [
        pltpu.VMEM((tt + HALO, SC_WIDTH), F32),
        pltpu.VMEM((tt + HALO, C_QKV), F32),
        pltpu.VMEM((tt, DN_KEY), F32),
        pltpu.VMEM((tt, DN_KEY), F32),
        pltpu.VMEM((tt, DN_VAL), F32),
        pltpu.VMEM((tt, DN_KEY), F32),
        pltpu.VMEM((tt, DN_KEY), F32),
        pltpu.VMEM((tt, DN_VAL), F32),
    ] + [pltpu.VMEM((tt, DN_KEY), F32) for _ in range(6)] + [
        pltpu.VMEM((N_PAIRS, PAIR, PAIR), F32),
    ]
    return pl.pallas_call(
        functools.partial(_mixer_kernel, tt=tt),
        grid=(batch, nt),
        in_specs=in_specs,
        out_specs=pl.BlockSpec((tt, D_MODEL), row),
        out_shape=jax.ShapeDtypeStruct((batch * seq, D_MODEL), F32),
        scratch_shapes=scratch,
        compiler_params=pltpu.CompilerParams(
            dimension_semantics=("arbitrary", "arbitrary"), vmem_limit_bytes=VMEM_MIXER_LIMIT),
        name="mixer",
    )(*[d for d, _ in data], mod, *weights)


def _qproj_kernel(x_ref, mod_ref, nw_ref, w_ref, h_ref, q_ref):
    b = pl.program_id(0)
    mod = mod_ref[pl.ds(b, 1), :]
    sh2 = mod[:, 3 * D_MODEL:4 * D_MODEL]
    sc2 = mod[:, 4 * D_MODEL:5 * D_MODEL]
    h = _rms(x_ref[...], nw_ref[...]) * (1.0 + sc2) + sh2
    h_ref[...] = h
    q_ref[...] = _bdot(h, w_ref[...])


def _qproj(x1, mod, nw, wq, batch, seq, tm):
    nt = seq // tm
    row = lambda b, t: (b * nt + t, 0)
    const = lambda b, t: (0, 0)
    n = batch * seq
    nq = wq.shape[1]
    return pl.pallas_call(
        _qproj_kernel,
        grid=(batch, nt),
        in_specs=[
            pl.BlockSpec((tm, D_MODEL), row),
            pl.BlockSpec(mod.shape, const),
            pl.BlockSpec((1, D_MODEL), const),
            pl.BlockSpec(wq.shape, const),
        ],
        out_specs=[pl.BlockSpec((tm, D_MODEL), row), pl.BlockSpec((tm, nq), row)],
        out_shape=[jax.ShapeDtypeStruct((n, D_MODEL), F32), jax.ShapeDtypeStruct((n, nq), F32)],
        compiler_params=pltpu.CompilerParams(dimension_semantics=("arbitrary", "arbitrary")),
        name="qproj",
    )(x1, mod, nw, wq)


def _sort16_network():
    n, pairs, p = 16, [], 1
    while p < n:
        k = p
        while k >= 1:
            for j in range(k % p, n - k, 2 * k):
                for i in range(min(k, n - j - k)):
                    if (i + j) // (2 * p) == (i + j + k) // (2 * p):
                        pairs.append((i + j, i + j + k))
            k //= 2
        p *= 2
    return pairs


def _top16_of_128(st, v_out, i_out):
    k = PEER_TOPK
    vs = [st[SUBLANES * i:SUBLANES * (i + 1), :] for i in range(k)]
    sub = lax.broadcasted_iota(I32, vs[0].shape, 0).astype(F32)
    ix = [sub + float(SUBLANES * i) for i in range(k)]
    for a, b in _sort16_network():
        swap = (vs[b] > vs[a]) | ((vs[b] == vs[a]) & (ix[b] < ix[a]))
        vs[a], vs[b] = jnp.where(swap, vs[b], vs[a]), jnp.where(swap, vs[a], vs[b])
        ix[a], ix[b] = jnp.where(swap, ix[b], ix[a]), jnp.where(swap, ix[a], ix[b])
    for it in range(k):
        m = jnp.max(vs[0], axis=0, keepdims=True)
        pos = jnp.min(jnp.where(vs[0] == m, ix[0], float(PEER_NKEYS)), axis=0, keepdims=True)
        v_out[it:it + 1, :] = m
        i_out[it:it + 1, :] = pos
        hit = ix[0] == pos
        depth = k - 1 - it
        for d in range(depth):
            vs[d] = jnp.where(hit, vs[d + 1], vs[d])
            ix[d] = jnp.where(hit, ix[d + 1], ix[d])


def _top16_of_pairs(v1, i1, v2, i2, v_out, e_out, row0):
    k = PEER_TOPK
    rows = lax.broadcasted_iota(I32, v1.shape, 0).astype(F32)
    cand = v1 + v2[0:1, :]
    taken = jnp.zeros_like(v1)
    for it in range(k):
        m = jnp.max(cand, axis=0, keepdims=True)
        a_win = jnp.min(jnp.where(cand == m, rows, float(k)), axis=0, keepdims=True)
        hit = rows == a_win
        b_win = jnp.sum(jnp.where(hit, taken, 0.0), axis=0, keepdims=True)
        i1_win = jnp.sum(jnp.where(hit, i1, 0.0), axis=0, keepdims=True)
        i2_win = jnp.sum(jnp.where(rows == b_win, i2, 0.0), axis=0, keepdims=True)
        v_out[row0 + it:row0 + it + 1, :] = m
        e_out[row0 + it:row0 + it + 1, :] = i1_win * float(PEER_NKEYS) + i2_win
        if it + 1 < k:
            v1_win = jnp.sum(jnp.where(hit, v1, 0.0), axis=0, keepdims=True)
            nxt = rows == b_win + 1.0
            v2_nxt = jnp.sum(jnp.where(nxt, v2, 0.0), axis=0, keepdims=True)
            new = jnp.where(b_win + 1.0 < float(k), v1_win + v2_nxt, -jnp.inf)
            cand = jnp.where(hit, new, cand)
            taken = jnp.where(hit, b_win + 1.0, taken)


def _topk_kernel(q_ref, keys_ref, e_ref, g_ref, v1_s, i1_s, v2_s, i2_s, ts_s, ei_s, gate_s, *, tt):
    k = PEER_TOPK
    for h in range(PEER_HEADS):
        for p, (v_s, i_s) in enumerate(((v1_s, i1_s), (v2_s, i2_s))):
            col = (2 * h + p) * PEER_HALF
            st = _sdot(_split(keys_ref[2 * h + p], 2), _split(q_ref[:, col:col + PEER_HALF], 2), NT_DIMS)
            _top16_of_128(st, v_s, i_s)
        _top16_of_pairs(v1_s[...], i1_s[...], v2_s[...], i2_s[...], ts_s, ei_s, h * k)
        ts = ts_s[h * k:(h + 1) * k, :]
        e = jnp.exp(ts - jnp.max(ts, axis=0, keepdims=True))
        gate_s[h * k:(h + 1) * k, :] = e / jnp.sum(e, axis=0, keepdims=True)
    e_ref[...] = ei_s[...].T.astype(I32) * HALF_ROWS
    g_ref[...] = gate_s[...].T


def _topk(q, keys, n, tt):
    row = lambda i: (i, 0)
    return pl.pallas_call(
        functools.partial(_topk_kernel, tt=tt),
        grid=(n // tt,),
        in_specs=[
            pl.BlockSpec((tt, q.shape[1]), row),
            pl.BlockSpec(keys.shape, lambda i: (0, 0, 0)),
        ],
        out_specs=[pl.BlockSpec((tt, PEER_E), row)] * 2,
        out_shape=[jax.ShapeDtypeStruct((n, PEER_E), I32), jax.ShapeDtypeStruct((n, PEER_E), F32)],
        scratch_shapes=[
            pltpu.VMEM((PEER_TOPK, tt), F32), pltpu.VMEM((PEER_TOPK, tt), F32),
            pltpu.VMEM((PEER_TOPK, tt), F32), pltpu.VMEM((PEER_TOPK, tt), F32),
            pltpu.VMEM((PEER_E, tt), F32), pltpu.VMEM((PEER_E, tt), F32),
            pltpu.VMEM((PEER_E, tt), F32),
        ],
        compiler_params=pltpu.CompilerParams(dimension_semantics=("arbitrary",)),
        name="topk",
    )(q, keys)


def _staged_tokens(srcs, bufs, sems, tt, body):
    n_groups = tt // STAGE_TOKENS
    nb = STAGE_BUFS
    assert n_groups % nb == 0 and STAGE_TOKENS % 2 == 0

    def copies(grp, slot):
        return [pltpu.make_async_copy(src.at[pl.ds(grp * STAGE_TOKENS, STAGE_TOKENS)], buf.at[slot], sem.at[slot])
                for src, buf, sem in zip(srcs, bufs, sems)]

    for s in range(nb - 1):
        for c in copies(s, s):
            c.start()

    def round_of_groups(i, carry):
        for slot in range(nb):
            grp = nb * i + slot
            for c in copies(jnp.minimum(grp + nb - 1, n_groups - 1), (slot + nb - 1) % nb):
                c.start()
            for c in copies(grp, slot):
                c.wait()
            for tl in range(STAGE_TOKENS):
                body(grp * STAGE_TOKENS + tl, tl,
                     *[lambda k, buf=buf, slot=slot, tl=tl: buf[slot, tl, k] for buf in bufs])
        return carry

    lax.fori_loop(0, n_groups // nb, round_of_groups, 0)
    for s in range(nb - 1):
        for c in copies(n_groups - 1, (n_groups + s) % nb):
            c.wait()


def _stage_scratch(dtype):
    return [pltpu.SMEM((STAGE_BUFS, STAGE_TOKENS, PEER_E), dtype), pltpu.SemaphoreType.DMA((STAGE_BUFS,))]


def _pair_rows(tab_ref, idx, m):
    wa = tab_ref[pl.ds(pl.multiple_of(idx(2 * m), HALF_ROWS), HALF_ROWS), :]
    wb = tab_ref[pl.ds(pl.multiple_of(idx(2 * m + 1), HALF_ROWS), HALF_ROWS), :]
    return jnp.concatenate([wa, wb], axis=0)


def _gather_rows(tab_ref, idx, g, scr):
    ref = scr[g % N_SCR]
    base = (g // N_SCR) * GROUP_ROWS
    for j in range(GROUP):
        e4 = pl.multiple_of(idx(g * GROUP + j), HALF_ROWS)
        ref[pl.ds(base + j, HALF_ROWS, stride=GROUP), :] = tab_ref[pl.ds(e4, HALF_ROWS), :]


def _group_tiles(g, scr):
    ref = scr[g % N_SCR]
    base = (g // N_SCR) * GROUP_ROWS
    return [ref[base + s * GROUP:base + (s + 1) * GROUP, :] for s in range(HALF_ROWS)]


def _unpack(w):
    return pltpu.bitcast(w << 16, F32), pltpu.bitcast(w & jnp.uint32(0xFFFF0000), F32)


def _row_sums_on_lanes(m, eye, ones):
    hi = m.astype(BF16)
    lo = (m - hi.astype(F32)).astype(BF16)
    sums = jnp.dot(hi, ones, preferred_element_type=F32) + jnp.dot(lo, ones, preferred_element_type=F32)
    return jnp.sum(sums * eye, axis=0, keepdims=True)


def _peer_u_kernel(e_ref, h_ref, g_ref, tab_ref, o_ref, prod_a, prod_b, idx_s, sem, *, tt):
    eye = (lax.broadcasted_iota(I32, (PEER_E, LANES), 0) == lax.broadcasted_iota(I32, (PEER_E, LANES), 1)).astype(F32)
    ones = jnp.ones((LANES, LANES), BF16)
    prod = (prod_a, prod_b)

    def finish(t, buf):
        part = buf[pl.ds(0, PEER_E, stride=HALF_ROWS), :]
        for s in range(1, HALF_ROWS):
            part = part + buf[pl.ds(s, PEER_E, stride=HALF_ROWS), :]
        act = _row_sums_on_lanes(part, eye, ones)
        o_ref[pl.ds(t, 1), :] = g_ref[pl.ds(t, 1), :] * _gelu(act)

    prod_b[...] = jnp.zeros(prod_b.shape, F32)

    def tok(t, tl, idx):
        finish(jnp.maximum(t - 1, 0), prod[(tl + 1) & 1])
        hv = h_ref[t]
        h_lo = jnp.concatenate([hv[0:HALF_ROWS], hv[0:HALF_ROWS]], axis=0)
        h_hi = jnp.concatenate([hv[HALF_ROWS:], hv[HALF_ROWS:]], axis=0)
        for m in range(PEER_E // 2):
            lo, hi = _unpack(_pair_rows(tab_ref, idx, m))
            prod[tl & 1][SUBLANES * m:SUBLANES * (m + 1), :] = lo * h_lo + hi * h_hi

    _staged_tokens([e_ref], [idx_s], [sem], tt, tok)
    finish(tt - 1, prod[(tt - 1) & 1])


def _peer_u(e4, h3, gates, tab, n, tt):
    row = lambda i: (i, 0)
    return pl.pallas_call(
        functools.partial(_peer_u_kernel, tt=tt),
        grid=(n // tt,),
        in_specs=[
            pl.BlockSpec((tt, PEER_E), row),
            pl.BlockSpec((tt, SUBLANES, LANES), lambda i: (i, 0, 0)),
            pl.BlockSpec((tt, PEER_E), row),
            pl.BlockSpec(tab.shape, lambda i: (0, 0), pipeline_mode=pl.Buffered(1)),
        ],
        out_specs=pl.BlockSpec((tt, PEER_E), row),
        out_shape=jax.ShapeDtypeStruct((n, PEER_E), F32),
        scratch_shapes=[pltpu.VMEM((PEER_E * HALF_ROWS, LANES), F32) for _ in range(2)] + _stage_scratch(I32),
        compiler_params=pltpu.CompilerParams(
            dimension_semantics=("arbitrary",), vmem_limit_bytes=VMEM_TABLE_LIMIT),
        name="peer_u",
    )(e4, h3, gates, tab)


def _peer_v_kernel(e_ref, a_ref, tab_ref, o_ref, idx_s, sem_i, wgt_s, sem_w, *, tt):
    top = lax.broadcasted_iota(I32, (SUBLANES, LANES), 0) < HALF_ROWS

    def tok(t, tl, idx, wgt):
        acc_lo = [jnp.zeros((SUBLANES, LANES), F32) for _ in range(2)]
        acc_hi = [jnp.zeros((SUBLANES, LANES), F32) for _ in range(2)]
        for m in range(PEER_E // 2):
            lo, hi = _unpack(_pair_rows(tab_ref, idx, m))
            w = jnp.where(top, wgt(2 * m), wgt(2 * m + 1))
            acc_lo[m % 2] = acc_lo[m % 2] + w * lo
            acc_hi[m % 2] = acc_hi[m % 2] + w * hi
        lo = acc_lo[0] + acc_lo[1]
        hi = acc_hi[0] + acc_hi[1]
        o_ref[t, 0:HALF_ROWS, :] = lo[0:HALF_ROWS] + lo[HALF_ROWS:]
        o_ref[t, HALF_ROWS:, :] = hi[0:HALF_ROWS] + hi[HALF_ROWS:]

    _staged_tokens([e_ref, a_ref], [idx_s, wgt_s], [sem_i, sem_w], tt, tok)


def _peer_v(e4, act, tab, n, tt):
    row = lambda i: (i, 0)
    return pl.pallas_call(
        functools.partial(_peer_v_kernel, tt=tt),
        grid=(n // tt,),
        in_specs=[
            pl.BlockSpec((tt, PEER_E), row),
            pl.BlockSpec((tt, PEER_E), row),
            pl.BlockSpec(tab.shape, lambda i: (0, 0), pipeline_mode=pl.Buffered(1)),
        ],
        out_specs=pl.BlockSpec((tt, SUBLANES, LANES), lambda i: (i, 0, 0)),
        out_shape=jax.ShapeDtypeStruct((n, SUBLANES, LANES), F32),
        scratch_shapes=_stage_scratch(I32) + _stage_scratch(F32),
        compiler_params=pltpu.CompilerParams(
            dimension_semantics=("arbitrary",), vmem_limit_bytes=VMEM_TABLE_LIMIT),
        name="peer_v",
    )(e4, act, tab)


def _final_kernel(x_ref, y_ref, mod_ref, nw_ref, o_ref):
    b = pl.program_id(0)
    gt2 = mod_ref[pl.ds(b, 1), :][:, 5 * D_MODEL:6 * D_MODEL]
    o_ref[...] = x_ref[...] + gt2 * _rms(y_ref[...], nw_ref[...])


def _final(x1, y2, mod, nw, batch, seq, tm):
    nt = seq // tm
    row = lambda b, t: (b * nt + t, 0)
    const = lambda b, t: (0, 0)
    return pl.pallas_call(
        _final_kernel,
        grid=(batch, nt),
        in_specs=[
            pl.BlockSpec((tm, D_MODEL), row),
            pl.BlockSpec((tm, D_MODEL), row),
            pl.BlockSpec(mod.shape, const),
            pl.BlockSpec((1, D_MODEL), const),
        ],
        out_specs=pl.BlockSpec((tm, D_MODEL), row),
        out_shape=jax.ShapeDtypeStruct((batch * seq, D_MODEL), F32),
        compiler_params=pltpu.CompilerParams(dimension_semantics=("arbitrary", "arbitrary")),
        name="final",
    )(x1, y2, mod, nw)


def _pack_kernel(x_ref, o_ref):
    half = D_MODEL // 2
    bits = pltpu.bitcast(x_ref[...].astype(BF16).astype(F32), U32)
    w = (bits[:, :half] >> 16) | bits[:, half:]
    for s in range(HALF_ROWS):
        o_ref[pl.ds(s, x_ref.shape[0], stride=HALF_ROWS), :] = w[:, s * LANES:(s + 1) * LANES]


def _pack_table(tab):
    e, d = tab.shape
    te = 512
    assert d == D_MODEL and e % te == 0
    return pl.pallas_call(
        _pack_kernel,
        grid=(e // te,),
        in_specs=[pl.BlockSpec((te, d), lambda i: (i, 0))],
        out_specs=pl.BlockSpec((te * HALF_ROWS, LANES), lambda i: (i, 0)),
        out_shape=jax.ShapeDtypeStruct((e * HALF_ROWS, LANES), U32),
        compiler_params=pltpu.CompilerParams(dimension_semantics=("arbitrary",)),
        name="pack",
    )(tab)


def _tile(seq, cap):
    t = min(seq, cap)
    assert seq % t == 0
    return t


def kernel(x, c, w_ada, b_ada, norm_pre_mix, norm_post_mix, w_in_mix, sc_conv_w, dn_conv_w, dn_a_log,
           dn_dt_bias, dn_out_norm, w_sc_out, w_dn_out, w_mix_out, norm_pre_ffn, norm_post_ffn, peer_w_q,
           peer_sub_keys, peer_u, peer_v):
    batch, seq, d = x.shape
    assert d == D_MODEL and batch <= SUBLANES and seq % DN_CHUNK == 0
    depth = w_ada.shape[0]
    n = batch * seq
    x2 = x.reshape(n, d)
    c_pad = jnp.pad(c, ((0, SUBLANES - batch), (0, 0)))
    tab_u = _pack_table(peer_u)
    tab_v = _pack_table(peer_v)
    rep = lambda v: jnp.repeat(v, DN_DK, axis=-1)
    for l in range(depth):
        mod = _ada(c_pad, w_ada[l], b_ada[l][None, :])
        w = w_in_mix[l]
        o_z = C_SC + C_QKV
        o_a = o_z + C_Z
        w_pad = jnp.concatenate(
            [w[:, :o_a], rep(w[:, o_a:o_a + DN_HEADS]), rep(w[:, o_a + DN_HEADS:o_a + 2 * DN_HEADS]),
             w[:, o_a + 2 * DN_HEADS:]], axis=1).astype(BF16)
        tm = _tile(seq, 512)
        parts = _inproj(x2, mod, norm_pre_mix[l][None, :], w_pad, batch, seq, tm)
        weights = [
            sc_conv_w[l], dn_conv_w[l], rep(dn_a_log[l])[None, :], rep(dn_dt_bias[l])[None, :],
            jnp.tile(dn_out_norm[l], DN_HEADS)[None, :], w_sc_out[l].astype(BF16), w_dn_out[l].astype(BF16),
            w_mix_out[l].astype(BF16), norm_post_mix[l][None, :],
        ]
        x1 = _mixer(parts, x2, mod, weights, batch, seq, _tile(seq, 512))
        h2, q = _qproj(x1, mod, norm_pre_ffn[l][None, :], peer_w_q[l].astype(BF16), batch, seq, tm)
        keys = peer_sub_keys[l].reshape(2 * PEER_HEADS, PEER_NKEYS, PEER_HALF)
        e4, gates = _topk(q, keys, n, _tile(n, LANES))
        tt = _tile(n, 64)
        act = _peer_u(e4, h2.reshape(n, SUBLANES, LANES), gates, tab_u, n, tt)
        y2 = _peer_v(e4, act, tab_v, n, tt).reshape(n, d)
        x2 = _final(x1, y2, mod, norm_post_ffn[l][None, :], batch, seq, tm)
    return x2.reshape(batch, seq, d)
```

```python
import functools

import jax
import jax.numpy as jnp
from jax import lax
from jax.experimental import pallas as pl
from jax.experimental.pallas import tpu as pltpu

F32 = jnp.float32
BF16 = jnp.bfloat16
I32 = jnp.int32
U32 = jnp.uint32

D_MODEL = 1024
SC_WIDTH = 512
SC_KERNEL = 3
DN_HEADS = 8
DN_DK = 64
DN_KEY = DN_HEADS * DN_DK
DN_VAL = DN_KEY
DN_CONV = 4
DN_CHUNK = 64
PEER_HEADS = 8
PEER_NKEYS = 128
PEER_HALF = 128
PEER_QDIM = 256
PEER_TOPK = 16
PEER_E = PEER_HEADS * PEER_TOPK
NORM_EPS = 1e-6

LANES = 128
SUBLANES = 8
HALO = SUBLANES
PAIR = 2 * DN_DK
N_PAIRS = DN_HEADS // 2
INV_TERMS = 2
PREP_CHUNKS = 2
HALF_ROWS = SUBLANES // 2
GROUP = SUBLANES
GROUP_ROWS = HALF_ROWS * GROUP
N_GROUPS = PEER_E // GROUP
N_SCR = 4
STAGE_TOKENS = 8
STAGE_BUFS = 2
VMEM_TABLE_LIMIT = 56 * 1024 * 1024
VMEM_MIXER_LIMIT = 56 * 1024 * 1024

C_SC = 3 * SC_WIDTH
C_QKV = 2 * DN_KEY + DN_VAL
C_Z = DN_VAL
C_A = DN_KEY
C_B = DN_KEY
C_G = 2 * D_MODEL
C_ALL = C_SC + C_QKV + C_Z + C_A + C_B + C_G


def _silu(x):
    return x * jax.nn.sigmoid(x)


def _softplus(x):
    return jnp.maximum(x, 0.0) + jnp.log(1.0 + jnp.exp(-jnp.abs(x)))


def _gelu(x):
    return 0.5 * x * (1.0 + lax.erf(x * (2.0 ** -0.5)))


def _rms(x, w):
    return x * lax.rsqrt(jnp.mean(x * x, axis=-1, keepdims=True) + NORM_EPS) * w


def _bdot(a, b):
    return jnp.dot(a.astype(BF16), b.astype(BF16), preferred_element_type=F32)


def _xdot(a, b):
    return jnp.dot(a, b, preferred_element_type=F32, precision=lax.Precision.HIGHEST)


def _split(x, n):
    terms = []
    for i in range(n):
        t = x.astype(BF16)
        terms.append(t)
        if i + 1 < n:
            x = x - t.astype(F32)
    return terms


NN_DIMS = (((1,), (0,)), ((), ()))
NT_DIMS = (((1,), (1,)), ((), ()))


def _sdot(a_terms, b_terms, dims=NN_DIMS):
    order = max(len(a_terms), len(b_terms))
    out = None
    for i, a in enumerate(a_terms):
        for j, b in enumerate(b_terms):
            if i + j < order:
                p = lax.dot_general(a, b, dims, preferred_element_type=F32)
                out = p if out is None else out + p
    return out


def _ada_kernel(c_ref, w_ref, b_ref, o_ref):
    o_ref[...] = _bdot(_silu(c_ref[...]), w_ref[...]) + b_ref[...]


def _ada(c_pad, w_ada, b_ada):
    n_out = w_ada.shape[1]
    tn = 1024
    return pl.pallas_call(
        _ada_kernel,
        grid=(n_out // tn,),
        in_specs=[
            pl.BlockSpec((SUBLANES, D_MODEL), lambda j: (0, 0)),
            pl.BlockSpec((D_MODEL, tn), lambda j: (0, j)),
            pl.BlockSpec((1, tn), lambda j: (0, j)),
        ],
        out_specs=pl.BlockSpec((SUBLANES, tn), lambda j: (0, j)),
        out_shape=jax.ShapeDtypeStruct((SUBLANES, n_out), F32),
        name="ada",
    )(c_pad, w_ada, b_ada)


def _inproj_kernel(x_ref, mod_ref, nw_ref, w_ref, sc_ref, qkv_ref, z_ref, a_ref, b_ref, g_ref):
    b = pl.program_id(0)
    mod = mod_ref[pl.ds(b, 1), :]
    sh1 = mod[:, 0:D_MODEL]
    sc1 = mod[:, D_MODEL:2 * D_MODEL]
    h = (_rms(x_ref[...], nw_ref[...]) * (1.0 + sc1) + sh1).astype(BF16)
    col = 0
    for ref, width in ((sc_ref, C_SC), (qkv_ref, C_QKV), (z_ref, C_Z), (a_ref, C_A), (b_ref, C_B), (g_ref, C_G)):
        for j in range(0, width, 512):
            ref[:, j:j + 512] = jnp.dot(h, w_ref[:, col + j:col + j + 512],
                                        preferred_element_type=F32).astype(ref.dtype)
        col += width


def _inproj(x2, mod, nw, w_pad, batch, seq, tm):
    nt = seq // tm
    row = lambda b, t: (b * nt + t, 0)
    const = lambda b, t: (0, 0)
    n = batch * seq
    outs = [(C_SC, BF16), (C_QKV, BF16), (C_Z, BF16), (C_A, F32), (C_B, F32), (C_G, BF16)]
    return pl.pallas_call(
        _inproj_kernel,
        grid=(batch, nt),
        in_specs=[
            pl.BlockSpec((tm, D_MODEL), row),
            pl.BlockSpec(mod.shape, const),
            pl.BlockSpec((1, D_MODEL), const),
            pl.BlockSpec((D_MODEL, C_ALL), const, pipeline_mode=pl.Buffered(1)),
        ],
        out_specs=[pl.BlockSpec((tm, w), row) for w, _ in outs],
        out_shape=[jax.ShapeDtypeStruct((n, w), dt) for w, dt in outs],
        compiler_params=pltpu.CompilerParams(
            dimension_semantics=("arbitrary", "arbitrary"), vmem_limit_bytes=VMEM_MIXER_LIMIT),
        name="inproj",
    )(x2, mod, nw, w_pad)


def _pair_consts():
    r = lax.broadcasted_iota(I32, (PAIR, PAIR), 0)
    c = lax.broadcasted_iota(I32, (PAIR, PAIR), 1)
    bd = ((r >> 6) == (c >> 6)).astype(F32)
    i = lax.broadcasted_iota(I32, (DN_CHUNK, PAIR), 0)
    j = lax.broadcasted_iota(I32, (DN_CHUNK, PAIR), 1) & (DN_DK - 1)
    return bd, i >= j, i > j, (i == j).astype(F32)


def _stack_bd(y, bd):
    return jnp.concatenate([y, y], axis=0) * bd


def _mixer_kernel(sc_ref, qkv_ref, z_ref, a_ref, b_ref, g_ref, x_ref, mod_ref,
                  scw_ref, dnw_ref, alog_ref, dtb_ref, onw_ref, wsc_ref, wdn_ref, wmix_ref, pnw_ref,
                  o_ref,
                  scx_s, qkvx_s, q_s, k_s, v_s, g_s, beta_s, o_s, gcum_s, aqk_s, u_s, wk_s, qg_s, kdec_s,
                  state_s, *, tt):
    b = pl.program_id(0)
    t = pl.program_id(1)

    @pl.when(t == 0)
    def _():
        scx_s[0:HALO, :] = jnp.zeros((HALO, SC_WIDTH), F32)
        qkvx_s[0:HALO, :] = jnp.zeros((HALO, C_QKV), F32)
        state_s[...] = jnp.zeros(state_s.shape, F32)

    sc = sc_ref[...].astype(F32)
    scx_s[HALO:HALO + tt, :] = sc[:, SC_WIDTH:2 * SC_WIDTH] * sc[:, 2 * SC_WIDTH:]
    conv = scw_ref[0:1, :] * scx_s[pl.ds(HALO - (SC_KERNEL - 1), tt), :]
    for kk in range(1, SC_KERNEL):
        conv = conv + scw_ref[kk:kk + 1, :] * scx_s[pl.ds(HALO - (SC_KERNEL - 1) + kk, tt), :]
    y_sc = _bdot(sc[:, 0:SC_WIDTH] * conv, wsc_ref[...])
    scx_s[0:HALO, :] = scx_s[tt:tt + HALO, :]

    qkvx_s[HALO:HALO + tt, :] = qkv_ref[...].astype(F32)
    cq = dnw_ref[0:1, :] * qkvx_s[pl.ds(HALO - (DN_CONV - 1), tt), :]
    for kk in range(1, DN_CONV):
        cq = cq + dnw_ref[kk:kk + 1, :] * qkvx_s[pl.ds(HALO - (DN_CONV - 1) + kk, tt), :]
    qkvx_s[0:HALO, :] = qkvx_s[tt:tt + HALO, :]
    cq = _silu(cq)
    r512 = lax.broadcasted_iota(I32, (DN_KEY, DN_KEY), 0) >> 6
    c512 = lax.broadcasted_iota(I32, (DN_KEY, DN_KEY), 1) >> 6
    head_ones = [(r512 == c512).astype(BF16)]
    q = cq[:, 0:DN_KEY]
    k = cq[:, DN_KEY:2 * DN_KEY]
    q_s[...] = q * lax.rsqrt(_sdot(_split(q * q, 2), head_ones) + NORM_EPS) * (DN_DK ** -0.5)
    k_s[...] = k * lax.rsqrt(_sdot(_split(k * k, 2), head_ones) + NORM_EPS)
    v_s[...] = cq[:, 2 * DN_KEY:]
    beta_s[...] = jax.nn.sigmoid(b_ref[...])
    g_s[...] = -jnp.exp(alog_ref[...]) * _softplus(a_ref[...] + dtb_ref[...])

    bd, incl, strict, eye = _pair_consts()
    bd_b = bd.astype(BF16)
    ri = lax.broadcasted_iota(I32, (DN_CHUNK, DN_CHUNK), 0)
    ci = lax.broadcasted_iota(I32, (DN_CHUNK, DN_CHUNK), 1)
    lt = [(ci <= ri).astype(BF16)]
    nt = (((1,), (1,)), ((), ()))

    def stack_terms(y):
        return [jnp.concatenate([t_, t_], axis=0) * bd_b for t_ in _split(y, INV_TERMS)]

    pairs = [slice(p * PAIR, (p + 1) * PAIR) for p in range(N_PAIRS)]

    def prepare(c, carry):
        prob = []
        for cc in range(PREP_CHUNKS):
            rows = pl.ds(pl.multiple_of((c * PREP_CHUNKS + cc) * DN_CHUNK, DN_CHUNK), DN_CHUNK)
            gcum_all = _sdot(lt, _split(g_s[rows, :], 3))
            gcum_s[rows, :] = gcum_all
            prob += [(rows, sl, gcum_all[:, sl]) for sl in pairs]
        pws, invs, rest = [], [], []
        for rows, sl, gc in prob:
            qp = q_s[rows, sl]
            kp = k_s[rows, sl]
            bp = beta_s[rows, sl]
            grow = jnp.sum(gc * eye, axis=0, keepdims=True)
            dec = jnp.where(incl, jnp.exp(jnp.where(incl, gc - grow, 0.0)), 0.0)
            eg = jnp.exp(gc)
            kb = kp * bp
            kbig = _stack_bd(kp, bd).astype(BF16)
            kk_s = lax.dot_general(kb.astype(BF16), kbig, nt, preferred_element_type=F32)
            qk_s = lax.dot_general(qp.astype(BF16), kbig, nt, preferred_element_type=F32)
            aqk_s[rows, sl] = qk_s * dec
            qg_s[rows, sl] = qp * eg
            kdec_s[rows, sl] = kp * jnp.exp(gc[DN_CHUNK - 1:DN_CHUNK, :] - gc)
            pw = -(kk_s * jnp.where(strict, dec, 0.0))
            pws.append(pw)
            invs.append(eye + pw)
            rest.append((v_s[rows, sl] * bp, kb * eg))
        for _ in range(5):
            pws = [_sdot(_split(pw, INV_TERMS), stack_terms(pw)) for pw in pws]
            invs = [inv + _sdot(_split(inv, INV_TERMS), stack_terms(pw)) for inv, pw in zip(invs, pws)]
        for (rows, sl, _), inv, (vb, kbg) in zip(prob, invs, rest):
            u_s[rows, sl] = _bdot(inv, _stack_bd(vb, bd))
            wk_s[rows, sl] = _bdot(inv, _stack_bd(kbg, bd))
        return carry

    def advance(c, carry):
        rows = pl.ds(pl.multiple_of(c * DN_CHUNK, DN_CHUNK), DN_CHUNK)
        tail = pl.ds(pl.multiple_of(c * DN_CHUNK + DN_CHUNK - SUBLANES, SUBLANES), SUBLANES)
        sts = [state_s[p] for p in range(N_PAIRS)]
        wss = [_bdot(jnp.concatenate([wk_s[rows, sl], qg_s[rows, sl]], axis=0), st) for sl, st in zip(pairs, sts)]
        vnews = [u_s[rows, sl] - ws[0:DN_CHUNK] for sl, ws in zip(pairs, wss)]
        for p, (sl, st, ws, vnew) in enumerate(zip(pairs, sts, wss, vnews)):
            o_s[rows, sl] = ws[DN_CHUNK:] + _bdot(aqk_s[rows, sl], _stack_bd(vnew, bd))
            glast = gcum_s[tail, sl][SUBLANES - 1:SUBLANES, :]
            state_s[p] = st * jnp.exp(glast) + bd * _bdot(kdec_s[rows, sl].T, vnew)
        return carry

    lax.fori_loop(0, tt // (DN_CHUNK * PREP_CHUNKS), prepare, 0)
    lax.fori_loop(0, tt // DN_CHUNK, advance, 0)

    o = o_s[...]
    ms = _sdot(_split(o * o, 2), head_ones) * (1.0 / DN_DK)
    og = o * lax.rsqrt(ms + NORM_EPS) * onw_ref[...] * _silu(z_ref[...].astype(F32))
    y_dn = _bdot(og, wdn_ref[...])
    gates = g_ref[...].astype(F32)
    merged = jax.nn.sigmoid(gates[:, 0:D_MODEL]) * y_sc + jax.nn.sigmoid(gates[:, D_MODEL:]) * y_dn
    y = _bdot(merged, wmix_ref[...])
    gt1 = mod_ref[pl.ds(b, 1), :][:, 2 * D_MODEL:3 * D_MODEL]
    o_ref[...] = x_ref[...] + gt1 * _rms(y, pnw_ref[...])


def _mixer(parts, x2, mod, weights, batch, seq, tt):
    nt = seq // tt
    row = lambda b, t: (b * nt + t, 0)
    const2 = lambda b, t: (0, 0)
    sc, qkv, z, a, bb, g = parts
    data = [(sc, C_SC), (qkv, C_QKV), (z, C_Z), (a, C_A), (bb, C_B), (g, C_G), (x2, D_MODEL)]
    in_specs = [pl.BlockSpec((tt, w), row) for _, w in data]
    in_specs.append(pl.BlockSpec(mod.shape, const2))
    in_specs += [pl.BlockSpec(wt.shape, const2) for wt in weights]
    scratch = [
        pltpu.VMEM((tt + HALO, SC_WIDTH), F32),
        pltpu.VMEM((tt + HALO, C_QKV), F32),
        pltpu.VMEM((tt, DN_KEY), F32),
        pltpu.VMEM((tt, DN_KEY), F32),
        pltpu.VMEM((tt, DN_VAL), F32),
        pltpu.VMEM((tt, DN_KEY), F32),
        pltpu.VMEM((tt, DN_KEY), F32),
        pltpu.VMEM((tt, DN_VAL), F32),
    ] + [pltpu.VMEM((tt, DN_KEY), F32) for _ in range(6)] + [
        pltpu.VMEM((N_PAIRS, PAIR, PAIR), F32),
    ]
    return pl.pallas_call(
        functools.partial(_mixer_kernel, tt=tt),
        grid=(batch, nt),
        in_specs=in_specs,
        out_specs=pl.BlockSpec((tt, D_MODEL), row),
        out_shape=jax.ShapeDtypeStruct((batch * seq, D_MODEL), F32),
        scratch_shapes=scratch,
        compiler_params=pltpu.CompilerParams(
            dimension_semantics=("arbitrary", "arbitrary"), vmem_limit_bytes=VMEM_MIXER_LIMIT),
        name="mixer",
    )(*[d for d, _ in data], mod, *weights)


def _qproj_kernel(x_ref, mod_ref, nw_ref, w_ref, h_ref, q_ref):
    b = pl.program_id(0)
    mod = mod_ref[pl.ds(b, 1), :]
    sh2 = mod[:, 3 * D_MODEL:4 * D_MODEL]
    sc2 = mod[:, 4 * D_MODEL:5 * D_MODEL]
    h = _rms(x_ref[...], nw_ref[...]) * (1.0 + sc2) + sh2
    for s in range(SUBLANES):
        h_ref[:, s, :] = h[:, s * LANES:(s + 1) * LANES]
    q_ref[...] = _bdot(h, w_ref[...])


def _qproj(x1, mod, nw, wq, batch, seq, tm):
    nt = seq // tm
    row = lambda b, t: (b * nt + t, 0)
    const = lambda b, t: (0, 0)
    n = batch * seq
    nq = wq.shape[1]
    return pl.pallas_call(
        _qproj_kernel,
        grid=(batch, nt),
        in_specs=[
            pl.BlockSpec((tm, D_MODEL), row),
            pl.BlockSpec(mod.shape, const),
            pl.BlockSpec((1, D_MODEL), const),
            pl.BlockSpec(wq.shape, const),
        ],
        out_specs=[pl.BlockSpec((tm, SUBLANES, LANES), lambda b, t: (b * nt + t, 0, 0)), pl.BlockSpec((tm, nq), row)],
        out_shape=[jax.ShapeDtypeStruct((n, SUBLANES, LANES), F32), jax.ShapeDtypeStruct((n, nq), F32)],
        compiler_params=pltpu.CompilerParams(dimension_semantics=("arbitrary", "arbitrary")),
        name="qproj",
    )(x1, mod, nw, wq)


def _sort16_network():
    n, pairs, p = 16, [], 1
    while p < n:
        k = p
        while k >= 1:
            for j in range(k % p, n - k, 2 * k):
                for i in range(min(k, n - j - k)):
                    if (i + j) // (2 * p) == (i + j + k) // (2 * p):
                        pairs.append((i + j, i + j + k))
            k //= 2
        p *= 2
    return pairs


def _top16_of_128(st, v_out, i_out):
    k = PEER_TOPK
    vs = [st[SUBLANES * i:SUBLANES * (i + 1), :] for i in range(k)]
    sub = lax.broadcasted_iota(I32, vs[0].shape, 0).astype(F32)
    ix = [sub + float(SUBLANES * i) for i in range(k)]
    for a, b in _sort16_network():
        swap = (vs[b] > vs[a]) | ((vs[b] == vs[a]) & (ix[b] < ix[a]))
        vs[a], vs[b] = jnp.where(swap, vs[b], vs[a]), jnp.where(swap, vs[a], vs[b])
        ix[a], ix[b] = jnp.where(swap, ix[b], ix[a]), jnp.where(swap, ix[a], ix[b])
    for it in range(k):
        m = jnp.max(vs[0], axis=0, keepdims=True)
        pos = jnp.min(jnp.where(vs[0] == m, ix[0], float(PEER_NKEYS)), axis=0, keepdims=True)
        v_out[it:it + 1, :] = m
        i_out[it:it + 1, :] = pos
        hit = ix[0] == pos
        depth = k - 1 - it
        for d in range(depth):
            vs[d] = jnp.where(hit, vs[d + 1], vs[d])
            ix[d] = jnp.where(hit, ix[d + 1], ix[d])


def _top16_of_pairs(v1, i1, v2, i2, v_out, e_out, row0):
    k = PEER_TOPK
    rows = lax.broadcasted_iota(I32, v1.shape, 0).astype(F32)
    cand = v1 + v2[0:1, :]
    taken = jnp.zeros_like(v1)
    for it in range(k):
        m = jnp.max(cand, axis=0, keepdims=True)
        a_win = jnp.min(jnp.where(cand == m, rows, float(k)), axis=0, keepdims=True)
        hit = rows == a_win
        b_win = jnp.sum(jnp.where(hit, taken, 0.0), axis=0, keepdims=True)
        i1_win = jnp.sum(jnp.where(hit, i1, 0.0), axis=0, keepdims=True)
        i2_win = jnp.sum(jnp.where(rows == b_win, i2, 0.0), axis=0, keepdims=True)
        v_out[row0 + it:row0 + it + 1, :] = m
        e_out[row0 + it:row0 + it + 1, :] = i1_win * float(PEER_NKEYS) + i2_win
        if it + 1 < k:
            v1_win = jnp.sum(jnp.where(hit, v1, 0.0), axis=0, keepdims=True)
            nxt = rows == b_win + 1.0
            v2_nxt = jnp.sum(jnp.where(nxt, v2, 0.0), axis=0, keepdims=True)
            new = jnp.where(b_win + 1.0 < float(k), v1_win + v2_nxt, -jnp.inf)
            cand = jnp.where(hit, new, cand)
            taken = jnp.where(hit, b_win + 1.0, taken)


def _topk_kernel(q_ref, keys_ref, e_ref, g_ref, v1_s, i1_s, v2_s, i2_s, ts_s, ei_s, gate_s, *, tt):
    k = PEER_TOPK
    for h in range(PEER_HEADS):
        for p, (v_s, i_s) in enumerate(((v1_s, i1_s), (v2_s, i2_s))):
            col = (2 * h + p) * PEER_HALF
            st = _sdot(_split(keys_ref[2 * h + p], 2), _split(q_ref[:, col:col + PEER_HALF], 2), NT_DIMS)
            _top16_of_128(st, v_s, i_s)
        _top16_of_pairs(v1_s[...], i1_s[...], v2_s[...], i2_s[...], ts_s, ei_s, h * k)
        ts = ts_s[h * k:(h + 1) * k, :]
        e = jnp.exp(ts - jnp.max(ts, axis=0, keepdims=True))
        gate_s[h * k:(h + 1) * k, :] = e / jnp.sum(e, axis=0, keepdims=True)
    e_ref[...] = ei_s[...].T.astype(I32) * HALF_ROWS
    g_ref[...] = gate_s[...].T


def _topk(q, keys, n, tt):
    row = lambda i: (i, 0)
    return pl.pallas_call(
        functools.partial(_topk_kernel, tt=tt),
        grid=(n // tt,),
        in_specs=[
            pl.BlockSpec((tt, q.shape[1]), row),
            pl.BlockSpec(keys.shape, lambda i: (0, 0, 0)),
        ],
        out_specs=[pl.BlockSpec((tt, PEER_E), row)] * 2,
        out_shape=[jax.ShapeDtypeStruct((n, PEER_E), I32), jax.ShapeDtypeStruct((n, PEER_E), F32)],
        scratch_shapes=[
            pltpu.VMEM((PEER_TOPK, tt), F32), pltpu.VMEM((PEER_TOPK, tt), F32),
            pltpu.VMEM((PEER_TOPK, tt), F32), pltpu.VMEM((PEER_TOPK, tt), F32),
            pltpu.VMEM((PEER_E, tt), F32), pltpu.VMEM((PEER_E, tt), F32),
            pltpu.VMEM((PEER_E, tt), F32),
        ],
        compiler_params=pltpu.CompilerParams(dimension_semantics=("arbitrary",)),
        name="topk",
    )(q, keys)


def _staged_tokens(srcs, bufs, sems, tt, body):
    n_groups = tt // STAGE_TOKENS
    nb = STAGE_BUFS
    assert n_groups % nb == 0 and STAGE_TOKENS % 2 == 0

    def copies(grp, slot):
        return [pltpu.make_async_copy(src.at[pl.ds(grp * STAGE_TOKENS, STAGE_TOKENS)], buf.at[slot], sem.at[slot])
                for src, buf, sem in zip(srcs, bufs, sems)]

    for s in range(nb - 1):
        for c in copies(s, s):
            c.start()

    def round_of_groups(i, carry):
        for slot in range(nb):
            grp = nb * i + slot
            for c in copies(jnp.minimum(grp + nb - 1, n_groups - 1), (slot + nb - 1) % nb):
                c.start()
            for c in copies(grp, slot):
                c.wait()
            for tl in range(STAGE_TOKENS):
                body(grp * STAGE_TOKENS + tl, tl,
                     *[lambda k, buf=buf, slot=slot, tl=tl: buf[slot, tl, k] for buf in bufs])
        return carry

    lax.fori_loop(0, n_groups // nb, round_of_groups, 0)
    for s in range(nb - 1):
        for c in copies(n_groups - 1, (n_groups + s) % nb):
            c.wait()


def _stage_scratch(dtype):
    return [pltpu.SMEM((STAGE_BUFS, STAGE_TOKENS, PEER_E), dtype), pltpu.SemaphoreType.DMA((STAGE_BUFS,))]


def _pair_rows(tab_ref, idx, m):
    wa = tab_ref[pl.ds(pl.multiple_of(idx(2 * m), HALF_ROWS), HALF_ROWS), :]
    wb = tab_ref[pl.ds(pl.multiple_of(idx(2 * m + 1), HALF_ROWS), HALF_ROWS), :]
    return jnp.concatenate([wa, wb], axis=0)


def _gather_rows(tab_ref, idx, g, scr):
    ref = scr[g % N_SCR]
    base = (g // N_SCR) * GROUP_ROWS
    for j in range(GROUP):
        e4 = pl.multiple_of(idx(g * GROUP + j), HALF_ROWS)
        ref[pl.ds(base + j, HALF_ROWS, stride=GROUP), :] = tab_ref[pl.ds(e4, HALF_ROWS), :]


def _group_tiles(g, scr):
    ref = scr[g % N_SCR]
    base = (g // N_SCR) * GROUP_ROWS
    return [ref[base + s * GROUP:base + (s + 1) * GROUP, :] for s in range(HALF_ROWS)]


def _unpack(w):
    return pltpu.bitcast(w << 16, F32), pltpu.bitcast(w & jnp.uint32(0xFFFF0000), F32)


def _row_sums_on_lanes(m, eye, ones):
    hi = m.astype(BF16)
    lo = (m - hi.astype(F32)).astype(BF16)
    sums = jnp.dot(hi, ones, preferred_element_type=F32) + jnp.dot(lo, ones, preferred_element_type=F32)
    return jnp.sum(sums * eye, axis=0, keepdims=True)


def _peer_u_kernel(e_ref, h_ref, g_ref, tab_ref, o_ref, prod_a, prod_b, idx_s, sem, *, tt):
    eye = (lax.broadcasted_iota(I32, (PEER_E, LANES), 0) == lax.broadcasted_iota(I32, (PEER_E, LANES), 1)).astype(F32)
    ones = jnp.ones((LANES, LANES), BF16)
    prod = (prod_a, prod_b)

    def finish(t, buf):
        part = buf[pl.ds(0, PEER_E, stride=HALF_ROWS), :]
        for s in range(1, HALF_ROWS):
            part = part + buf[pl.ds(s, PEER_E, stride=HALF_ROWS), :]
        act = jnp.sum(part.T, axis=0, keepdims=True)
        o_ref[pl.ds(t, 1), :] = g_ref[pl.ds(t, 1), :] * _gelu(act)

    prod_b[...] = jnp.zeros(prod_b.shape, F32)

    def tok(t, tl, idx):
        finish(jnp.maximum(t - 1, 0), prod[(tl + 1) & 1])
        hv = h_ref[t]
        h_lo = jnp.concatenate([hv[0:HALF_ROWS], hv[0:HALF_ROWS]], axis=0)
        h_hi = jnp.concatenate([hv[HALF_ROWS:], hv[HALF_ROWS:]], axis=0)
        for m in range(PEER_E // 2):
            lo, hi = _unpack(_pair_rows(tab_ref, idx, m))
            prod[tl & 1][SUBLANES * m:SUBLANES * (m + 1), :] = lo * h_lo + hi * h_hi

    _staged_tokens([e_ref], [idx_s], [sem], tt, tok)
    finish(tt - 1, prod[(tt - 1) & 1])


def _peer_u(e4, h3, gates, tab, n, tt):
    row = lambda i: (i, 0)
    return pl.pallas_call(
        functools.partial(_peer_u_kernel, tt=tt),
        grid=(n // tt,),
        in_specs=[
            pl.BlockSpec((tt, PEER_E), row),
            pl.BlockSpec((tt, SUBLANES, LANES), lambda i: (i, 0, 0)),
            pl.BlockSpec((tt, PEER_E), row),
            pl.BlockSpec(tab.shape, lambda i: (0, 0), pipeline_mode=pl.Buffered(1)),
        ],
        out_specs=pl.BlockSpec((tt, PEER_E), row),
        out_shape=jax.ShapeDtypeStruct((n, PEER_E), F32),
        scratch_shapes=[pltpu.VMEM((PEER_E * HALF_ROWS, LANES), F32) for _ in range(2)] + _stage_scratch(I32),
        compiler_params=pltpu.CompilerParams(
            dimension_semantics=("arbitrary",), vmem_limit_bytes=VMEM_TABLE_LIMIT),
        name="peer_u",
    )(e4, h3, gates, tab)


def _peer_v_kernel(e_ref, a_ref, tab_ref, o_ref, idx_s, sem_i, wgt_s, sem_w, *, tt):
    top = lax.broadcasted_iota(I32, (SUBLANES, LANES), 0) < HALF_ROWS

    def tok(t, tl, idx, wgt):
        acc_lo = [jnp.zeros((SUBLANES, LANES), F32) for _ in range(2)]
        acc_hi = [jnp.zeros((SUBLANES, LANES), F32) for _ in range(2)]
        for m in range(PEER_E // 2):
            lo, hi = _unpack(_pair_rows(tab_ref, idx, m))
            w = jnp.where(top, wgt(2 * m), wgt(2 * m + 1))
            acc_lo[m % 2] = acc_lo[m % 2] + w * lo
            acc_hi[m % 2] = acc_hi[m % 2] + w * hi
        lo = acc_lo[0] + acc_lo[1]
        hi = acc_hi[0] + acc_hi[1]
        o_ref[t, 0:HALF_ROWS, :] = lo[0:HALF_ROWS] + lo[HALF_ROWS:]
        o_ref[t, HALF_ROWS:, :] = hi[0:HALF_ROWS] + hi[HALF_ROWS:]

    _staged_tokens([e_ref, a_ref], [idx_s, wgt_s], [sem_i, sem_w], tt, tok)


def _peer_v(e4, act, tab, n, tt):
    row = lambda i: (i, 0)
    return pl.pallas_call(
        functools.partial(_peer_v_kernel, tt=tt),
        grid=(n // tt,),
        in_specs=[
            pl.BlockSpec((tt, PEER_E), row),
            pl.BlockSpec((tt, PEER_E), row),
            pl.BlockSpec(tab.shape, lambda i: (0, 0), pipeline_mode=pl.Buffered(1)),
        ],
        out_specs=pl.BlockSpec((tt, SUBLANES, LANES), lambda i: (i, 0, 0)),
        out_shape=jax.ShapeDtypeStruct((n, SUBLANES, LANES), F32),
        scratch_shapes=_stage_scratch(I32) + _stage_scratch(F32),
        compiler_params=pltpu.CompilerParams(
            dimension_semantics=("arbitrary",), vmem_limit_bytes=VMEM_TABLE_LIMIT),
        name="peer_v",
    )(e4, act, tab)


def _final_kernel(x_ref, y_ref, mod_ref, nw_ref, o_ref):
    b = pl.program_id(0)
    gt2 = mod_ref[pl.ds(b, 1), :][:, 5 * D_MODEL:6 * D_MODEL]
    o_ref[...] = x_ref[...] + gt2 * _rms(y_ref[...], nw_ref[...])


def _final(x1, y2, mod, nw, batch, seq, tm):
    nt = seq // tm
    row = lambda b, t: (b * nt + t, 0)
    const = lambda b, t: (0, 0)
    return pl.pallas_call(
        _final_kernel,
        grid=(batch, nt),
        in_specs=[
            pl.BlockSpec((tm, D_MODEL), row),
            pl.BlockSpec((tm, D_MODEL), row),
            pl.BlockSpec(mod.shape, const),
            pl.BlockSpec((1, D_MODEL), const),
        ],
        out_specs=pl.BlockSpec((tm, D_MODEL), row),
        out_shape=jax.ShapeDtypeStruct((batch * seq, D_MODEL), F32),
        compiler_params=pltpu.CompilerParams(dimension_semantics=("arbitrary", "arbitrary")),
        name="final",
    )(x1, y2, mod, nw)


def _pack_kernel(x_ref, o_ref):
    half = D_MODEL // 2
    bits = pltpu.bitcast(x_ref[...].astype(BF16).astype(F32), U32)
    w = (bits[:, :half] >> 16) | bits[:, half:]
    for s in range(HALF_ROWS):
        o_ref[pl.ds(s, x_ref.shape[0], stride=HALF_ROWS), :] = w[:, s * LANES:(s + 1) * LANES]


def _pack_table(tab):
    e, d = tab.shape
    te = 512
    assert d == D_MODEL and e % te == 0
    return pl.pallas_call(
        _pack_kernel,
        grid=(e // te,),
        in_specs=[pl.BlockSpec((te, d), lambda i: (i, 0))],
        out_specs=pl.BlockSpec((te * HALF_ROWS, LANES), lambda i: (i, 0)),
        out_shape=jax.ShapeDtypeStruct((e * HALF_ROWS, LANES), U32),
        compiler_params=pltpu.CompilerParams(dimension_semantics=("arbitrary",)),
        name="pack",
    )(tab)


def _tile(seq, cap):
    t = min(seq, cap)
    assert seq % t == 0
    return t


def kernel(x, c, w_ada, b_ada, norm_pre_mix, norm_post_mix, w_in_mix, sc_conv_w, dn_conv_w, dn_a_log,
           dn_dt_bias, dn_out_norm, w_sc_out, w_dn_out, w_mix_out, norm_pre_ffn, norm_post_ffn, peer_w_q,
           peer_sub_keys, peer_u, peer_v):
    batch, seq, d = x.shape
    assert d == D_MODEL and batch <= SUBLANES and seq % DN_CHUNK == 0
    depth = w_ada.shape[0]
    n = batch * seq
    x2 = x.reshape(n, d)
    c_pad = jnp.pad(c, ((0, SUBLANES - batch), (0, 0)))
    tab_u = _pack_table(peer_u)
    tab_v = _pack_table(peer_v)
    rep = lambda v: jnp.repeat(v, DN_DK, axis=-1)
    for l in range(depth):
        mod = _ada(c_pad, w_ada[l], b_ada[l][None, :])
        w = w_in_mix[l]
        o_z = C_SC + C_QKV
        o_a = o_z + C_Z
        w_pad = jnp.concatenate(
            [w[:, :o_a], rep(w[:, o_a:o_a + DN_HEADS]), rep(w[:, o_a + DN_HEADS:o_a + 2 * DN_HEADS]),
             w[:, o_a + 2 * DN_HEADS:]], axis=1).astype(BF16)
        tm = _tile(seq, 512)
        parts = _inproj(x2, mod, norm_pre_mix[l][None, :], w_pad, batch, seq, tm)
        weights = [
            sc_conv_w[l], dn_conv_w[l], rep(dn_a_log[l])[None, :], rep(dn_dt_bias[l])[None, :],
            jnp.tile(dn_out_norm[l], DN_HEADS)[None, :], w_sc_out[l].astype(BF16), w_dn_out[l].astype(BF16),
            w_mix_out[l].astype(BF16), norm_post_mix[l][None, :],
        ]
        x1 = _mixer(parts, x2, mod, weights, batch, seq, _tile(seq, 512))
        h2, q = _qproj(x1, mod, norm_pre_ffn[l][None, :], peer_w_q[l].astype(BF16), batch, seq, tm)
        keys = peer_sub_keys[l].reshape(2 * PEER_HEADS, PEER_NKEYS, PEER_HALF)
        e4, gates = _topk(q, keys, n, _tile(n, LANES))
        tt = _tile(n, 64)
        act = _peer_u(e4, h2, gates, tab_u, n, tt)
        y2 = _peer_v(e4, act, tab_v, n, tt).reshape(n, d)
        x2 = _final(x1, y2, mod, norm_post_ffn[l][None, :], batch, seq, tm)
    return x2.reshape(batch, seq, d)
```

```python
import functools

import jax
import jax.numpy as jnp
from jax import lax
from jax.experimental import pallas as pl
from jax.experimental.pallas import tpu as pltpu

F32 = jnp.float32
BF16 = jnp.bfloat16
I32 = jnp.int32
U32 = jnp.uint32

D_MODEL = 1024
SC_WIDTH = 512
SC_KERNEL = 3
DN_HEADS = 8
DN_DK = 64
DN_KEY = DN_HEADS * DN_DK
DN_VAL = DN_KEY
DN_CONV = 4
DN_CHUNK = 64
PEER_HEADS = 8
PEER_NKEYS = 128
PEER_HALF = 128
PEER_QDIM = 256
PEER_TOPK = 16
PEER_E = PEER_HEADS * PEER_TOPK
NORM_EPS = 1e-6

LANES = 128
SUBLANES = 8
HALO = SUBLANES
PAIR = 2 * DN_DK
N_PAIRS = DN_HEADS // 2
INV_TERMS = 2
PREP_CHUNKS = 2
HALF_ROWS = SUBLANES // 2
GROUP = SUBLANES
GROUP_ROWS = HALF_ROWS * GROUP
N_GROUPS = PEER_E // GROUP
N_SCR = 4
STAGE_TOKENS = 8
STAGE_BUFS = 2
VMEM_TABLE_LIMIT = 56 * 1024 * 1024
VMEM_MIXER_LIMIT = 56 * 1024 * 1024

C_SC = 3 * SC_WIDTH
C_QKV = 2 * DN_KEY + DN_VAL
C_Z = DN_VAL
C_A = DN_KEY
C_B = DN_KEY
C_G = 2 * D_MODEL
C_ALL = C_SC + C_QKV + C_Z + C_A + C_B + C_G


def _silu(x):
    return x * jax.nn.sigmoid(x)


def _softplus(x):
    return jnp.maximum(x, 0.0) + jnp.log(1.0 + jnp.exp(-jnp.abs(x)))


def _gelu(x):
    return 0.5 * x * (1.0 + lax.erf(x * (2.0 ** -0.5)))


def _rms(x, w):
    return x * lax.rsqrt(jnp.mean(x * x, axis=-1, keepdims=True) + NORM_EPS) * w


def _bdot(a, b):
    return jnp.dot(a.astype(BF16), b.astype(BF16), preferred_element_type=F32)


def _xdot(a, b):
    return jnp.dot(a, b, preferred_element_type=F32, precision=lax.Precision.HIGHEST)


def _split(x, n):
    terms = []
    for i in range(n):
        t = x.astype(BF16)
        terms.append(t)
        if i + 1 < n:
            x = x - t.astype(F32)
    return terms


NN_DIMS = (((1,), (0,)), ((), ()))
NT_DIMS = (((1,), (1,)), ((), ()))


def _sdot(a_terms, b_terms, dims=NN_DIMS):
    order = max(len(a_terms), len(b_terms))
    out = None
    for i, a in enumerate(a_terms):
        for j, b in enumerate(b_terms):
            if i + j < order:
                p = lax.dot_general(a, b, dims, preferred_element_type=F32)
                out = p if out is None else out + p
    return out


def _ada_kernel(c_ref, w_ref, b_ref, o_ref):
    o_ref[...] = _bdot(_silu(c_ref[...]), w_ref[...]) + b_ref[...]


def _ada(c_pad, w_ada, b_ada):
    n_out = w_ada.shape[1]
    tn = 1024
    return pl.pallas_call(
        _ada_kernel,
        grid=(n_out // tn,),
        in_specs=[
            pl.BlockSpec((SUBLANES, D_MODEL), lambda j: (0, 0)),
            pl.BlockSpec((D_MODEL, tn), lambda j: (0, j)),
            pl.BlockSpec((1, tn), lambda j: (0, j)),
        ],
        out_specs=pl.BlockSpec((SUBLANES, tn), lambda j: (0, j)),
        out_shape=jax.ShapeDtypeStruct((SUBLANES, n_out), F32),
        name="ada",
    )(c_pad, w_ada, b_ada)


def _inproj_kernel(x_ref, mod_ref, nw_ref, w_ref, sc_ref, qkv_ref, z_ref, a_ref, b_ref, g_ref):
    b = pl.program_id(0)
    mod = mod_ref[pl.ds(b, 1), :]
    sh1 = mod[:, 0:D_MODEL]
    sc1 = mod[:, D_MODEL:2 * D_MODEL]
    h = (_rms(x_ref[...], nw_ref[...]) * (1.0 + sc1) + sh1).astype(BF16)
    col = 0
    for ref, width in ((sc_ref, C_SC), (qkv_ref, C_QKV), (z_ref, C_Z), (a_ref, C_A), (b_ref, C_B), (g_ref, C_G)):
        for j in range(0, width, 512):
            ref[:, j:j + 512] = jnp.dot(h, w_ref[:, col + j:col + j + 512],
                                        preferred_element_type=F32).astype(ref.dtype)
        col += width


def _inproj(x2, mod, nw, w_pad, batch, seq, tm):
    nt = seq // tm
    row = lambda b, t: (b * nt + t, 0)
    const = lambda b, t: (0, 0)
    n = batch * seq
    outs = [(C_SC, BF16), (C_QKV, BF16), (C_Z, BF16), (C_A, F32), (C_B, F32), (C_G, BF16)]
    return pl.pallas_call(
        _inproj_kernel,
        grid=(batch, nt),
        in_specs=[
            pl.BlockSpec((tm, D_MODEL), row),
            pl.BlockSpec(mod.shape, const),
            pl.BlockSpec((1, D_MODEL), const),
            pl.BlockSpec((D_MODEL, C_ALL), const, pipeline_mode=pl.Buffered(1)),
        ],
        out_specs=[pl.BlockSpec((tm, w), row) for w, _ in outs],
        out_shape=[jax.ShapeDtypeStruct((n, w), dt) for w, dt in outs],
        compiler_params=pltpu.CompilerParams(
            dimension_semantics=("arbitrary", "arbitrary"), vmem_limit_bytes=VMEM_MIXER_LIMIT),
        name="inproj",
    )(x2, mod, nw, w_pad)


def _pair_consts():
    r = lax.broadcasted_iota(I32, (PAIR, PAIR), 0)
    c = lax.broadcasted_iota(I32, (PAIR, PAIR), 1)
    bd = ((r >> 6) == (c >> 6)).astype(F32)
    i = lax.broadcasted_iota(I32, (DN_CHUNK, PAIR), 0)
    j = lax.broadcasted_iota(I32, (DN_CHUNK, PAIR), 1) & (DN_DK - 1)
    return bd, i >= j, i > j, (i == j).astype(F32)


def _stack_bd(y, bd):
    return jnp.concatenate([y, y], axis=0) * bd


def _mixer_kernel(sc_ref, qkv_ref, z_ref, a_ref, b_ref, g_ref, x_ref, mod_ref,
                  scw_ref, dnw_ref, alog_ref, dtb_ref, onw_ref, wsc_ref, wdn_ref, wmix_ref, pnw_ref,
                  o_ref,
                  scx_s, qkvx_s, q_s, k_s, v_s, g_s, beta_s, o_s, gcum_s, aqk_s, u_s, wk_s, qg_s, kdec_s,
                  state_s, *, tt):
    b = pl.program_id(0)
    t = pl.program_id(1)

    @pl.when(t == 0)
    def _():
        scx_s[0:HALO, :] = jnp.zeros((HALO, SC_WIDTH), F32)
        qkvx_s[0:HALO, :] = jnp.zeros((HALO, C_QKV), F32)
        state_s[...] = jnp.zeros(state_s.shape, F32)

    sc = sc_ref[...].astype(F32)
    scx_s[HALO:HALO + tt, :] = sc[:, SC_WIDTH:2 * SC_WIDTH] * sc[:, 2 * SC_WIDTH:]
    conv = scw_ref[0:1, :] * scx_s[pl.ds(HALO - (SC_KERNEL - 1), tt), :]
    for kk in range(1, SC_KERNEL):
        conv = conv + scw_ref[kk:kk + 1, :] * scx_s[pl.ds(HALO - (SC_KERNEL - 1) + kk, tt), :]
    y_sc = _bdot(sc[:, 0:SC_WIDTH] * conv, wsc_ref[...])
    scx_s[0:HALO, :] = scx_s[tt:tt + HALO, :]

    qkvx_s[HALO:HALO + tt, :] = qkv_ref[...].astype(F32)
    cq = dnw_ref[0:1, :] * qkvx_s[pl.ds(HALO - (DN_CONV - 1), tt), :]
    for kk in range(1, DN_CONV):
        cq = cq + dnw_ref[kk:kk + 1, :] * qkvx_s[pl.ds(HALO - (DN_CONV - 1) + kk, tt), :]
    qkvx_s[0:HALO, :] = qkvx_s[tt:tt + HALO, :]
    cq = _silu(cq)
    r512 = lax.broadcasted_iota(I32, (DN_KEY, DN_KEY), 0) >> 6
    c512 = lax.broadcasted_iota(I32, (DN_KEY, DN_KEY), 1) >> 6
    head_ones = [(r512 == c512).astype(BF16)]
    q = cq[:, 0:DN_KEY]
    k = cq[:, DN_KEY:2 * DN_KEY]
    q_s[...] = q * lax.rsqrt(_sdot(_split(q * q, 2), head_ones) + NORM_EPS) * (DN_DK ** -0.5)
    k_s[...] = k * lax.rsqrt(_sdot(_split(k * k, 2), head_ones) + NORM_EPS)
    v_s[...] = cq[:, 2 * DN_KEY:]
    beta_s[...] = jax.nn.sigmoid(b_ref[...])
    g_s[...] = -jnp.exp(alog_ref[...]) * _softplus(a_ref[...] + dtb_ref[...])

    bd, incl, strict, eye = _pair_consts()
    bd_b = bd.astype(BF16)
    ri = lax.broadcasted_iota(I32, (DN_CHUNK, DN_CHUNK), 0)
    ci = lax.broadcasted_iota(I32, (DN_CHUNK, DN_CHUNK), 1)
    lt = [(ci <= ri).astype(BF16)]
    nt = (((1,), (1,)), ((), ()))

    def stack_terms(y):
        return [jnp.concatenate([t_, t_], axis=0) * bd_b for t_ in _split(y, INV_TERMS)]

    pairs = [slice(p * PAIR, (p + 1) * PAIR) for p in range(N_PAIRS)]

    def prepare(c, carry):
        prob = []
        for cc in range(PREP_CHUNKS):
            rows = pl.ds(pl.multiple_of((c * PREP_CHUNKS + cc) * DN_CHUNK, DN_CHUNK), DN_CHUNK)
            gcum_all = _sdot(lt, _split(g_s[rows, :], 3))
            gcum_s[rows, :] = gcum_all
            prob += [(rows, sl, gcum_all[:, sl]) for sl in pairs]
        pws, invs, rest = [], [], []
        for rows, sl, gc in prob:
            qp = q_s[rows, sl]
            kp = k_s[rows, sl]
            bp = beta_s[rows, sl]
            grow = jnp.sum(gc * eye, axis=0, keepdims=True)
            dec = jnp.where(incl, jnp.exp(jnp.where(incl, gc - grow, 0.0)), 0.0)
            eg = jnp.exp(gc)
            kb = kp * bp
            kbig = _stack_bd(kp, bd).astype(BF16)
            kk_s = lax.dot_general(kb.astype(BF16), kbig, nt, preferred_element_type=F32)
            qk_s = lax.dot_general(qp.astype(BF16), kbig, nt, preferred_element_type=F32)
            aqk_s[rows, sl] = qk_s * dec
            qg_s[rows, sl] = qp * eg
            kdec_s[rows, sl] = kp * jnp.exp(gc[DN_CHUNK - 1:DN_CHUNK, :] - gc)
            pw = -(kk_s * jnp.where(strict, dec, 0.0))
            pws.append(pw)
            invs.append(eye + pw)
            rest.append((v_s[rows, sl] * bp, kb * eg))
        for _ in range(5):
            pws = [_sdot(_split(pw, INV_TERMS), stack_terms(pw)) for pw in pws]
            invs = [inv + _sdot(_split(inv, INV_TERMS), stack_terms(pw)) for inv, pw in zip(invs, pws)]
        for (rows, sl, _), inv, (vb, kbg) in zip(prob, invs, rest):
            u_s[rows, sl] = _bdot(inv, _stack_bd(vb, bd))
            wk_s[rows, sl] = _bdot(inv, _stack_bd(kbg, bd))
        return carry

    def advance(c, carry):
        rows = pl.ds(pl.multiple_of(c * DN_CHUNK, DN_CHUNK), DN_CHUNK)
        tail = pl.ds(pl.multiple_of(c * DN_CHUNK + DN_CHUNK - SUBLANES, SUBLANES), SUBLANES)
        sts = [state_s[p] for p in range(N_PAIRS)]
        wss = [_bdot(jnp.concatenate([wk_s[rows, sl], qg_s[rows, sl]], axis=0), st) for sl, st in zip(pairs, sts)]
        vnews = [u_s[rows, sl] - ws[0:DN_CHUNK] for sl, ws in zip(pairs, wss)]
        for p, (sl, st, ws, vnew) in enumerate(zip(pairs, sts, wss, vnews)):
            o_s[rows, sl] = ws[DN_CHUNK:] + _bdot(aqk_s[rows, sl], _stack_bd(vnew, bd))
            glast = gcum_s[tail, sl][SUBLANES - 1:SUBLANES, :]
            state_s[p] = st * jnp.exp(glast) + bd * _bdot(kdec_s[rows, sl].T, vnew)
        return carry

    lax.fori_loop(0, tt // (DN_CHUNK * PREP_CHUNKS), prepare, 0)
    lax.fori_loop(0, tt // DN_CHUNK, advance, 0)

    o = o_s[...]
    ms = _sdot(_split(o * o, 2), head_ones) * (1.0 / DN_DK)
    og = o * lax.rsqrt(ms + NORM_EPS) * onw_ref[...] * _silu(z_ref[...].astype(F32))
    y_dn = _bdot(og, wdn_ref[...])
    gates = g_ref[...].astype(F32)
    merged = jax.nn.sigmoid(gates[:, 0:D_MODEL]) * y_sc + jax.nn.sigmoid(gates[:, D_MODEL:]) * y_dn
    y = _bdot(merged, wmix_ref[...])
    gt1 = mod_ref[pl.ds(b, 1), :][:, 2 * D_MODEL:3 * D_MODEL]
    o_ref[...] = x_ref[...] + gt1 * _rms(y, pnw_ref[...])


def _mixer(parts, x2, mod, weights, batch, seq, tt):
    nt = seq // tt
    row = lambda b, t: (b * nt + t, 0)
    const2 = lambda b, t: (0, 0)
    sc, qkv, z, a, bb, g = parts
    data = [(sc, C_SC), (qkv, C_QKV), (z, C_Z), (a, C_A), (bb, C_B), (g, C_G), (x2, D_MODEL)]
    in_specs = [pl.BlockSpec((tt, w), row) for _, w in data]
    in_specs.append(pl.BlockSpec(mod.shape, const2))
    in_specs += [pl.BlockSpec(wt.shape, const2) for wt in weights]
    scratch = [
        pltpu.VMEM((tt + HALO, SC_WIDTH), F32),
        pltpu.VMEM((tt + HALO, C_QKV), F32),
        pltpu.VMEM((tt, DN_KEY), F32),
        pltpu.VMEM((tt, DN_KEY), F32),
        pltpu.VMEM((tt, DN_VAL), F32),
        pltpu.VMEM((tt, DN_KEY), F32),
        pltpu.VMEM((tt, DN_KEY), F32),
        pltpu.VMEM((tt, DN_VAL), F32),
    ] + [pltpu.VMEM((tt, DN_KEY), F32) for _ in range(6)] + [
        pltpu.VMEM((N_PAIRS, PAIR, PAIR), F32),
    ]
    return pl.pallas_call(
        functools.partial(_mixer_kernel, tt=tt),
        grid=(batch, nt),
        in_specs=in_specs,
        out_specs=pl.BlockSpec((tt, D_MODEL), row),
        out_shape=jax.ShapeDtypeStruct((batch * seq, D_MODEL), F32),
        scratch_shapes=scratch,
        compiler_params=pltpu.CompilerParams(
            dimension_semantics=("arbitrary", "arbitrary"), vmem_limit_bytes=VMEM_MIXER_LIMIT),
        name="mixer",
    )(*[d for d, _ in data], mod, *weights)


def _qproj_kernel(x_ref, mod_ref, nw_ref, w_ref, h_ref, q_ref):
    b = pl.program_id(0)
    mod = mod_ref[pl.ds(b, 1), :]
    sh2 = mod[:, 3 * D_MODEL:4 * D_MODEL]
    sc2 = mod[:, 4 * D_MODEL:5 * D_MODEL]
    h = _rms(x_ref[...], nw_ref[...]) * (1.0 + sc2) + sh2
    h_ref[...] = h
    q_ref[...] = _bdot(h, w_ref[...])


def _qproj(x1, mod, nw, wq, batch, seq, tm):
    nt = seq // tm
    row = lambda b, t: (b * nt + t, 0)
    const = lambda b, t: (0, 0)
    n = batch * seq
    nq = wq.shape[1]
    return pl.pallas_call(
        _qproj_kernel,
        grid=(batch, nt),
        in_specs=[
            pl.BlockSpec((tm, D_MODEL), row),
            pl.BlockSpec(mod.shape, const),
            pl.BlockSpec((1, D_MODEL), const),
            pl.BlockSpec(wq.shape, const),
        ],
        out_specs=[pl.BlockSpec((tm, D_MODEL), row), pl.BlockSpec((tm, nq), row)],
        out_shape=[jax.ShapeDtypeStruct((n, D_MODEL), F32), jax.ShapeDtypeStruct((n, nq), F32)],
        compiler_params=pltpu.CompilerParams(dimension_semantics=("arbitrary", "arbitrary")),
        name="qproj",
    )(x1, mod, nw, wq)


def _sort16_network():
    n, pairs, p = 16, [], 1
    while p < n:
        k = p
        while k >= 1:
            for j in range(k % p, n - k, 2 * k):
                for i in range(min(k, n - j - k)):
                    if (i + j) // (2 * p) == (i + j + k) // (2 * p):
                        pairs.append((i + j, i + j + k))
            k //= 2
        p *= 2
    return pairs


def _top16_of_128(st, v_out, i_out):
    k = PEER_TOPK
    vs = [st[SUBLANES * i:SUBLANES * (i + 1), :] for i in range(k)]
    sub = lax.broadcasted_iota(I32, vs[0].shape, 0).astype(F32)
    ix = [sub + float(SUBLANES * i) for i in range(k)]
    for a, b in _sort16_network():
        swap = (vs[b] > vs[a]) | ((vs[b] == vs[a]) & (ix[b] < ix[a]))
        vs[a], vs[b] = jnp.where(swap, vs[b], vs[a]), jnp.where(swap, vs[a], vs[b])
        ix[a], ix[b] = jnp.where(swap, ix[b], ix[a]), jnp.where(swap, ix[a], ix[b])
    for it in range(k):
        m = jnp.max(vs[0], axis=0, keepdims=True)
        pos = jnp.min(jnp.where(vs[0] == m, ix[0], float(PEER_NKEYS)), axis=0, keepdims=True)
        v_out[it:it + 1, :] = m
        i_out[it:it + 1, :] = pos
        hit = ix[0] == pos
        depth = k - 1 - it
        for d in range(depth):
            vs[d] = jnp.where(hit, vs[d + 1], vs[d])
            ix[d] = jnp.where(hit, ix[d + 1], ix[d])


def _top16_of_pairs(v1, i1, v2, i2, v_out, e_out, row0):
    k = PEER_TOPK
    rows = lax.broadcasted_iota(I32, v1.shape, 0).astype(F32)
    cand = v1 + v2[0:1, :]
    taken = jnp.zeros_like(v1)
    for it in range(k):
        m = jnp.max(cand, axis=0, keepdims=True)
        a_win = jnp.min(jnp.where(cand == m, rows, float(k)), axis=0, keepdims=True)
        hit = rows == a_win
        b_win = jnp.sum(jnp.where(hit, taken, 0.0), axis=0, keepdims=True)
        i1_win = jnp.sum(jnp.where(hit, i1, 0.0), axis=0, keepdims=True)
        i2_win = jnp.sum(jnp.where(rows == b_win, i2, 0.0), axis=0, keepdims=True)
        v_out[row0 + it:row0 + it + 1, :] = m
        e_out[row0 + it:row0 + it + 1, :] = i1_win * float(PEER_NKEYS) + i2_win
        if it + 1 < k:
            v1_win = jnp.sum(jnp.where(hit, v1, 0.0), axis=0, keepdims=True)
            nxt = rows == b_win + 1.0
            v2_nxt = jnp.sum(jnp.where(nxt, v2, 0.0), axis=0, keepdims=True)
            new = jnp.where(b_win + 1.0 < float(k), v1_win + v2_nxt, -jnp.inf)
            cand = jnp.where(hit, new, cand)
            taken = jnp.where(hit, b_win + 1.0, taken)


def _topk_kernel(q_ref, keys_ref, e_ref, g_ref, v1_s, i1_s, v2_s, i2_s, ts_s, ei_s, gate_s, *, tt):
    k = PEER_TOPK
    for h in range(PEER_HEADS):
        for p, (v_s, i_s) in enumerate(((v1_s, i1_s), (v2_s, i2_s))):
            col = (2 * h + p) * PEER_HALF
            st = _sdot(_split(keys_ref[2 * h + p], 2), _split(q_ref[:, col:col + PEER_HALF], 2), NT_DIMS)
            _top16_of_128(st, v_s, i_s)
        _top16_of_pairs(v1_s[...], i1_s[...], v2_s[...], i2_s[...], ts_s, ei_s, h * k)
        ts = ts_s[h * k:(h + 1) * k, :]
        e = jnp.exp(ts - jnp.max(ts, axis=0, keepdims=True))
        gate_s[h * k:(h + 1) * k, :] = e / jnp.sum(e, axis=0, keepdims=True)
    e_ref[...] = ei_s[...].T.astype(I32) * HALF_ROWS
    g_ref[...] = gate_s[...].T


def _topk(q, keys, n, tt):
    row = lambda i: (i, 0)
    return pl.pallas_call(
        functools.partial(_topk_kernel, tt=tt),
        grid=(n // tt,),
        in_specs=[
            pl.BlockSpec((tt, q.shape[1]), row),
            pl.BlockSpec(keys.shape, lambda i: (0, 0, 0)),
        ],
        out_specs=[pl.BlockSpec((tt, PEER_E), row)] * 2,
        out_shape=[jax.ShapeDtypeStruct((n, PEER_E), I32), jax.ShapeDtypeStruct((n, PEER_E), F32)],
        scratch_shapes=[
            pltpu.VMEM((PEER_TOPK, tt), F32), pltpu.VMEM((PEER_TOPK, tt), F32),
            pltpu.VMEM((PEER_TOPK, tt), F32), pltpu.VMEM((PEER_TOPK, tt), F32),
            pltpu.VMEM((PEER_E, tt), F32), pltpu.VMEM((PEER_E, tt), F32),
            pltpu.VMEM((PEER_E, tt), F32),
        ],
        compiler_params=pltpu.CompilerParams(dimension_semantics=("arbitrary",)),
        name="topk",
    )(q, keys)


def _staged_tokens(srcs, bufs, sems, tt, body, after_group=None):
    n_groups = tt // STAGE_TOKENS
    nb = STAGE_BUFS
    assert n_groups % nb == 0 and STAGE_TOKENS % 2 == 0

    def copies(grp, slot):
        return [pltpu.make_async_copy(src.at[pl.ds(grp * STAGE_TOKENS, STAGE_TOKENS)], buf.at[slot], sem.at[slot])
                for src, buf, sem in zip(srcs, bufs, sems)]

    for s in range(nb - 1):
        for c in copies(s, s):
            c.start()

    def round_of_groups(i, carry):
        for slot in range(nb):
            grp = nb * i + slot
            for c in copies(jnp.minimum(grp + nb - 1, n_groups - 1), (slot + nb - 1) % nb):
                c.start()
            for c in copies(grp, slot):
                c.wait()
            if after_group is not None:
                after_group(pl.multiple_of(jnp.maximum(grp - 1, 0) * STAGE_TOKENS, STAGE_TOKENS), (slot - 1) % nb)
            for tl in range(STAGE_TOKENS):
                body(grp * STAGE_TOKENS + tl, tl, slot,
                     *[lambda k, buf=buf, slot=slot, tl=tl: buf[slot, tl, k] for buf in bufs])
        return carry

    lax.fori_loop(0, n_groups // nb, round_of_groups, 0)
    for s in range(nb - 1):
        for c in copies(n_groups - 1, (n_groups + s) % nb):
            c.wait()
    if after_group is not None:
        after_group((n_groups - 1) * STAGE_TOKENS, (n_groups - 1) % nb)


def _stage_scratch(dtype):
    return [pltpu.SMEM((STAGE_BUFS, STAGE_TOKENS, PEER_E), dtype), pltpu.SemaphoreType.DMA((STAGE_BUFS,))]


def _pair_rows(tab_ref, idx, m):
    wa = tab_ref[pl.ds(pl.multiple_of(idx(2 * m), HALF_ROWS), HALF_ROWS), :]
    wb = tab_ref[pl.ds(pl.multiple_of(idx(2 * m + 1), HALF_ROWS), HALF_ROWS), :]
    return jnp.concatenate([wa, wb], axis=0)


def _gather_rows(tab_ref, idx, g, scr):
    ref = scr[g % N_SCR]
    base = (g // N_SCR) * GROUP_ROWS
    for j in range(GROUP):
        e4 = pl.multiple_of(idx(g * GROUP + j), HALF_ROWS)
        ref[pl.ds(base + j, HALF_ROWS, stride=GROUP), :] = tab_ref[pl.ds(e4, HALF_ROWS), :]


def _group_tiles(g, scr):
    ref = scr[g % N_SCR]
    base = (g // N_SCR) * GROUP_ROWS
    return [ref[base + s * GROUP:base + (s + 1) * GROUP, :] for s in range(HALF_ROWS)]


def _unpack(w):
    return pltpu.bitcast(w << 16, F32), pltpu.bitcast(w & jnp.uint32(0xFFFF0000), F32)


def _row_sums_on_lanes(m, eye, ones):
    hi = m.astype(BF16)
    lo = (m - hi.astype(F32)).astype(BF16)
    sums = jnp.dot(hi, ones, preferred_element_type=F32) + jnp.dot(lo, ones, preferred_element_type=F32)
    return jnp.sum(sums * eye, axis=0, keepdims=True)


def _peer_u_kernel(e_ref, h_ref, g_ref, tab_ref, o_ref, prod_a, prod_b, idx_s, sem, *, tt):
    eye = (lax.broadcasted_iota(I32, (PEER_E, LANES), 0) == lax.broadcasted_iota(I32, (PEER_E, LANES), 1)).astype(F32)
    ones = jnp.ones((LANES, LANES), BF16)
    prod = (prod_a, prod_b)

    def finish(t, buf):
        part = buf[pl.ds(0, PEER_E, stride=HALF_ROWS), :]
        for s in range(1, HALF_ROWS):
            part = part + buf[pl.ds(s, PEER_E, stride=HALF_ROWS), :]
        act = jnp.sum(part.T, axis=0, keepdims=True)
        o_ref[pl.ds(t, 1), :] = g_ref[pl.ds(t, 1), :] * _gelu(act)

    prod_b[...] = jnp.zeros(prod_b.shape, F32)

    def tok(t, tl, _, idx):
        finish(jnp.maximum(t - 1, 0), prod[(tl + 1) & 1])
        hv = h_ref[t]
        h_lo = jnp.concatenate([hv[0:HALF_ROWS], hv[0:HALF_ROWS]], axis=0)
        h_hi = jnp.concatenate([hv[HALF_ROWS:], hv[HALF_ROWS:]], axis=0)
        for m in range(PEER_E // 2):
            lo, hi = _unpack(_pair_rows(tab_ref, idx, m))
            prod[tl & 1][SUBLANES * m:SUBLANES * (m + 1), :] = lo * h_lo + hi * h_hi

    _staged_tokens([e_ref], [idx_s], [sem], tt, tok)
    finish(tt - 1, prod[(tt - 1) & 1])


def _peer_u(e4, h3, gates, tab, n, tt):
    row = lambda i: (i, 0)
    return pl.pallas_call(
        functools.partial(_peer_u_kernel, tt=tt),
        grid=(n // tt,),
        in_specs=[
            pl.BlockSpec((tt, PEER_E), row),
            pl.BlockSpec((tt, SUBLANES, LANES), lambda i: (i, 0, 0)),
            pl.BlockSpec((tt, PEER_E), row),
            pl.BlockSpec(tab.shape, lambda i: (0, 0), pipeline_mode=pl.Buffered(1)),
        ],
        out_specs=pl.BlockSpec((tt, PEER_E), row),
        out_shape=jax.ShapeDtypeStruct((n, PEER_E), F32),
        scratch_shapes=[pltpu.VMEM((PEER_E * HALF_ROWS, LANES), F32) for _ in range(2)] + _stage_scratch(I32),
        compiler_params=pltpu.CompilerParams(
            dimension_semantics=("arbitrary",), vmem_limit_bytes=VMEM_TABLE_LIMIT),
        name="peer_u",
    )(e4, h3, gates, tab)


def _peer_v_kernel(e_ref, a_ref, x_ref, mod_ref, nw_ref, tab_ref, o_ref, yg_a, yg_b, idx_s, sem_i, wgt_s, sem_w,
                   *, tt, seq):
    assert STAGE_TOKENS == SUBLANES and STAGE_BUFS == 2
    yg = (yg_a, yg_b)
    top = lax.broadcasted_iota(I32, (SUBLANES, LANES), 0) < HALF_ROWS
    b = (pl.program_id(0) * tt) // seq
    gt2 = mod_ref[pl.ds(b, 1), :][:, 5 * D_MODEL:6 * D_MODEL]

    def tok(t, tl, slot, idx, wgt):
        acc_lo = [jnp.zeros((SUBLANES, LANES), F32) for _ in range(2)]
        acc_hi = [jnp.zeros((SUBLANES, LANES), F32) for _ in range(2)]
        for m in range(PEER_E // 2):
            lo, hi = _unpack(_pair_rows(tab_ref, idx, m))
            w = jnp.where(top, wgt(2 * m), wgt(2 * m + 1))
            acc_lo[m % 2] = acc_lo[m % 2] + w * lo
            acc_hi[m % 2] = acc_hi[m % 2] + w * hi
        lo = acc_lo[0] + acc_lo[1]
        hi = acc_hi[0] + acc_hi[1]
        yg[slot][tl, 0:HALF_ROWS, :] = lo[0:HALF_ROWS] + lo[HALF_ROWS:]
        yg[slot][tl, HALF_ROWS:, :] = hi[0:HALF_ROWS] + hi[HALF_ROWS:]

    def residual(t0, slot):
        y = jnp.concatenate([yg[slot][:, s, :] for s in range(SUBLANES)], axis=1)
        rows = pl.ds(t0, SUBLANES)
        o_ref[rows, :] = x_ref[rows, :] + gt2 * _rms(y, nw_ref[...])

    yg[STAGE_BUFS - 1][...] = jnp.zeros(yg[STAGE_BUFS - 1].shape, F32)
    _staged_tokens([e_ref, a_ref], [idx_s, wgt_s], [sem_i, sem_w], tt, tok, after_group=residual)


def _peer_v(e4, act, x1, mod, nw, tab, n, seq, tt):
    row = lambda i: (i, 0)
    const = lambda i: (0, 0)
    assert seq % tt == 0
    return pl.pallas_call(
        functools.partial(_peer_v_kernel, tt=tt, seq=seq),
        grid=(n // tt,),
        in_specs=[
            pl.BlockSpec((tt, PEER_E), row),
            pl.BlockSpec((tt, PEER_E), row),
            pl.BlockSpec((tt, D_MODEL), row),
            pl.BlockSpec(mod.shape, const),
            pl.BlockSpec((1, D_MODEL), const),
            pl.BlockSpec(tab.shape, const, pipeline_mode=pl.Buffered(1)),
        ],
        out_specs=pl.BlockSpec((tt, D_MODEL), row),
        out_shape=jax.ShapeDtypeStruct((n, D_MODEL), F32),
        scratch_shapes=[pltpu.VMEM((STAGE_TOKENS, SUBLANES, LANES), F32) for _ in range(STAGE_BUFS)]
        + _stage_scratch(I32) + _stage_scratch(F32),
        compiler_params=pltpu.CompilerParams(
            dimension_semantics=("arbitrary",), vmem_limit_bytes=VMEM_TABLE_LIMIT),
        name="peer_v",
    )(e4, act, x1, mod, nw, tab)


def _pack_kernel(x_ref, o_ref):
    half = D_MODEL // 2
    bits = pltpu.bitcast(x_ref[...].astype(BF16).astype(F32), U32)
    w = (bits[:, :half] >> 16) | bits[:, half:]
    for s in range(HALF_ROWS):
        o_ref[pl.ds(s, x_ref.shape[0], stride=HALF_ROWS), :] = w[:, s * LANES:(s + 1) * LANES]


def _pack_table(tab):
    e, d = tab.shape
    te = 512
    assert d == D_MODEL and e % te == 0
    return pl.pallas_call(
        _pack_kernel,
        grid=(e // te,),
        in_specs=[pl.BlockSpec((te, d), lambda i: (i, 0))],
        out_specs=pl.BlockSpec((te * HALF_ROWS, LANES), lambda i: (i, 0)),
        out_shape=jax.ShapeDtypeStruct((e * HALF_ROWS, LANES), U32),
        compiler_params=pltpu.CompilerParams(dimension_semantics=("arbitrary",)),
        name="pack",
    )(tab)


def _tile(seq, cap):
    t = min(seq, cap)
    assert seq % t == 0
    return t


def kernel(x, c, w_ada, b_ada, norm_pre_mix, norm_post_mix, w_in_mix, sc_conv_w, dn_conv_w, dn_a_log,
           dn_dt_bias, dn_out_norm, w_sc_out, w_dn_out, w_mix_out, norm_pre_ffn, norm_post_ffn, peer_w_q,
           peer_sub_keys, peer_u, peer_v):
    batch, seq, d = x.shape
    assert d == D_MODEL and batch <= SUBLANES and seq % DN_CHUNK == 0
    depth = w_ada.shape[0]
    n = batch * seq
    x2 = x.reshape(n, d)
    c_pad = jnp.pad(c, ((0, SUBLANES - batch), (0, 0)))
    tab_u = _pack_table(peer_u)
    tab_v = _pack_table(peer_v)
    rep = lambda v: jnp.repeat(v, DN_DK, axis=-1)
    for l in range(depth):
        mod = _ada(c_pad, w_ada[l], b_ada[l][None, :])
        w = w_in_mix[l]
        o_z = C_SC + C_QKV
        o_a = o_z + C_Z
        w_pad = jnp.concatenate(
            [w[:, :o_a], rep(w[:, o_a:o_a + DN_HEADS]), rep(w[:, o_a + DN_HEADS:o_a + 2 * DN_HEADS]),
             w[:, o_a + 2 * DN_HEADS:]], axis=1).astype(BF16)
        tm = _tile(seq, 512)
        parts = _inproj(x2, mod, norm_pre_mix[l][None, :], w_pad, batch, seq, tm)
        weights = [
            sc_conv_w[l], dn_conv_w[l], rep(dn_a_log[l])[None, :], rep(dn_dt_bias[l])[None, :],
            jnp.tile(dn_out_norm[l], DN_HEADS)[None, :], w_sc_out[l].astype(BF16), w_dn_out[l].astype(BF16),
            w_mix_out[l].astype(BF16), norm_post_mix[l][None, :],
        ]
        x1 = _mixer(parts, x2, mod, weights, batch, seq, _tile(seq, 512))
        h2, q = _qproj(x1, mod, norm_pre_ffn[l][None, :], peer_w_q[l].astype(BF16), batch, seq, tm)
        keys = peer_sub_keys[l].reshape(2 * PEER_HEADS, PEER_NKEYS, PEER_HALF)
        e4, gates = _topk(q, keys, n, _tile(n, LANES))
        tt = _tile(seq, 64)
        act = _peer_u(e4, h2.reshape(n, SUBLANES, LANES), gates, tab_u, n, tt)
        x2 = _peer_v(e4, act, x1, mod, norm_post_ffn[l][None, :], tab_v, n, seq, tt)
    return x2.reshape(batch, seq, d)
```

```python
import functools

import jax
import jax.numpy as jnp
from jax import lax
from jax.experimental import pallas as pl
from jax.experimental.pallas import tpu as pltpu

F32 = jnp.float32
BF16 = jnp.bfloat16
I32 = jnp.int32
U32 = jnp.uint32

D_MODEL = 1024
SC_WIDTH = 512
SC_KERNEL = 3
DN_HEADS = 8
DN_DK = 64
DN_KEY = DN_HEADS * DN_DK
DN_VAL = DN_KEY
DN_CONV = 4
DN_CHUNK = 64
PEER_HEADS = 8
PEER_NKEYS = 128
PEER_HALF = 128
PEER_TOPK = 16
PEER_E = PEER_HEADS * PEER_TOPK
NORM_EPS = 1e-6

LANES = 128
SUBLANES = 8
HALO = SUBLANES
PAIR = 2 * DN_DK
N_PAIRS = DN_HEADS // 2
INV_TERMS = 2
PREP_CHUNKS = 2
HALF_ROWS = SUBLANES // 2
STAGE_TOKENS = 8
STAGE_BUFS = 2
VMEM_TABLE_LIMIT = 56 * 1024 * 1024
VMEM_MIXER_LIMIT = 56 * 1024 * 1024

C_SC = 3 * SC_WIDTH
C_QKV = 2 * DN_KEY + DN_VAL
C_Z = DN_VAL
C_A = DN_KEY
C_B = DN_KEY
C_G = 2 * D_MODEL
C_ALL = C_SC + C_QKV + C_Z + C_A + C_B + C_G


def _silu(x):
    return x * jax.nn.sigmoid(x)


def _softplus(x):
    return jnp.maximum(x, 0.0) + jnp.log(1.0 + jnp.exp(-jnp.abs(x)))


def _gelu(x):
    return 0.5 * x * (1.0 + lax.erf(x * (2.0 ** -0.5)))


def _rms(x, w):
    return x * lax.rsqrt(jnp.mean(x * x, axis=-1, keepdims=True) + NORM_EPS) * w


def _bdot(a, b):
    return jnp.dot(a.astype(BF16), b.astype(BF16), preferred_element_type=F32)


def _split(x, n):
    terms = []
    for i in range(n):
        t = x.astype(BF16)
        terms.append(t)
        if i + 1 < n:
            x = x - t.astype(F32)
    return terms


NN_DIMS = (((1,), (0,)), ((), ()))
NT_DIMS = (((1,), (1,)), ((), ()))


def _sdot(a_terms, b_terms, dims=NN_DIMS):
    order = max(len(a_terms), len(b_terms))
    out = None
    for i, a in enumerate(a_terms):
        for j, b in enumerate(b_terms):
            if i + j < order:
                p = lax.dot_general(a, b, dims, preferred_element_type=F32)
                out = p if out is None else out + p
    return out


def _ada_kernel(c_ref, w_ref, b_ref, o_ref):
    o_ref[...] = _bdot(_silu(c_ref[...]), w_ref[...]) + b_ref[...]


def _ada(c_pad, w_ada, b_ada):
    n_out = w_ada.shape[1]
    tn = 1024
    return pl.pallas_call(
        _ada_kernel,
        grid=(n_out // tn,),
        in_specs=[
            pl.BlockSpec((SUBLANES, D_MODEL), lambda j: (0, 0)),
            pl.BlockSpec((D_MODEL, tn), lambda j: (0, j)),
            pl.BlockSpec((1, tn), lambda j: (0, j)),
        ],
        out_specs=pl.BlockSpec((SUBLANES, tn), lambda j: (0, j)),
        out_shape=jax.ShapeDtypeStruct((SUBLANES, n_out), F32),
        name="ada",
    )(c_pad, w_ada, b_ada)


def _inproj_kernel(x_ref, mod_ref, nw_ref, w_ref, sc_ref, qkv_ref, z_ref, a_ref, b_ref, g_ref):
    b = pl.program_id(0)
    mod = mod_ref[pl.ds(b, 1), :]
    sh1 = mod[:, 0:D_MODEL]
    sc1 = mod[:, D_MODEL:2 * D_MODEL]
    h = (_rms(x_ref[...], nw_ref[...]) * (1.0 + sc1) + sh1).astype(BF16)
    col = 0
    for ref, width in ((sc_ref, C_SC), (qkv_ref, C_QKV), (z_ref, C_Z), (a_ref, C_A), (b_ref, C_B), (g_ref, C_G)):
        for j in range(0, width, 512):
            ref[:, j:j + 512] = jnp.dot(h, w_ref[:, col + j:col + j + 512],
                                        preferred_element_type=F32).astype(ref.dtype)
        col += width


def _inproj(x2, mod, nw, w_pad, batch, seq, tm):
    nt = seq // tm
    row = lambda b, t: (b * nt + t, 0)
    const = lambda b, t: (0, 0)
    n = batch * seq
    outs = [(C_SC, BF16), (C_QKV, BF16), (C_Z, BF16), (C_A, F32), (C_B, F32), (C_G, BF16)]
    return pl.pallas_call(
        _inproj_kernel,
        grid=(batch, nt),
        in_specs=[
            pl.BlockSpec((tm, D_MODEL), row),
            pl.BlockSpec(mod.shape, const),
            pl.BlockSpec((1, D_MODEL), const),
            pl.BlockSpec((D_MODEL, C_ALL), const, pipeline_mode=pl.Buffered(1)),
        ],
        out_specs=[pl.BlockSpec((tm, w), row) for w, _ in outs],
        out_shape=[jax.ShapeDtypeStruct((n, w), dt) for w, dt in outs],
        compiler_params=pltpu.CompilerParams(
            dimension_semantics=("arbitrary", "arbitrary"), vmem_limit_bytes=VMEM_MIXER_LIMIT),
        name="inproj",
    )(x2, mod, nw, w_pad)


def _pair_consts():
    r = lax.broadcasted_iota(I32, (PAIR, PAIR), 0)
    c = lax.broadcasted_iota(I32, (PAIR, PAIR), 1)
    bd = ((r >> 6) == (c >> 6)).astype(F32)
    i = lax.broadcasted_iota(I32, (DN_CHUNK, PAIR), 0)
    j = lax.broadcasted_iota(I32, (DN_CHUNK, PAIR), 1) & (DN_DK - 1)
    return bd, i >= j, i > j, (i == j).astype(F32)


def _stack_bd(y, bd):
    return jnp.concatenate([y, y], axis=0) * bd


def _mixer_kernel(sc_ref, qkv_ref, z_ref, a_ref, b_ref, g_ref, x_ref, mod_ref,
                  scw_ref, dnw_ref, alog_ref, dtb_ref, onw_ref, wsc_ref, wdn_ref, wmix_ref, pnw_ref,
                  o_ref,
                  scx_s, qkvx_s, q_s, k_s, v_s, g_s, beta_s, o_s, gcum_s, aqk_s, u_s, wk_s, qg_s, kdec_s,
                  state_s, *, tt):
    b = pl.program_id(0)
    t = pl.program_id(1)

    @pl.when(t == 0)
    def _():
        scx_s[0:HALO, :] = jnp.zeros((HALO, SC_WIDTH), F32)
        qkvx_s[0:HALO, :] = jnp.zeros((HALO, C_QKV), F32)
        state_s[...] = jnp.zeros(state_s.shape, F32)

    sc = sc_ref[...].astype(F32)
    scx_s[HALO:HALO + tt, :] = sc[:, SC_WIDTH:2 * SC_WIDTH] * sc[:, 2 * SC_WIDTH:]
    conv = scw_ref[0:1, :] * scx_s[pl.ds(HALO - (SC_KERNEL - 1), tt), :]
    for kk in range(1, SC_KERNEL):
        conv = conv + scw_ref[kk:kk + 1, :] * scx_s[pl.ds(HALO - (SC_KERNEL - 1) + kk, tt), :]
    y_sc = _bdot(sc[:, 0:SC_WIDTH] * conv, wsc_ref[...])
    scx_s[0:HALO, :] = scx_s[tt:tt + HALO, :]

    qkvx_s[HALO:HALO + tt, :] = qkv_ref[...].astype(F32)
    cq = dnw_ref[0:1, :] * qkvx_s[pl.ds(HALO - (DN_CONV - 1), tt), :]
    for kk in range(1, DN_CONV):
        cq = cq + dnw_ref[kk:kk + 1, :] * qkvx_s[pl.ds(HALO - (DN_CONV - 1) + kk, tt), :]
    qkvx_s[0:HALO, :] = qkvx_s[tt:tt + HALO, :]
    cq = _silu(cq)
    r512 = lax.broadcasted_iota(I32, (DN_KEY, DN_KEY), 0) >> 6
    c512 = lax.broadcasted_iota(I32, (DN_KEY, DN_KEY), 1) >> 6
    head_ones = [(r512 == c512).astype(BF16)]
    q = cq[:, 0:DN_KEY]
    k = cq[:, DN_KEY:2 * DN_KEY]
    q_s[...] = q * lax.rsqrt(_sdot(_split(q * q, 2), head_ones) + NORM_EPS) * (DN_DK ** -0.5)
    k_s[...] = k * lax.rsqrt(_sdot(_split(k * k, 2), head_ones) + NORM_EPS)
    v_s[...] = cq[:, 2 * DN_KEY:]
    beta_s[...] = jax.nn.sigmoid(b_ref[...])
    g_s[...] = -jnp.exp(alog_ref[...]) * _softplus(a_ref[...] + dtb_ref[...])

    bd, incl, strict, eye = _pair_consts()
    bd_b = bd.astype(BF16)
    ri = lax.broadcasted_iota(I32, (DN_CHUNK, DN_CHUNK), 0)
    ci = lax.broadcasted_iota(I32, (DN_CHUNK, DN_CHUNK), 1)
    lt = [(ci <= ri).astype(BF16)]
    nt = (((1,), (1,)), ((), ()))

    def stack_terms(y):
        return [jnp.concatenate([t_, t_], axis=0) * bd_b for t_ in _split(y, INV_TERMS)]

    pairs = [slice(p * PAIR, (p + 1) * PAIR) for p in range(N_PAIRS)]

    def prepare(c, carry):
        prob = []
        for cc in range(PREP_CHUNKS):
            rows = pl.ds(pl.multiple_of((c * PREP_CHUNKS + cc) * DN_CHUNK, DN_CHUNK), DN_CHUNK)
            gcum_all = _sdot(lt, _split(g_s[rows, :], 3))
            gcum_s[rows, :] = gcum_all
            prob += [(rows, sl, gcum_all[:, sl]) for sl in pairs]
        pws, invs, rest = [], [], []
        for rows, sl, gc in prob:
            qp = q_s[rows, sl]
            kp = k_s[rows, sl]
            bp = beta_s[rows, sl]
            grow = jnp.sum(gc * eye, axis=0, keepdims=True)
            dec = jnp.where(incl, jnp.exp(jnp.where(incl, gc - grow, 0.0)), 0.0)
            eg = jnp.exp(gc)
            kb = kp * bp
            kbig = _stack_bd(kp, bd).astype(BF16)
            kk_s = lax.dot_general(kb.astype(BF16), kbig, nt, preferred_element_type=F32)
            qk_s = lax.dot_general(qp.astype(BF16), kbig, nt, preferred_element_type=F32)
            aqk_s[rows, sl] = qk_s * dec
            qg_s[rows, sl] = qp * eg
            kdec_s[rows, sl] = kp * jnp.exp(gc[DN_CHUNK - 1:DN_CHUNK, :] - gc)
            pw = -(kk_s * jnp.where(strict, dec, 0.0))
            pws.append(pw)
            invs.append(eye + pw)
            rest.append((v_s[rows, sl] * bp, kb * eg))
        for _ in range(5):
            pws = [_sdot(_split(pw, INV_TERMS), stack_terms(pw)) for pw in pws]
            invs = [inv + _sdot(_split(inv, INV_TERMS), stack_terms(pw)) for inv, pw in zip(invs, pws)]
        for (rows, sl, _), inv, (vb, kbg) in zip(prob, invs, rest):
            u_s[rows, sl] = _bdot(inv, _stack_bd(vb, bd))
            wk_s[rows, sl] = _bdot(inv, _stack_bd(kbg, bd))
        return carry

    def advance(c, carry):
        rows = pl.ds(pl.multiple_of(c * DN_CHUNK, DN_CHUNK), DN_CHUNK)
        tail = pl.ds(pl.multiple_of(c * DN_CHUNK + DN_CHUNK - SUBLANES, SUBLANES), SUBLANES)
        sts = [state_s[p] for p in range(N_PAIRS)]
        wss = [_bdot(jnp.concatenate([wk_s[rows, sl], qg_s[rows, sl]], axis=0), st) for sl, st in zip(pairs, sts)]
        vnews = [u_s[rows, sl] - ws[0:DN_CHUNK] for sl, ws in zip(pairs, wss)]
        for p, (sl, st, ws, vnew) in enumerate(zip(pairs, sts, wss, vnews)):
            o_s[rows, sl] = ws[DN_CHUNK:] + _bdot(aqk_s[rows, sl], _stack_bd(vnew, bd))
            glast = gcum_s[tail, sl][SUBLANES - 1:SUBLANES, :]
            state_s[p] = st * jnp.exp(glast) + bd * _bdot(kdec_s[rows, sl].T, vnew)
        return carry

    lax.fori_loop(0, tt // (DN_CHUNK * PREP_CHUNKS), prepare, 0)
    lax.fori_loop(0, tt // DN_CHUNK, advance, 0)

    o = o_s[...]
    ms = _sdot(_split(o * o, 2), head_ones) * (1.0 / DN_DK)
    og = o * lax.rsqrt(ms + NORM_EPS) * onw_ref[...] * _silu(z_ref[...].astype(F32))
    y_dn = _bdot(og, wdn_ref[...])
    gates = g_ref[...].astype(F32)
    merged = jax.nn.sigmoid(gates[:, 0:D_MODEL]) * y_sc + jax.nn.sigmoid(gates[:, D_MODEL:]) * y_dn
    y = _bdot(merged, wmix_ref[...])
    gt1 = mod_ref[pl.ds(b, 1), :][:, 2 * D_MODEL:3 * D_MODEL]
    o_ref[...] = x_ref[...] + gt1 * _rms(y, pnw_ref[...])


def _mixer(parts, x2, mod, weights, batch, seq, tt):
    nt = seq // tt
    row = lambda b, t: (b * nt + t, 0)
    const2 = lambda b, t: (0, 0)
    sc, qkv, z, a, bb, g = parts
    data = [(sc, C_SC), (qkv, C_QKV), (z, C_Z), (a, C_A), (bb, C_B), (g, C_G), (x2, D_MODEL)]
    in_specs = [pl.BlockSpec((tt, w), row) for _, w in data]
    in_specs.append(pl.BlockSpec(mod.shape, const2))
    in_specs += [pl.BlockSpec(wt.shape, const2) for wt in weights]
    scratch = [
        pltpu.VMEM((tt + HALO, SC_WIDTH), F32),
        pltpu.VMEM((tt + HALO, C_QKV), F32),
        pltpu.VMEM((tt, DN_KEY), F32),
        pltpu.VMEM((tt, DN_KEY), F32),
        pltpu.VMEM((tt, DN_VAL), F32),
        pltpu.VMEM((tt, DN_KEY), F32),
        pltpu.VMEM((tt, DN_KEY), F32),
        pltpu.VMEM((tt, DN_VAL), F32),
    ] + [pltpu.VMEM((tt, DN_KEY), F32) for _ in range(6)] + [
        pltpu.VMEM((N_PAIRS, PAIR, PAIR), F32),
    ]
    return pl.pallas_call(
        functools.partial(_mixer_kernel, tt=tt),
        grid=(batch, nt),
        in_specs=in_specs,
        out_specs=pl.BlockSpec((tt, D_MODEL), row),
        out_shape=jax.ShapeDtypeStruct((batch * seq, D_MODEL), F32),
        scratch_shapes=scratch,
        compiler_params=pltpu.CompilerParams(
            dimension_semantics=("arbitrary", "arbitrary"), vmem_limit_bytes=VMEM_MIXER_LIMIT),
        name="mixer",
    )(*[d for d, _ in data], mod, *weights)


def _qproj_kernel(x_ref, mod_ref, nw_ref, w_ref, h_ref, q_ref):
    b = pl.program_id(0)
    mod = mod_ref[pl.ds(b, 1), :]
    sh2 = mod[:, 3 * D_MODEL:4 * D_MODEL]
    sc2 = mod[:, 4 * D_MODEL:5 * D_MODEL]
    h = _rms(x_ref[...], nw_ref[...]) * (1.0 + sc2) + sh2
    h_ref[...] = h
    q_ref[...] = _bdot(h, w_ref[...])


def _qproj(x1, mod, nw, wq, batch, seq, tm):
    nt = seq // tm
    row = lambda b, t: (b * nt + t, 0)
    const = lambda b, t: (0, 0)
    n = batch * seq
    nq = wq.shape[1]
    return pl.pallas_call(
        _qproj_kernel,
        grid=(batch, nt),
        in_specs=[
            pl.BlockSpec((tm, D_MODEL), row),
            pl.BlockSpec(mod.shape, const),
            pl.BlockSpec((1, D_MODEL), const),
            pl.BlockSpec(wq.shape, const),
        ],
        out_specs=[pl.BlockSpec((tm, D_MODEL), row), pl.BlockSpec((tm, nq), row)],
        out_shape=[jax.ShapeDtypeStruct((n, D_MODEL), F32), jax.ShapeDtypeStruct((n, nq), F32)],
        compiler_params=pltpu.CompilerParams(dimension_semantics=("arbitrary", "arbitrary")),
        name="qproj",
    )(x1, mod, nw, wq)


def _sort16_network():
    n, pairs, p = 16, [], 1
    while p < n:
        k = p
        while k >= 1:
            for j in range(k % p, n - k, 2 * k):
                for i in range(min(k, n - j - k)):
                    if (i + j) // (2 * p) == (i + j + k) // (2 * p):
                        pairs.append((i + j, i + j + k))
            k //= 2
        p *= 2
    return pairs


def _top16_of_128(st, v_out, i_out):
    k = PEER_TOPK
    vs = [st[SUBLANES * i:SUBLANES * (i + 1), :] for i in range(k)]
    sub = lax.broadcasted_iota(I32, vs[0].shape, 0).astype(F32)
    ix = [sub + float(SUBLANES * i) for i in range(k)]
    for a, b in _sort16_network():
        swap = (vs[b] > vs[a]) | ((vs[b] == vs[a]) & (ix[b] < ix[a]))
        vs[a], vs[b] = jnp.where(swap, vs[b], vs[a]), jnp.where(swap, vs[a], vs[b])
        ix[a], ix[b] = jnp.where(swap, ix[b], ix[a]), jnp.where(swap, ix[a], ix[b])
    for it in range(k):
        m = jnp.max(vs[0], axis=0, keepdims=True)
        pos = jnp.min(jnp.where(vs[0] == m, ix[0], float(PEER_NKEYS)), axis=0, keepdims=True)
        v_out[it:it + 1, :] = m
        i_out[it:it + 1, :] = pos
        hit = ix[0] == pos
        depth = k - 1 - it
        for d in range(depth):
            vs[d] = jnp.where(hit, vs[d + 1], vs[d])
            ix[d] = jnp.where(hit, ix[d + 1], ix[d])


def _top16_of_pairs(v1, i1, v2, i2, v_out, e_out, row0):
    k = PEER_TOPK
    rows = lax.broadcasted_iota(I32, v1.shape, 0).astype(F32)
    cand = v1 + v2[0:1, :]
    taken = jnp.zeros_like(v1)
    for it in range(k):
        m = jnp.max(cand, axis=0, keepdims=True)
        a_win = jnp.min(jnp.where(cand == m, rows, float(k)), axis=0, keepdims=True)
        hit = rows == a_win
        b_win = jnp.sum(jnp.where(hit, taken, 0.0), axis=0, keepdims=True)
        i1_win = jnp.sum(jnp.where(hit, i1, 0.0), axis=0, keepdims=True)
        i2_win = jnp.sum(jnp.where(rows == b_win, i2, 0.0), axis=0, keepdims=True)
        v_out[row0 + it:row0 + it + 1, :] = m
        e_out[row0 + it:row0 + it + 1, :] = i1_win * float(PEER_NKEYS) + i2_win
        if it + 1 < k:
            v1_win = jnp.sum(jnp.where(hit, v1, 0.0), axis=0, keepdims=True)
            nxt = rows == b_win + 1.0
            v2_nxt = jnp.sum(jnp.where(nxt, v2, 0.0), axis=0, keepdims=True)
            new = jnp.where(b_win + 1.0 < float(k), v1_win + v2_nxt, -jnp.inf)
            cand = jnp.where(hit, new, cand)
            taken = jnp.where(hit, b_win + 1.0, taken)


def _topk_kernel(q_ref, keys_ref, e_ref, g_ref, v1_s, i1_s, v2_s, i2_s, ts_s, ei_s, gate_s, *, tt):
    k = PEER_TOPK
    for h in range(PEER_HEADS):
        for p, (v_s, i_s) in enumerate(((v1_s, i1_s), (v2_s, i2_s))):
            col = (2 * h + p) * PEER_HALF
            st = _sdot(_split(keys_ref[2 * h + p], 2), _split(q_ref[:, col:col + PEER_HALF], 2), NT_DIMS)
            _top16_of_128(st, v_s, i_s)
        _top16_of_pairs(v1_s[...], i1_s[...], v2_s[...], i2_s[...], ts_s, ei_s, h * k)
        ts = ts_s[h * k:(h + 1) * k, :]
        e = jnp.exp(ts - jnp.max(ts, axis=0, keepdims=True))
        gate_s[h * k:(h + 1) * k, :] = e / jnp.sum(e, axis=0, keepdims=True)
    e_ref[...] = ei_s[...].T.astype(I32) * HALF_ROWS
    g_ref[...] = gate_s[...].T


def _topk(q, keys, n, tt):
    row = lambda i: (i, 0)
    return pl.pallas_call(
        functools.partial(_topk_kernel, tt=tt),
        grid=(n // tt,),
        in_specs=[
            pl.BlockSpec((tt, q.shape[1]), row),
            pl.BlockSpec(keys.shape, lambda i: (0, 0, 0)),
        ],
        out_specs=[pl.BlockSpec((tt, PEER_E), row)] * 2,
        out_shape=[jax.ShapeDtypeStruct((n, PEER_E), I32), jax.ShapeDtypeStruct((n, PEER_E), F32)],
        scratch_shapes=[
            pltpu.VMEM((PEER_TOPK, tt), F32), pltpu.VMEM((PEER_TOPK, tt), F32),
            pltpu.VMEM((PEER_TOPK, tt), F32), pltpu.VMEM((PEER_TOPK, tt), F32),
            pltpu.VMEM((PEER_E, tt), F32), pltpu.VMEM((PEER_E, tt), F32),
            pltpu.VMEM((PEER_E, tt), F32),
        ],
        compiler_params=pltpu.CompilerParams(dimension_semantics=("arbitrary",)),
        name="topk",
    )(q, keys)


def _staged_tokens(srcs, bufs, sems, tt, body, after_group=None):
    n_groups = tt // STAGE_TOKENS
    nb = STAGE_BUFS
    assert n_groups % nb == 0 and STAGE_TOKENS % 2 == 0

    def copies(grp, slot):
        return [pltpu.make_async_copy(src.at[pl.ds(grp * STAGE_TOKENS, STAGE_TOKENS)], buf.at[slot], sem.at[slot])
                for src, buf, sem in zip(srcs, bufs, sems)]

    for s in range(nb - 1):
        for c in copies(s, s):
            c.start()

    def round_of_groups(i, carry):
        for slot in range(nb):
            grp = nb * i + slot
            for c in copies(jnp.minimum(grp + nb - 1, n_groups - 1), (slot + nb - 1) % nb):
                c.start()
            for c in copies(grp, slot):
                c.wait()
            if after_group is not None:
                after_group(pl.multiple_of(jnp.maximum(grp - 1, 0) * STAGE_TOKENS, STAGE_TOKENS), (slot - 1) % nb)
            for tl in range(STAGE_TOKENS):
                body(grp * STAGE_TOKENS + tl, tl, slot,
                     *[lambda k, buf=buf, slot=slot, tl=tl: buf[slot, tl, k] for buf in bufs])
        return carry

    lax.fori_loop(0, n_groups // nb, round_of_groups, 0)
    for s in range(nb - 1):
        for c in copies(n_groups - 1, (n_groups + s) % nb):
            c.wait()
    if after_group is not None:
        after_group((n_groups - 1) * STAGE_TOKENS, (n_groups - 1) % nb)


def _stage_scratch(dtype):
    return [pltpu.SMEM((STAGE_BUFS, STAGE_TOKENS, PEER_E), dtype), pltpu.SemaphoreType.DMA((STAGE_BUFS,))]


def _pair_rows(tab_ref, idx, m):
    wa = tab_ref[pl.ds(pl.multiple_of(idx(2 * m), HALF_ROWS), HALF_ROWS), :]
    wb = tab_ref[pl.ds(pl.multiple_of(idx(2 * m + 1), HALF_ROWS), HALF_ROWS), :]
    return jnp.concatenate([wa, wb], axis=0)


def _unpack(w):
    return pltpu.bitcast(w << 16, F32), pltpu.bitcast(w & jnp.uint32(0xFFFF0000), F32)


def _peer_u_kernel(e_ref, h_ref, g_ref, tab_ref, o_ref, prod_a, prod_b, idx_s, sem, *, tt):
    prod = (prod_a, prod_b)

    def finish(t, buf):
        part = buf[pl.ds(0, PEER_E, stride=HALF_ROWS), :]
        for s in range(1, HALF_ROWS):
            part = part + buf[pl.ds(s, PEER_E, stride=HALF_ROWS), :]
        act = jnp.sum(part.T, axis=0, keepdims=True)
        o_ref[pl.ds(t, 1), :] = g_ref[pl.ds(t, 1), :] * _gelu(act)

    prod_b[...] = jnp.zeros(prod_b.shape, F32)

    def tok(t, tl, _, idx):
        finish(jnp.maximum(t - 1, 0), prod[(tl + 1) & 1])
        hv = h_ref[t]
        h_lo = jnp.concatenate([hv[0:HALF_ROWS], hv[0:HALF_ROWS]], axis=0)
        h_hi = jnp.concatenate([hv[HALF_ROWS:], hv[HALF_ROWS:]], axis=0)
        for m in range(PEER_E // 2):
            lo, hi = _unpack(_pair_rows(tab_ref, idx, m))
            prod[tl & 1][SUBLANES * m:SUBLANES * (m + 1), :] = lo * h_lo + hi * h_hi

    _staged_tokens([e_ref], [idx_s], [sem], tt, tok)
    finish(tt - 1, prod[(tt - 1) & 1])


def _peer_u(e4, h3, gates, tab, n, tt):
    row = lambda i: (i, 0)
    return pl.pallas_call(
        functools.partial(_peer_u_kernel, tt=tt),
        grid=(n // tt,),
        in_specs=[
            pl.BlockSpec((tt, PEER_E), row),
            pl.BlockSpec((tt, SUBLANES, LANES), lambda i: (i, 0, 0)),
            pl.BlockSpec((tt, PEER_E), row),
            pl.BlockSpec(tab.shape, lambda i: (0, 0), pipeline_mode=pl.Buffered(1)),
        ],
        out_specs=pl.BlockSpec((tt, PEER_E), row),
        out_shape=jax.ShapeDtypeStruct((n, PEER_E), F32),
        scratch_shapes=[pltpu.VMEM((PEER_E * HALF_ROWS, LANES), F32) for _ in range(2)] + _stage_scratch(I32),
        compiler_params=pltpu.CompilerParams(
            dimension_semantics=("arbitrary",), vmem_limit_bytes=VMEM_TABLE_LIMIT),
        name="peer_u",
    )(e4, h3, gates, tab)


def _peer_v_kernel(e_ref, a_ref, x_ref, mod_ref, nw_ref, tab_ref, o_ref, yg_a, yg_b, idx_s, sem_i, wgt_s, sem_w,
                   *, tt, seq):
    assert STAGE_TOKENS == SUBLANES and STAGE_BUFS == 2
    yg = (yg_a, yg_b)
    top = lax.broadcasted_iota(I32, (SUBLANES, LANES), 0) < HALF_ROWS
    b = (pl.program_id(0) * tt) // seq
    gt2 = mod_ref[pl.ds(b, 1), :][:, 5 * D_MODEL:6 * D_MODEL]

    def tok(t, tl, slot, idx, wgt):
        acc_lo = [jnp.zeros((SUBLANES, LANES), F32) for _ in range(2)]
        acc_hi = [jnp.zeros((SUBLANES, LANES), F32) for _ in range(2)]
        for m in range(PEER_E // 2):
            lo, hi = _unpack(_pair_rows(tab_ref, idx, m))
            w = jnp.where(top, wgt(2 * m), wgt(2 * m + 1))
            acc_lo[m % 2] = acc_lo[m % 2] + w * lo
            acc_hi[m % 2] = acc_hi[m % 2] + w * hi
        lo = acc_lo[0] + acc_lo[1]
        hi = acc_hi[0] + acc_hi[1]
        yg[slot][tl, 0:HALF_ROWS, :] = lo[0:HALF_ROWS] + lo[HALF_ROWS:]
        yg[slot][tl, HALF_ROWS:, :] = hi[0:HALF_ROWS] + hi[HALF_ROWS:]

    def residual(t0, slot):
        y = jnp.concatenate([yg[slot][:, s, :] for s in range(SUBLANES)], axis=1)
        rows = pl.ds(t0, SUBLANES)
        o_ref[rows, :] = x_ref[rows, :] + gt2 * _rms(y, nw_ref[...])

    yg[STAGE_BUFS - 1][...] = jnp.zeros(yg[STAGE_BUFS - 1].shape, F32)
    _staged_tokens([e_ref, a_ref], [idx_s, wgt_s], [sem_i, sem_w], tt, tok, after_group=residual)


def _peer_v(e4, act, x1, mod, nw, tab, n, seq, tt):
    row = lambda i: (i, 0)
    const = lambda i: (0, 0)
    assert seq % tt == 0
    return pl.pallas_call(
        functools.partial(_peer_v_kernel, tt=tt, seq=seq),
        grid=(n // tt,),
        in_specs=[
            pl.BlockSpec((tt, PEER_E), row),
            pl.BlockSpec((tt, PEER_E), row),
            pl.BlockSpec((tt, D_MODEL), row),
            pl.BlockSpec(mod.shape, const),
            pl.BlockSpec((1, D_MODEL), const),
            pl.BlockSpec(tab.shape, const, pipeline_mode=pl.Buffered(1)),
        ],
        out_specs=pl.BlockSpec((tt, D_MODEL), row),
        out_shape=jax.ShapeDtypeStruct((n, D_MODEL), F32),
        scratch_shapes=[pltpu.VMEM((STAGE_TOKENS, SUBLANES, LANES), F32) for _ in range(STAGE_BUFS)]
        + _stage_scratch(I32) + _stage_scratch(F32),
        compiler_params=pltpu.CompilerParams(
            dimension_semantics=("arbitrary",), vmem_limit_bytes=VMEM_TABLE_LIMIT),
        name="peer_v",
    )(e4, act, x1, mod, nw, tab)


def _pack_kernel(x_ref, o_ref):
    half = D_MODEL // 2
    bits = pltpu.bitcast(x_ref[...].astype(BF16).astype(F32), U32)
    w = (bits[:, :half] >> 16) | bits[:, half:]
    for s in range(HALF_ROWS):
        o_ref[pl.ds(s, x_ref.shape[0], stride=HALF_ROWS), :] = w[:, s * LANES:(s + 1) * LANES]


def _pack_table(tab):
    e, d = tab.shape
    te = 512
    assert d == D_MODEL and e % te == 0
    return pl.pallas_call(
        _pack_kernel,
        grid=(e // te,),
        in_specs=[pl.BlockSpec((te, d), lambda i: (i, 0))],
        out_specs=pl.BlockSpec((te * HALF_ROWS, LANES), lambda i: (i, 0)),
        out_shape=jax.ShapeDtypeStruct((e * HALF_ROWS, LANES), U32),
        compiler_params=pltpu.CompilerParams(dimension_semantics=("arbitrary",)),
        name="pack",
    )(tab)


def _tile(seq, cap):
    t = min(seq, cap)
    assert seq % t == 0
    return t


def kernel(x, c, w_ada, b_ada, norm_pre_mix, norm_post_mix, w_in_mix, sc_conv_w, dn_conv_w, dn_a_log,
           dn_dt_bias, dn_out_norm, w_sc_out, w_dn_out, w_mix_out, norm_pre_ffn, norm_post_ffn, peer_w_q,
           peer_sub_keys, peer_u, peer_v):
    batch, seq, d = x.shape
    assert d == D_MODEL and batch <= SUBLANES and seq % DN_CHUNK == 0
    depth = w_ada.shape[0]
    n = batch * seq
    x2 = x.reshape(n, d)
    c_pad = jnp.pad(c, ((0, SUBLANES - batch), (0, 0)))
    tab_u = _pack_table(peer_u)
    tab_v = _pack_table(peer_v)
    rep = lambda v: jnp.repeat(v, DN_DK, axis=-1)
    for l in range(depth):
        mod = _ada(c_pad, w_ada[l], b_ada[l][None, :])
        w = w_in_mix[l]
        o_z = C_SC + C_QKV
        o_a = o_z + C_Z
        w_pad = jnp.concatenate(
            [w[:, :o_a], rep(w[:, o_a:o_a + DN_HEADS]), rep(w[:, o_a + DN_HEADS:o_a + 2 * DN_HEADS]),
             w[:, o_a + 2 * DN_HEADS:]], axis=1).astype(BF16)
        tm = _tile(seq, 512)
        parts = _inproj(x2, mod, norm_pre_mix[l][None, :], w_pad, batch, seq, tm)
        weights = [
            sc_conv_w[l], dn_conv_w[l], rep(dn_a_log[l])[None, :], rep(dn_dt_bias[l])[None, :],
            jnp.tile(dn_out_norm[l], DN_HEADS)[None, :], w_sc_out[l].astype(BF16), w_dn_out[l].astype(BF16),
            w_mix_out[l].astype(BF16), norm_post_mix[l][None, :],
        ]
        x1 = _mixer(parts, x2, mod, weights, batch, seq, _tile(seq, 512))
        h2, q = _qproj(x1, mod, norm_pre_ffn[l][None, :], peer_w_q[l].astype(BF16), batch, seq, tm)
        keys = peer_sub_keys[l].reshape(2 * PEER_HEADS, PEER_NKEYS, PEER_HALF)
        e4, gates = _topk(q, keys, n, _tile(n, LANES))
        tt = _tile(seq, 64)
        act = _peer_u(e4, h2.reshape(n, SUBLANES, LANES), gates, tab_u, n, tt)
        x2 = _peer_v(e4, act, x1, mod, norm_post_ffn[l][None, :], tab_v, n, seq, tt)
    return x2.reshape(batch, seq, d)
```

```python
import functools

import jax
import jax.numpy as jnp
from jax import lax
from jax.experimental import pallas as pl
from jax.experimental.pallas import tpu as pltpu

F32 = jnp.float32
BF16 = jnp.bfloat16
I32 = jnp.int32
U32 = jnp.uint32

D_MODEL = 1024
SC_WIDTH = 512
SC_KERNEL = 3
DN_HEADS = 8
DN_DK = 64
DN_KEY = DN_HEADS * DN_DK
DN_VAL = DN_KEY
DN_CONV = 4
DN_CHUNK = 64
PEER_HEADS = 8
PEER_NKEYS = 128
PEER_HALF = 128
PEER_TOPK = 16
PEER_E = PEER_HEADS * PEER_TOPK
NORM_EPS = 1e-6

LANES = 128
SUBLANES = 8
HALO = SUBLANES
PAIR = 2 * DN_DK
N_PAIRS = DN_HEADS // 2
INV_TERMS = 2
PREP_CHUNKS = 2
HALF_ROWS = SUBLANES // 2
STAGE_TOKENS = 8
STAGE_BUFS = 2
VMEM_TABLE_LIMIT = 56 * 1024 * 1024
VMEM_MIXER_LIMIT = 56 * 1024 * 1024

C_SC = 3 * SC_WIDTH
C_QKV = 2 * DN_KEY + DN_VAL
C_Z = DN_VAL
C_A = DN_KEY
C_B = DN_KEY
C_G = 2 * D_MODEL
C_ALL = C_SC + C_QKV + C_Z + C_A + C_B + C_G


def _silu(x):
    return x * jax.nn.sigmoid(x)


def _softplus(x):
    return jnp.maximum(x, 0.0) + jnp.log(1.0 + jnp.exp(-jnp.abs(x)))


def _gelu(x):
    return 0.5 * x * (1.0 + lax.erf(x * (2.0 ** -0.5)))


def _rms(x, w):
    return x * lax.rsqrt(jnp.mean(x * x, axis=-1, keepdims=True) + NORM_EPS) * w


def _bdot(a, b):
    return jnp.dot(a.astype(BF16), b.astype(BF16), preferred_element_type=F32)


def _split(x, n):
    terms = []
    for i in range(n):
        t = x.astype(BF16)
        terms.append(t)
        if i + 1 < n:
            x = x - t.astype(F32)
    return terms


NN_DIMS = (((1,), (0,)), ((), ()))
NT_DIMS = (((1,), (1,)), ((), ()))


def _sdot(a_terms, b_terms, dims=NN_DIMS):
    order = max(len(a_terms), len(b_terms))
    out = None
    for i, a in enumerate(a_terms):
        for j, b in enumerate(b_terms):
            if i + j < order:
                p = lax.dot_general(a, b, dims, preferred_element_type=F32)
                out = p if out is None else out + p
    return out


def _ada_kernel(c_ref, w_ref, b_ref, o_ref):
    o_ref[...] = _bdot(_silu(c_ref[...]), w_ref[...]) + b_ref[...]


def _ada(c_pad, w_ada, b_ada):
    n_out = w_ada.shape[1]
    tn = 1024
    return pl.pallas_call(
        _ada_kernel,
        grid=(n_out // tn,),
        in_specs=[
            pl.BlockSpec((SUBLANES, D_MODEL), lambda j: (0, 0)),
            pl.BlockSpec((D_MODEL, tn), lambda j: (0, j)),
            pl.BlockSpec((1, tn), lambda j: (0, j)),
        ],
        out_specs=pl.BlockSpec((SUBLANES, tn), lambda j: (0, j)),
        out_shape=jax.ShapeDtypeStruct((SUBLANES, n_out), F32),
        name="ada",
    )(c_pad, w_ada, b_ada)


def _inproj_kernel(x_ref, mod_ref, nw_ref, w_ref, sc_ref, qkv_ref, z_ref, a_ref, b_ref, g_ref):
    b = pl.program_id(0)
    mod = mod_ref[pl.ds(b, 1), :]
    sh1 = mod[:, 0:D_MODEL]
    sc1 = mod[:, D_MODEL:2 * D_MODEL]
    h = (_rms(x_ref[...], nw_ref[...]) * (1.0 + sc1) + sh1).astype(BF16)
    col = 0
    for ref, width in ((sc_ref, C_SC), (qkv_ref, C_QKV), (z_ref, C_Z), (a_ref, C_A), (b_ref, C_B), (g_ref, C_G)):
        for j in range(0, width, 512):
            ref[:, j:j + 512] = jnp.dot(h, w_ref[:, col + j:col + j + 512],
                                        preferred_element_type=F32).astype(ref.dtype)
        col += width


def _inproj(x2, mod, nw, w_pad, batch, seq, tm):
    nt = seq // tm
    row = lambda b, t: (b * nt + t, 0)
    const = lambda b, t: (0, 0)
    n = batch * seq
    outs = [(C_SC, BF16), (C_QKV, BF16), (C_Z, BF16), (C_A, F32), (C_B, F32), (C_G, BF16)]
    return pl.pallas_call(
        _inproj_kernel,
        grid=(batch, nt),
        in_specs=[
            pl.BlockSpec((tm, D_MODEL), row),
            pl.BlockSpec(mod.shape, const),
            pl.BlockSpec((1, D_MODEL), const),
            pl.BlockSpec((D_MODEL, C_ALL), const, pipeline_mode=pl.Buffered(1)),
        ],
        out_specs=[pl.BlockSpec((tm, w), row) for w, _ in outs],
        out_shape=[jax.ShapeDtypeStruct((n, w), dt) for w, dt in outs],
        compiler_params=pltpu.CompilerParams(
            dimension_semantics=("arbitrary", "arbitrary"), vmem_limit_bytes=VMEM_MIXER_LIMIT),
        name="inproj",
    )(x2, mod, nw, w_pad)


def _pair_consts():
    r = lax.broadcasted_iota(I32, (PAIR, PAIR), 0)
    c = lax.broadcasted_iota(I32, (PAIR, PAIR), 1)
    bd = ((r >> 6) == (c >> 6)).astype(F32)
    i = lax.broadcasted_iota(I32, (DN_CHUNK, PAIR), 0)
    j = lax.broadcasted_iota(I32, (DN_CHUNK, PAIR), 1) & (DN_DK - 1)
    return bd, i >= j, i > j, (i == j).astype(F32)


def _stack_bd(y, bd):
    return jnp.concatenate([y, y], axis=0) * bd


def _mixer_kernel(sc_ref, qkv_ref, z_ref, a_ref, b_ref, g_ref, x_ref, mod_ref,
                  scw_ref, dnw_ref, alog_ref, dtb_ref, onw_ref, wsc_ref, wdn_ref, wmix_ref, pnw_ref,
                  o_ref,
                  scx_s, qkvx_s, q_s, k_s, v_s, g_s, beta_s, o_s, gcum_s, aqk_s, u_s, wk_s, qg_s, kdec_s,
                  state_s, *, tt):
    b = pl.program_id(0)
    t = pl.program_id(1)

    @pl.when(t == 0)
    def _():
        scx_s[0:HALO, :] = jnp.zeros((HALO, SC_WIDTH), F32)
        qkvx_s[0:HALO, :] = jnp.zeros((HALO, C_QKV), F32)
        state_s[...] = jnp.zeros(state_s.shape, F32)

    sc = sc_ref[...].astype(F32)
    scx_s[HALO:HALO + tt, :] = sc[:, SC_WIDTH:2 * SC_WIDTH] * sc[:, 2 * SC_WIDTH:]
    conv = scw_ref[0:1, :] * scx_s[pl.ds(HALO - (SC_KERNEL - 1), tt), :]
    for kk in range(1, SC_KERNEL):
        conv = conv + scw_ref[kk:kk + 1, :] * scx_s[pl.ds(HALO - (SC_KERNEL - 1) + kk, tt), :]
    y_sc = _bdot(sc[:, 0:SC_WIDTH] * conv, wsc_ref[...])
    scx_s[0:HALO, :] = scx_s[tt:tt + HALO, :]

    qkvx_s[HALO:HALO + tt, :] = qkv_ref[...].astype(F32)
    cq = dnw_ref[0:1, :] * qkvx_s[pl.ds(HALO - (DN_CONV - 1), tt), :]
    for kk in range(1, DN_CONV):
        cq = cq + dnw_ref[kk:kk + 1, :] * qkvx_s[pl.ds(HALO - (DN_CONV - 1) + kk, tt), :]
    qkvx_s[0:HALO, :] = qkvx_s[tt:tt + HALO, :]
    cq = _silu(cq)
    r512 = lax.broadcasted_iota(I32, (DN_KEY, DN_KEY), 0) >> 6
    c512 = lax.broadcasted_iota(I32, (DN_KEY, DN_KEY), 1) >> 6
    head_ones = [(r512 == c512).astype(BF16)]
    q = cq[:, 0:DN_KEY]
    k = cq[:, DN_KEY:2 * DN_KEY]
    q_s[...] = q * lax.rsqrt(_sdot(_split(q * q, 2), head_ones) + NORM_EPS) * (DN_DK ** -0.5)
    k_s[...] = k * lax.rsqrt(_sdot(_split(k * k, 2), head_ones) + NORM_EPS)
    v_s[...] = cq[:, 2 * DN_KEY:]
    beta_s[...] = jax.nn.sigmoid(b_ref[...])
    g_s[...] = -jnp.exp(alog_ref[...]) * _softplus(a_ref[...] + dtb_ref[...])

    bd, incl, strict, eye = _pair_consts()
    bd_b = bd.astype(BF16)
    ri = lax.broadcasted_iota(I32, (DN_CHUNK, DN_CHUNK), 0)
    ci = lax.broadcasted_iota(I32, (DN_CHUNK, DN_CHUNK), 1)
    lt = [(ci <= ri).astype(BF16)]
    nt = (((1,), (1,)), ((), ()))

    def stack_terms(y):
        return [jnp.concatenate([t_, t_], axis=0) * bd_b for t_ in _split(y, INV_TERMS)]

    pairs = [slice(p * PAIR, (p + 1) * PAIR) for p in range(N_PAIRS)]

    def prepare(c, carry):
        prob = []
        for cc in range(PREP_CHUNKS):
            rows = pl.ds(pl.multiple_of((c * PREP_CHUNKS + cc) * DN_CHUNK, DN_CHUNK), DN_CHUNK)
            gcum_all = _sdot(lt, _split(g_s[rows, :], 3))
            gcum_s[rows, :] = gcum_all
            prob += [(rows, sl, gcum_all[:, sl]) for sl in pairs]
        pws, invs, rest = [], [], []
        for rows, sl, gc in prob:
            qp = q_s[rows, sl]
            kp = k_s[rows, sl]
            bp = beta_s[rows, sl]
            grow = jnp.sum(gc * eye, axis=0, keepdims=True)
            dec = jnp.where(incl, jnp.exp(jnp.where(incl, gc - grow, 0.0)), 0.0)
            eg = jnp.exp(gc)
            kb = kp * bp
            kbig = _stack_bd(kp, bd).astype(BF16)
            kk_s = lax.dot_general(kb.astype(BF16), kbig, nt, preferred_element_type=F32)
            qk_s = lax.dot_general(qp.astype(BF16), kbig, nt, preferred_element_type=F32)
            aqk_s[rows, sl] = qk_s * dec
            qg_s[rows, sl] = qp * eg
            kdec_s[rows, sl] = kp * jnp.exp(gc[DN_CHUNK - 1:DN_CHUNK, :] - gc)
            pw = -(kk_s * jnp.where(strict, dec, 0.0))
            pws.append(pw)
            invs.append(eye + pw)
            rest.append((v_s[rows, sl] * bp, kb * eg))
        for _ in range(5):
            pws = [_sdot(_split(pw, INV_TERMS), stack_terms(pw)) for pw in pws]
            invs = [inv + _sdot(_split(inv, INV_TERMS), stack_terms(pw)) for inv, pw in zip(invs, pws)]
        for (rows, sl, _), inv, (vb, kbg) in zip(prob, invs, rest):
            u_s[rows, sl] = _bdot(inv, _stack_bd(vb, bd))
            wk_s[rows, sl] = _bdot(inv, _stack_bd(kbg, bd))
        return carry

    def advance(c, carry):
        rows = pl.ds(pl.multiple_of(c * DN_CHUNK, DN_CHUNK), DN_CHUNK)
        tail = pl.ds(pl.multiple_of(c * DN_CHUNK + DN_CHUNK - SUBLANES, SUBLANES), SUBLANES)
        sts = [state_s[p] for p in range(N_PAIRS)]
        wss = [_bdot(jnp.concatenate([wk_s[rows, sl], qg_s[rows, sl]], axis=0), st) for sl, st in zip(pairs, sts)]
        vnews = [u_s[rows, sl] - ws[0:DN_CHUNK] for sl, ws in zip(pairs, wss)]
        for p, (sl, st, ws, vnew) in enumerate(zip(pairs, sts, wss, vnews)):
            o_s[rows, sl] = ws[DN_CHUNK:] + _bdot(aqk_s[rows, sl], _stack_bd(vnew, bd))
            glast = gcum_s[tail, sl][SUBLANES - 1:SUBLANES, :]
            state_s[p] = st * jnp.exp(glast) + bd * _bdot(kdec_s[rows, sl].T, vnew)
        return carry

    lax.fori_loop(0, tt // (DN_CHUNK * PREP_CHUNKS), prepare, 0)
    lax.fori_loop(0, tt // DN_CHUNK, advance, 0)

    o = o_s[...]
    ms = _sdot(_split(o * o, 2), head_ones) * (1.0 / DN_DK)
    og = o * lax.rsqrt(ms + NORM_EPS) * onw_ref[...] * _silu(z_ref[...].astype(F32))
    y_dn = _bdot(og, wdn_ref[...])
    gates = g_ref[...].astype(F32)
    merged = jax.nn.sigmoid(gates[:, 0:D_MODEL]) * y_sc + jax.nn.sigmoid(gates[:, D_MODEL:]) * y_dn
    y = _bdot(merged, wmix_ref[...])
    gt1 = mod_ref[pl.ds(b, 1), :][:, 2 * D_MODEL:3 * D_MODEL]
    o_ref[...] = x_ref[...] + gt1 * _rms(y, pnw_ref[...])


def _mixer(parts, x2, mod, weights, batch, seq, tt):
    nt = seq // tt
    row = lambda b, t: (b * nt + t, 0)
    const2 = lambda b, t: (0, 0)
    sc, qkv, z, a, bb, g = parts
    data = [(sc, C_SC), (qkv, C_QKV), (z, C_Z), (a, C_A), (bb, C_B), (g, C_G), (x2, D_MODEL)]
    in_specs = [pl.BlockSpec((tt, w), row) for _, w in data]
    in_specs.append(pl.BlockSpec(mod.shape, const2))
    in_specs += [pl.BlockSpec(wt.shape, const2) for wt in weights]
    scratch = [
        pltpu.VMEM((tt + HALO, SC_WIDTH), F32),
        pltpu.VMEM((tt + HALO, C_QKV), F32),
        pltpu.VMEM((tt, DN_KEY), F32),
        pltpu.VMEM((tt, DN_KEY), F32),
        pltpu.VMEM((tt, DN_VAL), F32),
        pltpu.VMEM((tt, DN_KEY), F32),
        pltpu.VMEM((tt, DN_KEY), F32),
        pltpu.VMEM((tt, DN_VAL), F32),
    ] + [pltpu.VMEM((tt, DN_KEY), F32) for _ in range(6)] + [
        pltpu.VMEM((N_PAIRS, PAIR, PAIR), F32),
    ]
    return pl.pallas_call(
        functools.partial(_mixer_kernel, tt=tt),
        grid=(batch, nt),
        in_specs=in_specs,
        out_specs=pl.BlockSpec((tt, D_MODEL), row),
        out_shape=jax.ShapeDtypeStruct((batch * seq, D_MODEL), F32),
        scratch_shapes=scratch,
        compiler_params=pltpu.CompilerParams(
            dimension_semantics=("arbitrary", "arbitrary"), vmem_limit_bytes=VMEM_MIXER_LIMIT),
        name="mixer",
    )(*[d for d, _ in data], mod, *weights)


def _qproj_kernel(x_ref, mod_ref, nw_ref, w_ref, h_ref, q_ref):
    b = pl.program_id(0)
    mod = mod_ref[pl.ds(b, 1), :]
    sh2 = mod[:, 3 * D_MODEL:4 * D_MODEL]
    sc2 = mod[:, 4 * D_MODEL:5 * D_MODEL]
    h = _rms(x_ref[...], nw_ref[...]) * (1.0 + sc2) + sh2
    h_ref[...] = h
    q_ref[...] = _bdot(h, w_ref[...])


def _qproj(x1, mod, nw, wq, batch, seq, tm):
    nt = seq // tm
    row = lambda b, t: (b * nt + t, 0)
    const = lambda b, t: (0, 0)
    n = batch * seq
    nq = wq.shape[1]
    return pl.pallas_call(
        _qproj_kernel,
        grid=(batch, nt),
        in_specs=[
            pl.BlockSpec((tm, D_MODEL), row),
            pl.BlockSpec(mod.shape, const),
            pl.BlockSpec((1, D_MODEL), const),
            pl.BlockSpec(wq.shape, const),
        ],
        out_specs=[pl.BlockSpec((tm, D_MODEL), row), pl.BlockSpec((tm, nq), row)],
        out_shape=[jax.ShapeDtypeStruct((n, D_MODEL), F32), jax.ShapeDtypeStruct((n, nq), F32)],
        compiler_params=pltpu.CompilerParams(dimension_semantics=("arbitrary", "arbitrary")),
        name="qproj",
    )(x1, mod, nw, wq)


def _sort16_network():
    n, pairs, p = 16, [], 1
    while p < n:
        k = p
        while k >= 1:
            for j in range(k % p, n - k, 2 * k):
                for i in range(min(k, n - j - k)):
                    if (i + j) // (2 * p) == (i + j + k) // (2 * p):
                        pairs.append((i + j, i + j + k))
            k //= 2
        p *= 2
    return pairs


def _top16_of_128(st, v_out, i_out):
    k = PEER_TOPK
    vs = [st[SUBLANES * i:SUBLANES * (i + 1), :] for i in range(k)]
    sub = lax.broadcasted_iota(I32, vs[0].shape, 0).astype(F32)
    ix = [sub + float(SUBLANES * i) for i in range(k)]
    for a, b in _sort16_network():
        swap = (vs[b] > vs[a]) | ((vs[b] == vs[a]) & (ix[b] < ix[a]))
        vs[a], vs[b] = jnp.where(swap, vs[b], vs[a]), jnp.where(swap, vs[a], vs[b])
        ix[a], ix[b] = jnp.where(swap, ix[b], ix[a]), jnp.where(swap, ix[a], ix[b])
    for it in range(k):
        m = jnp.max(vs[0], axis=0, keepdims=True)
        pos = jnp.min(jnp.where(vs[0] == m, ix[0], float(PEER_NKEYS)), axis=0, keepdims=True)
        v_out[it:it + 1, :] = m
        i_out[it:it + 1, :] = pos
        hit = ix[0] == pos
        depth = k - 1 - it
        for d in range(depth):
            vs[d] = jnp.where(hit, vs[d + 1], vs[d])
            ix[d] = jnp.where(hit, ix[d + 1], ix[d])


def _top16_of_pairs(v1, i1, v2, i2, v_out, e_out, row0):
    k = PEER_TOPK
    rows = lax.broadcasted_iota(I32, v1.shape, 0).astype(F32)
    cand = v1 + v2[0:1, :]
    taken = jnp.zeros_like(v1)
    for it in range(k):
        m = jnp.max(cand, axis=0, keepdims=True)
        a_win = jnp.min(jnp.where(cand == m, rows, float(k)), axis=0, keepdims=True)
        hit = rows == a_win
        b_win = jnp.sum(jnp.where(hit, taken, 0.0), axis=0, keepdims=True)
        i1_win = jnp.sum(jnp.where(hit, i1, 0.0), axis=0, keepdims=True)
        i2_win = jnp.sum(jnp.where(rows == b_win, i2, 0.0), axis=0, keepdims=True)
        v_out[row0 + it:row0 + it + 1, :] = m
        e_out[row0 + it:row0 + it + 1, :] = i1_win * float(PEER_NKEYS) + i2_win
        if it + 1 < k:
            v1_win = jnp.sum(jnp.where(hit, v1, 0.0), axis=0, keepdims=True)
            nxt = rows == b_win + 1.0
            v2_nxt = jnp.sum(jnp.where(nxt, v2, 0.0), axis=0, keepdims=True)
            new = jnp.where(b_win + 1.0 < float(k), v1_win + v2_nxt, -jnp.inf)
            cand = jnp.where(hit, new, cand)
            taken = jnp.where(hit, b_win + 1.0, taken)


def _topk_kernel(q_ref, keys_ref, e_ref, g_ref, v1_s, i1_s, v2_s, i2_s, ts_s, ei_s, gate_s, *, tt):
    k = PEER_TOPK
    for h in range(PEER_HEADS):
        for p, (v_s, i_s) in enumerate(((v1_s, i1_s), (v2_s, i2_s))):
            col = (2 * h + p) * PEER_HALF
            st = _sdot(_split(keys_ref[2 * h + p], 2), _split(q_ref[:, col:col + PEER_HALF], 2), NT_DIMS)
            _top16_of_128(st, v_s, i_s)
        _top16_of_pairs(v1_s[...], i1_s[...], v2_s[...], i2_s[...], ts_s, ei_s, h * k)
        ts = ts_s[h * k:(h + 1) * k, :]
        e = jnp.exp(ts - jnp.max(ts, axis=0, keepdims=True))
        gate_s[h * k:(h + 1) * k, :] = e / jnp.sum(e, axis=0, keepdims=True)
    e_ref[...] = ei_s[...].T.astype(I32) * HALF_ROWS
    g_ref[...] = gate_s[...].T


def _topk(q, keys, n, tt):
    row = lambda i: (i, 0)
    return pl.pallas_call(
        functools.partial(_topk_kernel, tt=tt),
        grid=(n // tt,),
        in_specs=[
            pl.BlockSpec((tt, q.shape[1]), row),
            pl.BlockSpec(keys.shape, lambda i: (0, 0, 0)),
        ],
        out_specs=[pl.BlockSpec((tt, PEER_E), row)] * 2,
        out_shape=[jax.ShapeDtypeStruct((n, PEER_E), I32), jax.ShapeDtypeStruct((n, PEER_E), F32)],
        scratch_shapes=[
            pltpu.VMEM((PEER_TOPK, tt), F32), pltpu.VMEM((PEER_TOPK, tt), F32),
            pltpu.VMEM((PEER_TOPK, tt), F32), pltpu.VMEM((PEER_TOPK, tt), F32),
            pltpu.VMEM((PEER_E, tt), F32), pltpu.VMEM((PEER_E, tt), F32),
            pltpu.VMEM((PEER_E, tt), F32),
        ],
        compiler_params=pltpu.CompilerParams(dimension_semantics=("arbitrary",)),
        name="topk",
    )(q, keys)


def _staged_tokens(srcs, bufs, sems, tt, body, after_group=None):
    n_groups = tt // STAGE_TOKENS
    nb = STAGE_BUFS
    assert n_groups % nb == 0 and STAGE_TOKENS % 2 == 0

    def copies(grp, slot):
        return [pltpu.make_async_copy(src.at[pl.ds(grp * STAGE_TOKENS, STAGE_TOKENS)], buf.at[slot], sem.at[slot])
                for src, buf, sem in zip(srcs, bufs, sems)]

    for s in range(nb - 1):
        for c in copies(s, s):
            c.start()

    def round_of_groups(i, carry):
        for slot in range(nb):
            grp = nb * i + slot
            for c in copies(jnp.minimum(grp + nb - 1, n_groups - 1), (slot + nb - 1) % nb):
                c.start()
            for c in copies(grp, slot):
                c.wait()
            if after_group is not None:
                after_group(pl.multiple_of(jnp.maximum(grp - 1, 0) * STAGE_TOKENS, STAGE_TOKENS), (slot - 1) % nb)
            for tl in range(STAGE_TOKENS):
                body(grp * STAGE_TOKENS + tl, tl, slot,
                     *[lambda k, buf=buf, slot=slot, tl=tl: buf[slot, tl, k] for buf in bufs])
        return carry

    lax.fori_loop(0, n_groups // nb, round_of_groups, 0)
    for s in range(nb - 1):
        for c in copies(n_groups - 1, (n_groups + s) % nb):
            c.wait()
    if after_group is not None:
        after_group((n_groups - 1) * STAGE_TOKENS, (n_groups - 1) % nb)


def _stage_scratch(dtype):
    return [pltpu.SMEM((STAGE_BUFS, STAGE_TOKENS, PEER_E), dtype), pltpu.SemaphoreType.DMA((STAGE_BUFS,))]


def _pair_rows(tab_ref, idx, m):
    wa = tab_ref[pl.ds(pl.multiple_of(idx(2 * m), HALF_ROWS), HALF_ROWS), :]
    wb = tab_ref[pl.ds(pl.multiple_of(idx(2 * m + 1), HALF_ROWS), HALF_ROWS), :]
    return jnp.concatenate([wa, wb], axis=0)


def _unpack(w):
    return pltpu.bitcast(w << 16, F32), pltpu.bitcast(w & jnp.uint32(0xFFFF0000), F32)


def _peer_u_kernel(e_ref, h_ref, g_ref, tab_ref, o_ref, prod_a, prod_b, idx_s, sem, *, tt):
    prod = (prod_a, prod_b)

    def finish(t, buf):
        part = buf[pl.ds(0, PEER_E, stride=HALF_ROWS), :]
        for s in range(1, HALF_ROWS):
            part = part + buf[pl.ds(s, PEER_E, stride=HALF_ROWS), :]
        act = jnp.sum(part.T, axis=0, keepdims=True)
        o_ref[pl.ds(t, 1), :] = g_ref[pl.ds(t, 1), :] * _gelu(act)

    prod_b[...] = jnp.zeros(prod_b.shape, F32)

    def tok(t, tl, _, idx):
        finish(jnp.maximum(t - 1, 0), prod[(tl + 1) & 1])
        hv = h_ref[t]
        h_lo = jnp.concatenate([hv[0:HALF_ROWS], hv[0:HALF_ROWS]], axis=0)
        h_hi = jnp.concatenate([hv[HALF_ROWS:], hv[HALF_ROWS:]], axis=0)
        for m in range(PEER_E // 2):
            lo, hi = _unpack(_pair_rows(tab_ref, idx, m))
            prod[tl & 1][SUBLANES * m:SUBLANES * (m + 1), :] = lo * h_lo + hi * h_hi

    _staged_tokens([e_ref], [idx_s], [sem], tt, tok)
    finish(tt - 1, prod[(tt - 1) & 1])


def _peer_u(e4, h3, gates, tab, n, tt):
    row = lambda i: (i, 0)
    return pl.pallas_call(
        functools.partial(_peer_u_kernel, tt=tt),
        grid=(n // tt,),
        in_specs=[
            pl.BlockSpec((tt, PEER_E), row),
            pl.BlockSpec((tt, SUBLANES, LANES), lambda i: (i, 0, 0)),
            pl.BlockSpec((tt, PEER_E), row),
            pl.BlockSpec(tab.shape, lambda i: (0, 0), pipeline_mode=pl.Buffered(1)),
        ],
        out_specs=pl.BlockSpec((tt, PEER_E), row),
        out_shape=jax.ShapeDtypeStruct((n, PEER_E), F32),
        scratch_shapes=[pltpu.VMEM((PEER_E * HALF_ROWS, LANES), F32) for _ in range(2)] + _stage_scratch(I32),
        compiler_params=pltpu.CompilerParams(
            dimension_semantics=("arbitrary",), vmem_limit_bytes=VMEM_TABLE_LIMIT),
        name="peer_u",
    )(e4, h3, gates, tab)


def _peer_v_kernel(e_ref, a_ref, x_ref, mod_ref, nw_ref, tab_ref, o_ref, yg_a, yg_b, idx_s, sem_i, wgt_s, sem_w,
                   *, tt, seq):
    assert STAGE_TOKENS == SUBLANES and STAGE_BUFS == 2
    yg = (yg_a, yg_b)
    top = lax.broadcasted_iota(I32, (SUBLANES, LANES), 0) < HALF_ROWS
    b = (pl.program_id(0) * tt) // seq
    gt2 = mod_ref[pl.ds(b, 1), :][:, 5 * D_MODEL:6 * D_MODEL]

    def tok(t, tl, slot, idx, wgt):
        acc_lo = [jnp.zeros((SUBLANES, LANES), F32) for _ in range(2)]
        acc_hi = [jnp.zeros((SUBLANES, LANES), F32) for _ in range(2)]
        for m in range(PEER_E // 2):
            lo, hi = _unpack(_pair_rows(tab_ref, idx, m))
            w = jnp.where(top, wgt(2 * m), wgt(2 * m + 1))
            acc_lo[m % 2] = acc_lo[m % 2] + w * lo
            acc_hi[m % 2] = acc_hi[m % 2] + w * hi
        lo = acc_lo[0] + acc_lo[1]
        hi = acc_hi[0] + acc_hi[1]
        yg[slot][tl, 0:HALF_ROWS, :] = lo[0:HALF_ROWS] + lo[HALF_ROWS:]
        yg[slot][tl, HALF_ROWS:, :] = hi[0:HALF_ROWS] + hi[HALF_ROWS:]

    def residual(t0, slot):
        y = jnp.concatenate([yg[slot][:, s, :] for s in range(SUBLANES)], axis=1)
        rows = pl.ds(t0, SUBLANES)
        o_ref[rows, :] = x_ref[rows, :] + gt2 * _rms(y, nw_ref[...])

    yg[STAGE_BUFS - 1][...] = jnp.zeros(yg[STAGE_BUFS - 1].shape, F32)
    _staged_tokens([e_ref, a_ref], [idx_s, wgt_s], [sem_i, sem_w], tt, tok, after_group=residual)


def _peer_v(e4, act, x1, mod, nw, tab, n, seq, tt):
    row = lambda i: (i, 0)
    const = lambda i: (0, 0)
    assert seq % tt == 0
    return pl.pallas_call(
        functools.partial(_peer_v_kernel, tt=tt, seq=seq),
        grid=(n // tt,),
        in_specs=[
            pl.BlockSpec((tt, PEER_E), row),
            pl.BlockSpec((tt, PEER_E), row),
            pl.BlockSpec((tt, D_MODEL), row),
            pl.BlockSpec(mod.shape, const),
            pl.BlockSpec((1, D_MODEL), const),
            pl.BlockSpec(tab.shape, const, pipeline_mode=pl.Buffered(1)),
        ],
        out_specs=pl.BlockSpec((tt, D_MODEL), row),
        out_shape=jax.ShapeDtypeStruct((n, D_MODEL), F32),
        scratch_shapes=[pltpu.VMEM((STAGE_TOKENS, SUBLANES, LANES), F32) for _ in range(STAGE_BUFS)]
        + _stage_scratch(I32) + _stage_scratch(F32),
        compiler_params=pltpu.CompilerParams(
            dimension_semantics=("arbitrary",), vmem_limit_bytes=VMEM_TABLE_LIMIT),
        name="peer_v",
    )(e4, act, x1, mod, nw, tab)


def _pack_kernel(x_ref, o_ref):
    half = D_MODEL // 2
    bits = pltpu.bitcast(x_ref[...].astype(BF16).astype(F32), U32)
    w = (bits[:, :half] >> 16) | bits[:, half:]
    for s in range(HALF_ROWS):
        o_ref[pl.ds(s, x_ref.shape[0], stride=HALF_ROWS), :] = w[:, s * LANES:(s + 1) * LANES]


def _pack_table(tab):
    e, d = tab.shape
    te = 2048
    assert d == D_MODEL and e % te == 0
    return pl.pallas_call(
        _pack_kernel,
        grid=(e // te,),
        in_specs=[pl.BlockSpec((te, d), lambda i: (i, 0))],
        out_specs=pl.BlockSpec((te * HALF_ROWS, LANES), lambda i: (i, 0)),
        out_shape=jax.ShapeDtypeStruct((e * HALF_ROWS, LANES), U32),
        compiler_params=pltpu.CompilerParams(dimension_semantics=("arbitrary",), vmem_limit_bytes=VMEM_MIXER_LIMIT),
        name="pack",
    )(tab)


def _tile(seq, cap):
    t = min(seq, cap)
    assert seq % t == 0
    return t


def kernel(x, c, w_ada, b_ada, norm_pre_mix, norm_post_mix, w_in_mix, sc_conv_w, dn_conv_w, dn_a_log,
           dn_dt_bias, dn_out_norm, w_sc_out, w_dn_out, w_mix_out, norm_pre_ffn, norm_post_ffn, peer_w_q,
           peer_sub_keys, peer_u, peer_v):
    batch, seq, d = x.shape
    assert d == D_MODEL and batch <= SUBLANES and seq % DN_CHUNK == 0
    depth = w_ada.shape[0]
    n = batch * seq
    x2 = x.reshape(n, d)
    c_pad = jnp.pad(c, ((0, SUBLANES - batch), (0, 0)))
    tab_u = _pack_table(peer_u)
    tab_v = _pack_table(peer_v)
    rep = lambda v: jnp.repeat(v, DN_DK, axis=-1)
    for l in range(depth):
        mod = _ada(c_pad, w_ada[l], b_ada[l][None, :])
        w = w_in_mix[l]
        o_z = C_SC + C_QKV
        o_a = o_z + C_Z
        w_pad = jnp.concatenate(
            [w[:, :o_a], rep(w[:, o_a:o_a + DN_HEADS]), rep(w[:, o_a + DN_HEADS:o_a + 2 * DN_HEADS]),
             w[:, o_a + 2 * DN_HEADS:]], axis=1).astype(BF16)
        tm = _tile(seq, 512)
        parts = _inproj(x2, mod, norm_pre_mix[l][None, :], w_pad, batch, seq, tm)
        weights = [
            sc_conv_w[l], dn_conv_w[l], rep(dn_a_log[l])[None, :], rep(dn_dt_bias[l])[None, :],
            jnp.tile(dn_out_norm[l], DN_HEADS)[None, :], w_sc_out[l].astype(BF16), w_dn_out[l].astype(BF16),
            w_mix_out[l].astype(BF16), norm_post_mix[l][None, :],
        ]
        x1 = _mixer(parts, x2, mod, weights, batch, seq, _tile(seq, 512))
        h2, q = _qproj(x1, mod, norm_pre_ffn[l][None, :], peer_w_q[l].astype(BF16), batch, seq, tm)
        keys = peer_sub_keys[l].reshape(2 * PEER_HEADS, PEER_NKEYS, PEER_HALF)
        e4, gates = _topk(q, keys, n, _tile(n, LANES))
        tt = _tile(seq, 128)
        act = _peer_u(e4, h2.reshape(n, SUBLANES, LANES), gates, tab_u, n, tt)
        x2 = _peer_v(e4, act, x1, mod, norm_post_ffn[l][None, :], tab_v, n, seq, tt)
    return x2.reshape(batch, seq, d)
```

```python
import functools

import jax
import jax.numpy as jnp
from jax import lax
from jax.experimental import pallas as pl
from jax.experimental.pallas import tpu as pltpu

F32 = jnp.float32
BF16 = jnp.bfloat16
I32 = jnp.int32
U32 = jnp.uint32

D_MODEL = 1024
SC_WIDTH = 512
SC_KERNEL = 3
DN_HEADS = 8
DN_DK = 64
DN_KEY = DN_HEADS * DN_DK
DN_VAL = DN_KEY
DN_CONV = 4
DN_CHUNK = 64
PEER_HEADS = 8
PEER_NKEYS = 128
PEER_HALF = 128
PEER_TOPK = 16
PEER_E = PEER_HEADS * PEER_TOPK
NORM_EPS = 1e-6

LANES = 128
SUBLANES = 8
HALO = SUBLANES
PAIR = 2 * DN_DK
N_PAIRS = DN_HEADS // 2
INV_TERMS = 2
PREP_CHUNKS = 2
HALF_ROWS = SUBLANES // 2
STAGE_TOKENS = 8
STAGE_BUFS = 2
VMEM_TABLE_LIMIT = 56 * 1024 * 1024
VMEM_MIXER_LIMIT = 56 * 1024 * 1024

C_SC = 3 * SC_WIDTH
C_QKV = 2 * DN_KEY + DN_VAL
C_Z = DN_VAL
C_A = DN_KEY
C_B = DN_KEY
C_G = 2 * D_MODEL
C_ALL = C_SC + C_QKV + C_Z + C_A + C_B + C_G


def _silu(x):
    return x * jax.nn.sigmoid(x)


def _softplus(x):
    return jnp.maximum(x, 0.0) + jnp.log(1.0 + jnp.exp(-jnp.abs(x)))


def _gelu(x):
    return 0.5 * x * (1.0 + lax.erf(x * (2.0 ** -0.5)))


def _rms(x, w):
    return x * lax.rsqrt(jnp.mean(x * x, axis=-1, keepdims=True) + NORM_EPS) * w


def _bdot(a, b):
    return jnp.dot(a.astype(BF16), b.astype(BF16), preferred_element_type=F32)


def _split(x, n):
    terms = []
    for i in range(n):
        t = x.astype(BF16)
        terms.append(t)
        if i + 1 < n:
            x = x - t.astype(F32)
    return terms


NN_DIMS = (((1,), (0,)), ((), ()))
NT_DIMS = (((1,), (1,)), ((), ()))


def _sdot(a_terms, b_terms, dims=NN_DIMS):
    order = max(len(a_terms), len(b_terms))
    out = None
    for i, a in enumerate(a_terms):
        for j, b in enumerate(b_terms):
            if i + j < order:
                p = lax.dot_general(a, b, dims, preferred_element_type=F32)
                out = p if out is None else out + p
    return out


def _ada_kernel(c_ref, w_ref, b_ref, o_ref):
    o_ref[...] = _bdot(_silu(c_ref[...]), w_ref[...]) + b_ref[...]


def _ada(c_pad, w_ada, b_ada):
    n_out = w_ada.shape[1]
    tn = 1024
    return pl.pallas_call(
        _ada_kernel,
        grid=(n_out // tn,),
        in_specs=[
            pl.BlockSpec((SUBLANES, D_MODEL), lambda j: (0, 0)),
            pl.BlockSpec((D_MODEL, tn), lambda j: (0, j)),
            pl.BlockSpec((1, tn), lambda j: (0, j)),
        ],
        out_specs=pl.BlockSpec((SUBLANES, tn), lambda j: (0, j)),
        out_shape=jax.ShapeDtypeStruct((SUBLANES, n_out), F32),
        name="ada",
    )(c_pad, w_ada, b_ada)


def _inproj_kernel(x_ref, mod_ref, nw_ref, w_ref, sc_ref, qkv_ref, z_ref, a_ref, b_ref, g_ref):
    b = pl.program_id(0)
    mod = mod_ref[pl.ds(b, 1), :]
    sh1 = mod[:, 0:D_MODEL]
    sc1 = mod[:, D_MODEL:2 * D_MODEL]
    h = (_rms(x_ref[...], nw_ref[...]) * (1.0 + sc1) + sh1).astype(BF16)
    col = 0
    for ref, width in ((sc_ref, C_SC), (qkv_ref, C_QKV), (z_ref, C_Z), (a_ref, C_A), (b_ref, C_B), (g_ref, C_G)):
        for j in range(0, width, 512):
            ref[:, j:j + 512] = jnp.dot(h, w_ref[:, col + j:col + j + 512],
                                        preferred_element_type=F32).astype(ref.dtype)
        col += width


def _inproj(x2, mod, nw, w_pad, batch, seq, tm):
    nt = seq // tm
    row = lambda b, t: (b * nt + t, 0)
    const = lambda b, t: (0, 0)
    n = batch * seq
    outs = [(C_SC, BF16), (C_QKV, BF16), (C_Z, BF16), (C_A, F32), (C_B, F32), (C_G, BF16)]
    return pl.pallas_call(
        _inproj_kernel,
        grid=(batch, nt),
        in_specs=[
            pl.BlockSpec((tm, D_MODEL), row),
            pl.BlockSpec(mod.shape, const),
            pl.BlockSpec((1, D_MODEL), const),
            pl.BlockSpec((D_MODEL, C_ALL), const, pipeline_mode=pl.Buffered(1)),
        ],
        out_specs=[pl.BlockSpec((tm, w), row) for w, _ in outs],
        out_shape=[jax.ShapeDtypeStruct((n, w), dt) for w, dt in outs],
        compiler_params=pltpu.CompilerParams(
            dimension_semantics=("arbitrary", "arbitrary"), vmem_limit_bytes=VMEM_MIXER_LIMIT),
        name="inproj",
    )(x2, mod, nw, w_pad)


def _pair_consts():
    r = lax.broadcasted_iota(I32, (PAIR, PAIR), 0)
    c = lax.broadcasted_iota(I32, (PAIR, PAIR), 1)
    bd = ((r >> 6) == (c >> 6)).astype(F32)
    i = lax.broadcasted_iota(I32, (DN_CHUNK, PAIR), 0)
    j = lax.broadcasted_iota(I32, (DN_CHUNK, PAIR), 1) & (DN_DK - 1)
    return bd, i >= j, i > j, (i == j).astype(F32)


def _stack_bd(y, bd):
    return jnp.concatenate([y, y], axis=0) * bd


def _mixer_kernel(sc_ref, qkv_ref, z_ref, a_ref, b_ref, g_ref, x_ref, mod_ref,
                  scw_ref, dnw_ref, alog_ref, dtb_ref, onw_ref, wsc_ref, wdn_ref, wmix_ref, pnw_ref,
                  o_ref,
                  scx_s, qkvx_s, q_s, k_s, v_s, g_s, beta_s, o_s, gcum_s, aqk_s, u_s, wk_s, qg_s, kdec_s,
                  state_s, *, tt):
    b = pl.program_id(0)
    t = pl.program_id(1)

    @pl.when(t == 0)
    def _():
        scx_s[0:HALO, :] = jnp.zeros((HALO, SC_WIDTH), F32)
        qkvx_s[0:HALO, :] = jnp.zeros((HALO, C_QKV), F32)
        state_s[...] = jnp.zeros(state_s.shape, F32)

    sc = sc_ref[...].astype(F32)
    scx_s[HALO:HALO + tt, :] = sc[:, SC_WIDTH:2 * SC_WIDTH] * sc[:, 2 * SC_WIDTH:]
    conv = scw_ref[0:1, :] * scx_s[pl.ds(HALO - (SC_KERNEL - 1), tt), :]
    for kk in range(1, SC_KERNEL):
        conv = conv + scw_ref[kk:kk + 1, :] * scx_s[pl.ds(HALO - (SC_KERNEL - 1) + kk, tt), :]
    y_sc = _bdot(sc[:, 0:SC_WIDTH] * conv, wsc_ref[...])
    scx_s[0:HALO, :] = scx_s[tt:tt + HALO, :]

    qkvx_s[HALO:HALO + tt, :] = qkv_ref[...].astype(F32)
    cq = dnw_ref[0:1, :] * qkvx_s[pl.ds(HALO - (DN_CONV - 1), tt), :]
    for kk in range(1, DN_CONV):
        cq = cq + dnw_ref[kk:kk + 1, :] * qkvx_s[pl.ds(HALO - (DN_CONV - 1) + kk, tt), :]
    qkvx_s[0:HALO, :] = qkvx_s[tt:tt + HALO, :]
    cq = _silu(cq)
    r512 = lax.broadcasted_iota(I32, (DN_KEY, DN_KEY), 0) >> 6
    c512 = lax.broadcasted_iota(I32, (DN_KEY, DN_KEY), 1) >> 6
    head_ones = [(r512 == c512).astype(BF16)]
    q = cq[:, 0:DN_KEY]
    k = cq[:, DN_KEY:2 * DN_KEY]
    q_s[...] = q * lax.rsqrt(_sdot(_split(q * q, 2), head_ones) + NORM_EPS) * (DN_DK ** -0.5)
    k_s[...] = k * lax.rsqrt(_sdot(_split(k * k, 2), head_ones) + NORM_EPS)
    v_s[...] = cq[:, 2 * DN_KEY:]
    beta_s[...] = jax.nn.sigmoid(b_ref[...])
    g_s[...] = -jnp.exp(alog_ref[...]) * _softplus(a_ref[...] + dtb_ref[...])

    bd, incl, strict, eye = _pair_consts()
    bd_b = bd.astype(BF16)
    ri = lax.broadcasted_iota(I32, (DN_CHUNK, DN_CHUNK), 0)
    ci = lax.broadcasted_iota(I32, (DN_CHUNK, DN_CHUNK), 1)
    lt = [(ci <= ri).astype(BF16)]
    nt = (((1,), (1,)), ((), ()))

    def stack_terms(y):
        return [jnp.concatenate([t_, t_], axis=0) * bd_b for t_ in _split(y, INV_TERMS)]

    pairs = [slice(p * PAIR, (p + 1) * PAIR) for p in range(N_PAIRS)]

    def prepare(c, carry):
        prob = []
        for cc in range(PREP_CHUNKS):
            rows = pl.ds(pl.multiple_of((c * PREP_CHUNKS + cc) * DN_CHUNK, DN_CHUNK), DN_CHUNK)
            gcum_all = _sdot(lt, _split(g_s[rows, :], 3))
            gcum_s[rows, :] = gcum_all
            prob += [(rows, sl, gcum_all[:, sl]) for sl in pairs]
        pws, invs, rest = [], [], []
        for rows, sl, gc in prob:
            qp = q_s[rows, sl]
            kp = k_s[rows, sl]
            bp = beta_s[rows, sl]
            grow = jnp.sum(gc * eye, axis=0, keepdims=True)
            dec = jnp.where(incl, jnp.exp(jnp.where(incl, gc - grow, 0.0)), 0.0)
            eg = jnp.exp(gc)
            kb = kp * bp
            kbig = _stack_bd(kp, bd).astype(BF16)
            kk_s = lax.dot_general(kb.astype(BF16), kbig, nt, preferred_element_type=F32)
            qk_s = lax.dot_general(qp.astype(BF16), kbig, nt, preferred_element_type=F32)
            aqk_s[rows, sl] = qk_s * dec
            qg_s[rows, sl] = qp * eg
            kdec_s[rows, sl] = kp * jnp.exp(gc[DN_CHUNK - 1:DN_CHUNK, :] - gc)
            pw = -(kk_s * jnp.where(strict, dec, 0.0))
            pws.append(pw)
            invs.append(eye + pw)
            rest.append((v_s[rows, sl] * bp, kb * eg))
        for _ in range(5):
            pws = [_sdot(_split(pw, INV_TERMS), stack_terms(pw)) for pw in pws]
            invs = [inv + _sdot(_split(inv, INV_TERMS), stack_terms(pw)) for inv, pw in zip(invs, pws)]
        for (rows, sl, _), inv, (vb, kbg) in zip(prob, invs, rest):
            u_s[rows, sl] = _bdot(inv, _stack_bd(vb, bd))
            wk_s[rows, sl] = _bdot(inv, _stack_bd(kbg, bd))
        return carry

    def advance(c, carry):
        rows = pl.ds(pl.multiple_of(c * DN_CHUNK, DN_CHUNK), DN_CHUNK)
        tail = pl.ds(pl.multiple_of(c * DN_CHUNK + DN_CHUNK - SUBLANES, SUBLANES), SUBLANES)
        sts = [state_s[p] for p in range(N_PAIRS)]
        wss = [_bdot(jnp.concatenate([wk_s[rows, sl], qg_s[rows, sl]], axis=0), st) for sl, st in zip(pairs, sts)]
        vnews = [u_s[rows, sl] - ws[0:DN_CHUNK] for sl, ws in zip(pairs, wss)]
        for p, (sl, st, ws, vnew) in enumerate(zip(pairs, sts, wss, vnews)):
            o_s[rows, sl] = ws[DN_CHUNK:] + _bdot(aqk_s[rows, sl], _stack_bd(vnew, bd))
            glast = gcum_s[tail, sl][SUBLANES - 1:SUBLANES, :]
            state_s[p] = st * jnp.exp(glast) + bd * _bdot(kdec_s[rows, sl].T, vnew)
        return carry

    lax.fori_loop(0, tt // (DN_CHUNK * PREP_CHUNKS), prepare, 0)
    lax.fori_loop(0, tt // DN_CHUNK, advance, 0)

    o = o_s[...]
    ms = _sdot(_split(o * o, 2), head_ones) * (1.0 / DN_DK)
    og = o * lax.rsqrt(ms + NORM_EPS) * onw_ref[...] * _silu(z_ref[...].astype(F32))
    y_dn = _bdot(og, wdn_ref[...])
    gates = g_ref[...].astype(F32)
    merged = jax.nn.sigmoid(gates[:, 0:D_MODEL]) * y_sc + jax.nn.sigmoid(gates[:, D_MODEL:]) * y_dn
    y = _bdot(merged, wmix_ref[...])
    gt1 = mod_ref[pl.ds(b, 1), :][:, 2 * D_MODEL:3 * D_MODEL]
    o_ref[...] = x_ref[...] + gt1 * _rms(y, pnw_ref[...])


def _mixer(parts, x2, mod, weights, batch, seq, tt):
    nt = seq // tt
    row = lambda b, t: (b * nt + t, 0)
    const2 = lambda b, t: (0, 0)
    sc, qkv, z, a, bb, g = parts
    data = [(sc, C_SC), (qkv, C_QKV), (z, C_Z), (a, C_A), (bb, C_B), (g, C_G), (x2, D_MODEL)]
    in_specs = [pl.BlockSpec((tt, w), row) for _, w in data]
    in_specs.append(pl.BlockSpec(mod.shape, const2))
    in_specs += [pl.BlockSpec(wt.shape, const2) for wt in weights]
    scratch = [
        pltpu.VMEM((tt + HALO, SC_WIDTH), F32),
        pltpu.VMEM((tt + HALO, C_QKV), F32),
        pltpu.VMEM((tt, DN_KEY), F32),
        pltpu.VMEM((tt, DN_KEY), F32),
        pltpu.VMEM((tt, DN_VAL), F32),
        pltpu.VMEM((tt, DN_KEY), F32),
        pltpu.VMEM((tt, DN_KEY), F32),
        pltpu.VMEM((tt, DN_VAL), F32),
    ] + [pltpu.VMEM((tt, DN_KEY), F32) for _ in range(6)] + [
        pltpu.VMEM((N_PAIRS, PAIR, PAIR), F32),
    ]
    return pl.pallas_call(
        functools.partial(_mixer_kernel, tt=tt),
        grid=(batch, nt),
        in_specs=in_specs,
        out_specs=pl.BlockSpec((tt, D_MODEL), row),
        out_shape=jax.ShapeDtypeStruct((batch * seq, D_MODEL), F32),
        scratch_shapes=scratch,
        compiler_params=pltpu.CompilerParams(
            dimension_semantics=("arbitrary", "arbitrary"), vmem_limit_bytes=VMEM_MIXER_LIMIT),
        name="mixer",
    )(*[d for d, _ in data], mod, *weights)


def _qproj_kernel(x_ref, mod_ref, nw_ref, w_ref, h_ref, q_ref):
    b = pl.program_id(0)
    mod = mod_ref[pl.ds(b, 1), :]
    sh2 = mod[:, 3 * D_MODEL:4 * D_MODEL]
    sc2 = mod[:, 4 * D_MODEL:5 * D_MODEL]
    h = _rms(x_ref[...], nw_ref[...]) * (1.0 + sc2) + sh2
    h_ref[...] = h
    q_ref[...] = _bdot(h, w_ref[...])


def _qproj(x1, mod, nw, wq, batch, seq, tm):
    nt = seq // tm
    row = lambda b, t: (b * nt + t, 0)
    const = lambda b, t: (0, 0)
    n = batch * seq
    nq = wq.shape[1]
    return pl.pallas_call(
        _qproj_kernel,
        grid=(batch, nt),
        in_specs=[
            pl.BlockSpec((tm, D_MODEL), row),
            pl.BlockSpec(mod.shape, const),
            pl.BlockSpec((1, D_MODEL), const),
            pl.BlockSpec(wq.shape, const),
        ],
        out_specs=[pl.BlockSpec((tm, D_MODEL), row), pl.BlockSpec((tm, nq), row)],
        out_shape=[jax.ShapeDtypeStruct((n, D_MODEL), F32), jax.ShapeDtypeStruct((n, nq), F32)],
        compiler_params=pltpu.CompilerParams(dimension_semantics=("arbitrary", "arbitrary")),
        name="qproj",
    )(x1, mod, nw, wq)


def _sort16_network():
    n, pairs, p = 16, [], 1
    while p < n:
        k = p
        while k >= 1:
            for j in range(k % p, n - k, 2 * k):
                for i in range(min(k, n - j - k)):
                    if (i + j) // (2 * p) == (i + j + k) // (2 * p):
                        pairs.append((i + j, i + j + k))
            k //= 2
        p *= 2
    return pairs


def _top16_of_128(st, v_out, i_out):
    k = PEER_TOPK
    vs = [st[SUBLANES * i:SUBLANES * (i + 1), :] for i in range(k)]
    sub = lax.broadcasted_iota(I32, vs[0].shape, 0).astype(F32)
    ix = [sub + float(SUBLANES * i) for i in range(k)]
    for a, b in _sort16_network():
        swap = (vs[b] > vs[a]) | ((vs[b] == vs[a]) & (ix[b] < ix[a]))
        vs[a], vs[b] = jnp.where(swap, vs[b], vs[a]), jnp.where(swap, vs[a], vs[b])
        ix[a], ix[b] = jnp.where(swap, ix[b], ix[a]), jnp.where(swap, ix[a], ix[b])
    for it in range(k):
        m = jnp.max(vs[0], axis=0, keepdims=True)
        pos = jnp.min(jnp.where(vs[0] == m, ix[0], float(PEER_NKEYS)), axis=0, keepdims=True)
        v_out[it:it + 1, :] = m
        i_out[it:it + 1, :] = pos
        hit = ix[0] == pos
        depth = k - 1 - it
        for d in range(depth):
            vs[d] = jnp.where(hit, vs[d + 1], vs[d])
            ix[d] = jnp.where(hit, ix[d + 1], ix[d])


def _top16_of_pairs(v1, i1, v2, i2, v_out, e_out, row0):
    k = PEER_TOPK
    rows = lax.broadcasted_iota(I32, v1.shape, 0).astype(F32)
    cand = v1 + v2[0:1, :]
    taken = jnp.zeros_like(v1)
    for it in range(k):
        m = jnp.max(cand, axis=0, keepdims=True)
        a_win = jnp.min(jnp.where(cand == m, rows, float(k)), axis=0, keepdims=True)
        hit = rows == a_win
        b_win = jnp.sum(jnp.where(hit, taken, 0.0), axis=0, keepdims=True)
        i1_win = jnp.sum(jnp.where(hit, i1, 0.0), axis=0, keepdims=True)
        i2_win = jnp.sum(jnp.where(rows == b_win, i2, 0.0), axis=0, keepdims=True)
        v_out[row0 + it:row0 + it + 1, :] = m
        e_out[row0 + it:row0 + it + 1, :] = i1_win * float(PEER_NKEYS) + i2_win
        if it + 1 < k:
            v1_win = jnp.sum(jnp.where(hit, v1, 0.0), axis=0, keepdims=True)
            nxt = rows == b_win + 1.0
            v2_nxt = jnp.sum(jnp.where(nxt, v2, 0.0), axis=0, keepdims=True)
            new = jnp.where(b_win + 1.0 < float(k), v1_win + v2_nxt, -jnp.inf)
            cand = jnp.where(hit, new, cand)
            taken = jnp.where(hit, b_win + 1.0, taken)


def _topk_kernel(q_ref, keys_ref, e_ref, g_ref, v1_s, i1_s, v2_s, i2_s, ts_s, ei_s, gate_s, *, tt):
    k = PEER_TOPK
    for h in range(PEER_HEADS):
        for p, (v_s, i_s) in enumerate(((v1_s, i1_s), (v2_s, i2_s))):
            col = (2 * h + p) * PEER_HALF
            st = _sdot(_split(keys_ref[2 * h + p], 2), _split(q_ref[:, col:col + PEER_HALF], 2), NT_DIMS)
            _top16_of_128(st, v_s, i_s)
        _top16_of_pairs(v1_s[...], i1_s[...], v2_s[...], i2_s[...], ts_s, ei_s, h * k)
        ts = ts_s[h * k:(h + 1) * k, :]
        e = jnp.exp(ts - jnp.max(ts, axis=0, keepdims=True))
        gate_s[h * k:(h + 1) * k, :] = e / jnp.sum(e, axis=0, keepdims=True)
    e_ref[...] = ei_s[...].T.astype(I32) * HALF_ROWS
    g_ref[...] = gate_s[...].T


def _topk(q, keys, n, tt):
    row = lambda i: (i, 0)
    return pl.pallas_call(
        functools.partial(_topk_kernel, tt=tt),
        grid=(n // tt,),
        in_specs=[
            pl.BlockSpec((tt, q.shape[1]), row),
            pl.BlockSpec(keys.shape, lambda i: (0, 0, 0)),
        ],
        out_specs=[pl.BlockSpec((tt, PEER_E), row)] * 2,
        out_shape=[jax.ShapeDtypeStruct((n, PEER_E), I32), jax.ShapeDtypeStruct((n, PEER_E), F32)],
        scratch_shapes=[
            pltpu.VMEM((PEER_TOPK, tt), F32), pltpu.VMEM((PEER_TOPK, tt), F32),
            pltpu.VMEM((PEER_TOPK, tt), F32), pltpu.VMEM((PEER_TOPK, tt), F32),
            pltpu.VMEM((PEER_E, tt), F32), pltpu.VMEM((PEER_E, tt), F32),
            pltpu.VMEM((PEER_E, tt), F32),
        ],
        compiler_params=pltpu.CompilerParams(dimension_semantics=("arbitrary",)),
        name="topk",
    )(q, keys)


def _staged_tokens(srcs, bufs, sems, tt, body, after_group=None):
    n_groups = tt // STAGE_TOKENS
    nb = STAGE_BUFS
    assert n_groups % nb == 0 and STAGE_TOKENS % 2 == 0

    def copies(grp, slot):
        return [pltpu.make_async_copy(src.at[pl.ds(grp * STAGE_TOKENS, STAGE_TOKENS)], buf.at[slot], sem.at[slot])
                for src, buf, sem in zip(srcs, bufs, sems)]

    for s in range(nb - 1):
        for c in copies(s, s):
            c.start()

    def round_of_groups(i, carry):
        for slot in range(nb):
            grp = nb * i + slot
            for c in copies(jnp.minimum(grp + nb - 1, n_groups - 1), (slot + nb - 1) % nb):
                c.start()
            for c in copies(grp, slot):
                c.wait()
            if after_group is not None:
                after_group(pl.multiple_of(jnp.maximum(grp - 1, 0) * STAGE_TOKENS, STAGE_TOKENS), (slot - 1) % nb)
            for tl in range(STAGE_TOKENS):
                body(grp * STAGE_TOKENS + tl, tl, slot,
                     *[lambda k, buf=buf, slot=slot, tl=tl: buf[slot, tl, k] for buf in bufs])
        return carry

    lax.fori_loop(0, n_groups // nb, round_of_groups, 0)
    for s in range(nb - 1):
        for c in copies(n_groups - 1, (n_groups + s) % nb):
            c.wait()
    if after_group is not None:
        after_group((n_groups - 1) * STAGE_TOKENS, (n_groups - 1) % nb)


def _stage_scratch(dtype):
    return [pltpu.SMEM((STAGE_BUFS, STAGE_TOKENS, PEER_E), dtype), pltpu.SemaphoreType.DMA((STAGE_BUFS,))]


def _pair_rows(tab_ref, idx, m):
    wa = tab_ref[pl.ds(pl.multiple_of(idx(2 * m), HALF_ROWS), HALF_ROWS), :]
    wb = tab_ref[pl.ds(pl.multiple_of(idx(2 * m + 1), HALF_ROWS), HALF_ROWS), :]
    return jnp.concatenate([wa, wb], axis=0)


def _unpack(w):
    return pltpu.bitcast(w << 16, F32), pltpu.bitcast(w & jnp.uint32(0xFFFF0000), F32)


def _peer_u_kernel(e_ref, h_ref, g_ref, tab_ref, o_ref, prod_a, prod_b, idx_s, sem, *, tt):
    prod = (prod_a, prod_b)

    def finish(t, buf):
        part = buf[pl.ds(0, PEER_E, stride=HALF_ROWS), :]
        for s in range(1, HALF_ROWS):
            part = part + buf[pl.ds(s, PEER_E, stride=HALF_ROWS), :]
        act = jnp.sum(part.T, axis=0, keepdims=True)
        o_ref[pl.ds(t, 1), :] = g_ref[pl.ds(t, 1), :] * _gelu(act)

    prod_b[...] = jnp.zeros(prod_b.shape, F32)

    def tok(t, tl, _, idx):
        finish(jnp.maximum(t - 1, 0), prod[(tl + 1) & 1])
        hv = h_ref[t]
        h_lo = jnp.concatenate([hv[0:HALF_ROWS], hv[0:HALF_ROWS]], axis=0)
        h_hi = jnp.concatenate([hv[HALF_ROWS:], hv[HALF_ROWS:]], axis=0)
        for m in range(PEER_E // 2):
            lo, hi = _unpack(_pair_rows(tab_ref, idx, m))
            prod[tl & 1][SUBLANES * m:SUBLANES * (m + 1), :] = lo * h_lo + hi * h_hi

    _staged_tokens([e_ref], [idx_s], [sem], tt, tok)
    finish(tt - 1, prod[(tt - 1) & 1])


def _peer_u(e4, h3, gates, tab, n, tt):
    row = lambda i: (i, 0)
    return pl.pallas_call(
        functools.partial(_peer_u_kernel, tt=tt),
        grid=(n // tt,),
        in_specs=[
            pl.BlockSpec((tt, PEER_E), row),
            pl.BlockSpec((tt, SUBLANES, LANES), lambda i: (i, 0, 0)),
            pl.BlockSpec((tt, PEER_E), row),
            pl.BlockSpec(tab.shape, lambda i: (0, 0), pipeline_mode=pl.Buffered(1)),
        ],
        out_specs=pl.BlockSpec((tt, PEER_E), row),
        out_shape=jax.ShapeDtypeStruct((n, PEER_E), F32),
        scratch_shapes=[pltpu.VMEM((PEER_E * HALF_ROWS, LANES), F32) for _ in range(2)] + _stage_scratch(I32),
        compiler_params=pltpu.CompilerParams(
            dimension_semantics=("arbitrary",), vmem_limit_bytes=VMEM_TABLE_LIMIT),
        name="peer_u",
    )(e4, h3, gates, tab)


def _peer_v_kernel(e_ref, a_ref, x_ref, mod_ref, nw_ref, tab_ref, o_ref, yg_a, yg_b, idx_s, sem_i, wgt_s, sem_w,
                   *, tt, seq):
    assert STAGE_TOKENS == SUBLANES and STAGE_BUFS == 2
    yg = (yg_a, yg_b)
    top = lax.broadcasted_iota(I32, (SUBLANES, LANES), 0) < HALF_ROWS
    b = (pl.program_id(0) * tt) // seq
    gt2 = mod_ref[pl.ds(b, 1), :][:, 5 * D_MODEL:6 * D_MODEL]

    def tok(t, tl, slot, idx, wgt):
        acc_lo = [jnp.zeros((SUBLANES, LANES), F32) for _ in range(2)]
        acc_hi = [jnp.zeros((SUBLANES, LANES), F32) for _ in range(2)]
        for m in range(PEER_E // 2):
            lo, hi = _unpack(_pair_rows(tab_ref, idx, m))
            w = jnp.where(top, wgt(2 * m), wgt(2 * m + 1))
            acc_lo[m % 2] = acc_lo[m % 2] + w * lo
            acc_hi[m % 2] = acc_hi[m % 2] + w * hi
        lo = acc_lo[0] + acc_lo[1]
        hi = acc_hi[0] + acc_hi[1]
        yg[slot][tl, 0:HALF_ROWS, :] = lo[0:HALF_ROWS] + lo[HALF_ROWS:]
        yg[slot][tl, HALF_ROWS:, :] = hi[0:HALF_ROWS] + hi[HALF_ROWS:]

    def residual(t0, slot):
        y = jnp.concatenate([yg[slot][:, s, :] for s in range(SUBLANES)], axis=1)
        rows = pl.ds(t0, SUBLANES)
        o_ref[rows, :] = x_ref[rows, :] + gt2 * _rms(y, nw_ref[...])

    yg[STAGE_BUFS - 1][...] = jnp.zeros(yg[STAGE_BUFS - 1].shape, F32)
    _staged_tokens([e_ref, a_ref], [idx_s, wgt_s], [sem_i, sem_w], tt, tok, after_group=residual)


def _peer_v(e4, act, x1, mod, nw, tab, n, seq, tt):
    row = lambda i: (i, 0)
    const = lambda i: (0, 0)
    assert seq % tt == 0
    return pl.pallas_call(
        functools.partial(_peer_v_kernel, tt=tt, seq=seq),
        grid=(n // tt,),
        in_specs=[
            pl.BlockSpec((tt, PEER_E), row),
            pl.BlockSpec((tt, PEER_E), row),
            pl.BlockSpec((tt, D_MODEL), row),
            pl.BlockSpec(mod.shape, const),
            pl.BlockSpec((1, D_MODEL), const),
            pl.BlockSpec(tab.shape, const, pipeline_mode=pl.Buffered(1)),
        ],
        out_specs=pl.BlockSpec((tt, D_MODEL), row),
        out_shape=jax.ShapeDtypeStruct((n, D_MODEL), F32),
        scratch_shapes=[pltpu.VMEM((STAGE_TOKENS, SUBLANES, LANES), F32) for _ in range(STAGE_BUFS)]
        + _stage_scratch(I32) + _stage_scratch(F32),
        compiler_params=pltpu.CompilerParams(
            dimension_semantics=("arbitrary",), vmem_limit_bytes=VMEM_TABLE_LIMIT),
        name="peer_v",
    )(e4, act, x1, mod, nw, tab)


def _pack_kernel(x_ref, o_ref):
    half = D_MODEL // 2
    bits = pltpu.bitcast(x_ref[...].astype(BF16).astype(F32), U32)
    w = (bits[:, :half] >> 16) | bits[:, half:]
    for s in range(HALF_ROWS):
        o_ref[pl.ds(s, x_ref.shape[0], stride=HALF_ROWS), :] = w[:, s * LANES:(s + 1) * LANES]


def _pack_table(tab):
    e, d = tab.shape
    te = 2048
    assert d == D_MODEL and e % te == 0
    return pl.pallas_call(
        _pack_kernel,
        grid=(e // te,),
        in_specs=[pl.BlockSpec((te, d), lambda i: (i, 0))],
        out_specs=pl.BlockSpec((te * HALF_ROWS, LANES), lambda i: (i, 0)),
        out_shape=jax.ShapeDtypeStruct((e * HALF_ROWS, LANES), U32),
        compiler_params=pltpu.CompilerParams(dimension_semantics=("arbitrary",), vmem_limit_bytes=VMEM_MIXER_LIMIT),
        name="pack",
    )(tab)


def _tile(seq, cap):
    t = min(seq, cap)
    assert seq % t == 0
    return t


def kernel(x, c, w_ada, b_ada, norm_pre_mix, norm_post_mix, w_in_mix, sc_conv_w, dn_conv_w, dn_a_log,
           dn_dt_bias, dn_out_norm, w_sc_out, w_dn_out, w_mix_out, norm_pre_ffn, norm_post_ffn, peer_w_q,
           peer_sub_keys, peer_u, peer_v):
    batch, seq, d = x.shape
    assert d == D_MODEL and batch <= SUBLANES and seq % DN_CHUNK == 0
    depth = w_ada.shape[0]
    n = batch * seq
    x2 = x.reshape(n, d)
    c_pad = jnp.pad(c, ((0, SUBLANES - batch), (0, 0)))
    tab_u = _pack_table(peer_u)
    tab_v = _pack_table(peer_v)
    rep = lambda v: jnp.repeat(v, DN_DK, axis=-1)
    for l in range(depth):
        mod = _ada(c_pad, w_ada[l], b_ada[l][None, :])
        w = w_in_mix[l]
        o_z = C_SC + C_QKV
        o_a = o_z + C_Z
        w_pad = jnp.concatenate(
            [w[:, :o_a], rep(w[:, o_a:o_a + DN_HEADS]), rep(w[:, o_a + DN_HEADS:o_a + 2 * DN_HEADS]),
             w[:, o_a + 2 * DN_HEADS:]], axis=1).astype(BF16)
        tm = _tile(seq, 512)
        parts = _inproj(x2, mod, norm_pre_mix[l][None, :], w_pad, batch, seq, tm)
        weights = [
            sc_conv_w[l], dn_conv_w[l], rep(dn_a_log[l])[None, :], rep(dn_dt_bias[l])[None, :],
            jnp.tile(dn_out_norm[l], DN_HEADS)[None, :], w_sc_out[l].astype(BF16), w_dn_out[l].astype(BF16),
            w_mix_out[l].astype(BF16), norm_post_mix[l][None, :],
        ]
        x1 = _mixer(parts, x2, mod, weights, batch, seq, _tile(seq, 512))
        h2, q = _qproj(x1, mod, norm_pre_ffn[l][None, :], peer_w_q[l].astype(BF16), batch, seq, tm)
        keys = peer_sub_keys[l].reshape(2 * PEER_HEADS, PEER_NKEYS, PEER_HALF)
        e4, gates = _topk(q, keys, n, _tile(n, LANES))
        act = _peer_u(e4, h2.reshape(n, SUBLANES, LANES), gates, tab_u, n, _tile(seq, 256))
        x2 = _peer_v(e4, act, x1, mod, norm_post_ffn[l][None, :], tab_v, n, seq, _tile(seq, 128))
    return x2.reshape(batch, seq, d)
```

```python
import functools

import jax
import jax.numpy as jnp
from jax import lax
from jax.experimental import pallas as pl
from jax.experimental.pallas import tpu as pltpu

F32 = jnp.float32
BF16 = jnp.bfloat16
I32 = jnp.int32
U32 = jnp.uint32

D_MODEL = 1024
SC_WIDTH = 512
SC_KERNEL = 3
DN_HEADS = 8
DN_DK = 64
DN_KEY = DN_HEADS * DN_DK
DN_VAL = DN_KEY
DN_CONV = 4
DN_CHUNK = 64
PEER_HEADS = 8
PEER_NKEYS = 128
PEER_HALF = 128
PEER_TOPK = 16
PEER_E = PEER_HEADS * PEER_TOPK
NORM_EPS = 1e-6

LANES = 128
SUBLANES = 8
HALO = SUBLANES
PAIR = 2 * DN_DK
N_PAIRS = DN_HEADS // 2
INV_TERMS = 2
PREP_CHUNKS = 2
HALF_ROWS = SUBLANES // 2
STAGE_TOKENS = 8
STAGE_BUFS = 2
VMEM_TABLE_LIMIT = 56 * 1024 * 1024
VMEM_MIXER_LIMIT = 56 * 1024 * 1024

C_SC = 3 * SC_WIDTH
C_QKV = 2 * DN_KEY + DN_VAL
C_Z = DN_VAL
C_A = DN_KEY
C_B = DN_KEY
C_G = 2 * D_MODEL
C_ALL = C_SC + C_QKV + C_Z + C_A + C_B + C_G


def _silu(x):
    return x * jax.nn.sigmoid(x)


def _softplus(x):
    return jnp.maximum(x, 0.0) + jnp.log(1.0 + jnp.exp(-jnp.abs(x)))


def _gelu(x):
    return 0.5 * x * (1.0 + lax.erf(x * (2.0 ** -0.5)))


def _rms(x, w):
    return x * lax.rsqrt(jnp.mean(x * x, axis=-1, keepdims=True) + NORM_EPS) * w


def _bdot(a, b):
    return jnp.dot(a.astype(BF16), b.astype(BF16), preferred_element_type=F32)


def _split(x, n):
    terms = []
    for i in range(n):
        t = x.astype(BF16)
        terms.append(t)
        if i + 1 < n:
            x = x - t.astype(F32)
    return terms


NN_DIMS = (((1,), (0,)), ((), ()))
NT_DIMS = (((1,), (1,)), ((), ()))


def _sdot(a_terms, b_terms, dims=NN_DIMS):
    order = max(len(a_terms), len(b_terms))
    out = None
    for i, a in enumerate(a_terms):
        for j, b in enumerate(b_terms):
            if i + j < order:
                p = lax.dot_general(a, b, dims, preferred_element_type=F32)
                out = p if out is None else out + p
    return out


def _ada_kernel(c_ref, w_ref, b_ref, o_ref):
    o_ref[...] = _bdot(_silu(c_ref[...]), w_ref[...]) + b_ref[...]


def _ada(c_pad, w_ada, b_ada):
    n_out = w_ada.shape[1]
    tn = 1024
    return pl.pallas_call(
        _ada_kernel,
        grid=(n_out // tn,),
        in_specs=[
            pl.BlockSpec((SUBLANES, D_MODEL), lambda j: (0, 0)),
            pl.BlockSpec((D_MODEL, tn), lambda j: (0, j)),
            pl.BlockSpec((1, tn), lambda j: (0, j)),
        ],
        out_specs=pl.BlockSpec((SUBLANES, tn), lambda j: (0, j)),
        out_shape=jax.ShapeDtypeStruct((SUBLANES, n_out), F32),
        name="ada",
    )(c_pad, w_ada, b_ada)


def _inproj_kernel(x_ref, mod_ref, nw_ref, w_ref, sc_ref, qkv_ref, z_ref, a_ref, b_ref, g_ref):
    b = pl.program_id(0)
    mod = mod_ref[pl.ds(b, 1), :]
    sh1 = mod[:, 0:D_MODEL]
    sc1 = mod[:, D_MODEL:2 * D_MODEL]
    h = (_rms(x_ref[...], nw_ref[...]) * (1.0 + sc1) + sh1).astype(BF16)
    col = 0
    for ref, width in ((sc_ref, C_SC), (qkv_ref, C_QKV), (z_ref, C_Z), (a_ref, C_A), (b_ref, C_B), (g_ref, C_G)):
        for j in range(0, width, 512):
            ref[:, j:j + 512] = jnp.dot(h, w_ref[:, col + j:col + j + 512],
                                        preferred_element_type=F32).astype(ref.dtype)
        col += width


def _inproj(x2, mod, nw, w_pad, batch, seq, tm):
    nt = seq // tm
    row = lambda b, t: (b * nt + t, 0)
    const = lambda b, t: (0, 0)
    n = batch * seq
    outs = [(C_SC, BF16), (C_QKV, BF16), (C_Z, BF16), (C_A, F32), (C_B, F32), (C_G, BF16)]
    return pl.pallas_call(
        _inproj_kernel,
        grid=(batch, nt),
        in_specs=[
            pl.BlockSpec((tm, D_MODEL), row),
            pl.BlockSpec(mod.shape, const),
            pl.BlockSpec((1, D_MODEL), const),
            pl.BlockSpec((D_MODEL, C_ALL), const, pipeline_mode=pl.Buffered(1)),
        ],
        out_specs=[pl.BlockSpec((tm, w), row) for w, _ in outs],
        out_shape=[jax.ShapeDtypeStruct((n, w), dt) for w, dt in outs],
        compiler_params=pltpu.CompilerParams(
            dimension_semantics=("arbitrary", "arbitrary"), vmem_limit_bytes=VMEM_MIXER_LIMIT),
        name="inproj",
    )(x2, mod, nw, w_pad)


def _pair_consts():
    r = lax.broadcasted_iota(I32, (PAIR, PAIR), 0)
    c = lax.broadcasted_iota(I32, (PAIR, PAIR), 1)
    bd = ((r >> 6) == (c >> 6)).astype(F32)
    i = lax.broadcasted_iota(I32, (DN_CHUNK, PAIR), 0)
    j = lax.broadcasted_iota(I32, (DN_CHUNK, PAIR), 1) & (DN_DK - 1)
    return bd, i >= j, i > j, (i == j).astype(F32)


def _stack_bd(y, bd):
    return jnp.concatenate([y, y], axis=0) * bd


def _mixer_kernel(sc_ref, qkv_ref, z_ref, a_ref, b_ref, g_ref, x_ref, mod_ref,
                  scw_ref, dnw_ref, alog_ref, dtb_ref, onw_ref, wsc_ref, wdn_ref, wmix_ref, pnw_ref,
                  o_ref,
                  scx_s, qkvx_s, q_s, k_s, v_s, g_s, beta_s, o_s, gcum_s, aqk_s, u_s, wk_s, qg_s, kdec_s,
                  state_s, *, tt):
    b = pl.program_id(0)
    t = pl.program_id(1)

    @pl.when(t == 0)
    def _():
        scx_s[0:HALO, :] = jnp.zeros((HALO, SC_WIDTH), F32)
        qkvx_s[0:HALO, :] = jnp.zeros((HALO, C_QKV), F32)
        state_s[...] = jnp.zeros(state_s.shape, F32)

    sc = sc_ref[...].astype(F32)
    scx_s[HALO:HALO + tt, :] = sc[:, SC_WIDTH:2 * SC_WIDTH] * sc[:, 2 * SC_WIDTH:]
    conv = scw_ref[0:1, :] * scx_s[pl.ds(HALO - (SC_KERNEL - 1), tt), :]
    for kk in range(1, SC_KERNEL):
        conv = conv + scw_ref[kk:kk + 1, :] * scx_s[pl.ds(HALO - (SC_KERNEL - 1) + kk, tt), :]
    y_sc = _bdot(sc[:, 0:SC_WIDTH] * conv, wsc_ref[...])
    scx_s[0:HALO, :] = scx_s[tt:tt + HALO, :]

    qkvx_s[HALO:HALO + tt, :] = qkv_ref[...].astype(F32)
    cq = dnw_ref[0:1, :] * qkvx_s[pl.ds(HALO - (DN_CONV - 1), tt), :]
    for kk in range(1, DN_CONV):
        cq = cq + dnw_ref[kk:kk + 1, :] * qkvx_s[pl.ds(HALO - (DN_CONV - 1) + kk, tt), :]
    qkvx_s[0:HALO, :] = qkvx_s[tt:tt + HALO, :]
    cq = _silu(cq)
    r512 = lax.broadcasted_iota(I32, (DN_KEY, DN_KEY), 0) >> 6
    c512 = lax.broadcasted_iota(I32, (DN_KEY, DN_KEY), 1) >> 6
    head_ones = [(r512 == c512).astype(BF16)]
    q = cq[:, 0:DN_KEY]
    k = cq[:, DN_KEY:2 * DN_KEY]
    q_s[...] = q * lax.rsqrt(_sdot(_split(q * q, 2), head_ones) + NORM_EPS) * (DN_DK ** -0.5)
    k_s[...] = k * lax.rsqrt(_sdot(_split(k * k, 2), head_ones) + NORM_EPS)
    v_s[...] = cq[:, 2 * DN_KEY:]
    beta_s[...] = jax.nn.sigmoid(b_ref[...])
    g_s[...] = -jnp.exp(alog_ref[...]) * _softplus(a_ref[...] + dtb_ref[...])

    bd, incl, strict, eye = _pair_consts()
    bd_b = bd.astype(BF16)
    ri = lax.broadcasted_iota(I32, (DN_CHUNK, DN_CHUNK), 0)
    ci = lax.broadcasted_iota(I32, (DN_CHUNK, DN_CHUNK), 1)
    lt = [(ci <= ri).astype(BF16)]
    nt = (((1,), (1,)), ((), ()))

    def stack_terms(y):
        return [jnp.concatenate([t_, t_], axis=0) * bd_b for t_ in _split(y, INV_TERMS)]

    pairs = [slice(p * PAIR, (p + 1) * PAIR) for p in range(N_PAIRS)]

    def prepare(c, carry):
        prob = []
        for cc in range(PREP_CHUNKS):
            rows = pl.ds(pl.multiple_of((c * PREP_CHUNKS + cc) * DN_CHUNK, DN_CHUNK), DN_CHUNK)
            gcum_all = _sdot(lt, _split(g_s[rows, :], 3))
            gcum_s[rows, :] = gcum_all
            prob += [(rows, sl, gcum_all[:, sl]) for sl in pairs]
        pws, invs, rest = [], [], []
        for rows, sl, gc in prob:
            qp = q_s[rows, sl]
            kp = k_s[rows, sl]
            bp = beta_s[rows, sl]
            grow = jnp.sum(gc * eye, axis=0, keepdims=True)
            dec = jnp.where(incl, jnp.exp(jnp.where(incl, gc - grow, 0.0)), 0.0)
            eg = jnp.exp(gc)
            kb = kp * bp
            kbig = _stack_bd(kp, bd).astype(BF16)
            kk_s = lax.dot_general(kb.astype(BF16), kbig, nt, preferred_element_type=F32)
            qk_s = lax.dot_general(qp.astype(BF16), kbig, nt, preferred_element_type=F32)
            aqk_s[rows, sl] = qk_s * dec
            qg_s[rows, sl] = qp * eg
            kdec_s[rows, sl] = kp * jnp.exp(gc[DN_CHUNK - 1:DN_CHUNK, :] - gc)
            pw = -(kk_s * jnp.where(strict, dec, 0.0))
            pws.append(pw)
            invs.append(eye + pw)
            rest.append((v_s[rows, sl] * bp, kb * eg))
        for _ in range(5):
            pws = [_sdot(_split(pw, INV_TERMS), stack_terms(pw)) for pw in pws]
            invs = [inv + _sdot(_split(inv, INV_TERMS), stack_terms(pw)) for inv, pw in zip(invs, pws)]
        for (rows, sl, _), inv, (vb, kbg) in zip(prob, invs, rest):
            u_s[rows, sl] = _bdot(inv, _stack_bd(vb, bd))
            wk_s[rows, sl] = _bdot(inv, _stack_bd(kbg, bd))
        return carry

    def advance(c, carry):
        rows = pl.ds(pl.multiple_of(c * DN_CHUNK, DN_CHUNK), DN_CHUNK)
        tail = pl.ds(pl.multiple_of(c * DN_CHUNK + DN_CHUNK - SUBLANES, SUBLANES), SUBLANES)
        sts = [state_s[p] for p in range(N_PAIRS)]
        wss = [_bdot(jnp.concatenate([wk_s[rows, sl], qg_s[rows, sl]], axis=0), st) for sl, st in zip(pairs, sts)]
        vnews = [u_s[rows, sl] - ws[0:DN_CHUNK] for sl, ws in zip(pairs, wss)]
        for p, (sl, st, ws, vnew) in enumerate(zip(pairs, sts, wss, vnews)):
            o_s[rows, sl] = ws[DN_CHUNK:] + _bdot(aqk_s[rows, sl], _stack_bd(vnew, bd))
            glast = gcum_s[tail, sl][SUBLANES - 1:SUBLANES, :]
            state_s[p] = st * jnp.exp(glast) + bd * _bdot(kdec_s[rows, sl].T, vnew)
        return carry

    lax.fori_loop(0, tt // (DN_CHUNK * PREP_CHUNKS), prepare, 0)
    lax.fori_loop(0, tt // DN_CHUNK, advance, 0)

    o = o_s[...]
    ms = _sdot(_split(o * o, 2), head_ones) * (1.0 / DN_DK)
    og = o * lax.rsqrt(ms + NORM_EPS) * onw_ref[...] * _silu(z_ref[...].astype(F32))
    y_dn = _bdot(og, wdn_ref[...])
    gates = g_ref[...].astype(F32)
    merged = jax.nn.sigmoid(gates[:, 0:D_MODEL]) * y_sc + jax.nn.sigmoid(gates[:, D_MODEL:]) * y_dn
    y = _bdot(merged, wmix_ref[...])
    gt1 = mod_ref[pl.ds(b, 1), :][:, 2 * D_MODEL:3 * D_MODEL]
    o_ref[...] = x_ref[...] + gt1 * _rms(y, pnw_ref[...])


def _mixer(parts, x2, mod, weights, batch, seq, tt):
    nt = seq // tt
    row = lambda b, t: (b * nt + t, 0)
    const2 = lambda b, t: (0, 0)
    sc, qkv, z, a, bb, g = parts
    data = [(sc, C_SC), (qkv, C_QKV), (z, C_Z), (a, C_A), (bb, C_B), (g, C_G), (x2, D_MODEL)]
    in_specs = [pl.BlockSpec((tt, w), row) for _, w in data]
    in_specs.append(pl.BlockSpec(mod.shape, const2))
    in_specs += [pl.BlockSpec(wt.shape, const2) for wt in weights]
    scratch = [
        pltpu.VMEM((tt + HALO, SC_WIDTH), F32),
        pltpu.VMEM((tt + HALO, C_QKV), F32),
        pltpu.VMEM((tt, DN_KEY), F32),
        pltpu.VMEM((tt, DN_KEY), F32),
        pltpu.VMEM((tt, DN_VAL), F32),
        pltpu.VMEM((tt, DN_KEY), F32),
        pltpu.VMEM((tt, DN_KEY), F32),
        pltpu.VMEM((tt, DN_VAL), F32),
    ] + [pltpu.VMEM((tt, DN_KEY), F32) for _ in range(6)] + [
        pltpu.VMEM((N_PAIRS, PAIR, PAIR), F32),
    ]
    return pl.pallas_call(
        functools.partial(_mixer_kernel, tt=tt),
        grid=(batch, nt),
        in_specs=in_specs,
        out_specs=pl.BlockSpec((tt, D_MODEL), row),
        out_shape=jax.ShapeDtypeStruct((batch * seq, D_MODEL), F32),
        scratch_shapes=scratch,
        compiler_params=pltpu.CompilerParams(
            dimension_semantics=("arbitrary", "arbitrary"), vmem_limit_bytes=VMEM_MIXER_LIMIT),
        name="mixer",
    )(*[d for d, _ in data], mod, *weights)


def _sort16_network():
    n, pairs, p = 16, [], 1
    while p < n:
        k = p
        while k >= 1:
            for j in range(k % p, n - k, 2 * k):
                for i in range(min(k, n - j - k)):
                    if (i + j) // (2 * p) == (i + j + k) // (2 * p):
                        pairs.append((i + j, i + j + k))
            k //= 2
        p *= 2
    return pairs


def _top16_of_128(st, v_out, i_out):
    k = PEER_TOPK
    vs = [st[SUBLANES * i:SUBLANES * (i + 1), :] for i in range(k)]
    sub = lax.broadcasted_iota(I32, vs[0].shape, 0).astype(F32)
    ix = [sub + float(SUBLANES * i) for i in range(k)]
    for a, b in _sort16_network():
        swap = (vs[b] > vs[a]) | ((vs[b] == vs[a]) & (ix[b] < ix[a]))
        vs[a], vs[b] = jnp.where(swap, vs[b], vs[a]), jnp.where(swap, vs[a], vs[b])
        ix[a], ix[b] = jnp.where(swap, ix[b], ix[a]), jnp.where(swap, ix[a], ix[b])
    for it in range(k):
        m = jnp.max(vs[0], axis=0, keepdims=True)
        pos = jnp.min(jnp.where(vs[0] == m, ix[0], float(PEER_NKEYS)), axis=0, keepdims=True)
        v_out[it:it + 1, :] = m
        i_out[it:it + 1, :] = pos
        hit = ix[0] == pos
        depth = k - 1 - it
        for d in range(depth):
            vs[d] = jnp.where(hit, vs[d + 1], vs[d])
            ix[d] = jnp.where(hit, ix[d + 1], ix[d])


def _top16_of_pairs(v1, i1, v2, i2, v_out, e_out, row0):
    k = PEER_TOPK
    rows = lax.broadcasted_iota(I32, v1.shape, 0).astype(F32)
    cand = v1 + v2[0:1, :]
    taken = jnp.zeros_like(v1)
    for it in range(k):
        m = jnp.max(cand, axis=0, keepdims=True)
        a_win = jnp.min(jnp.where(cand == m, rows, float(k)), axis=0, keepdims=True)
        hit = rows == a_win
        b_win = jnp.sum(jnp.where(hit, taken, 0.0), axis=0, keepdims=True)
        i1_win = jnp.sum(jnp.where(hit, i1, 0.0), axis=0, keepdims=True)
        i2_win = jnp.sum(jnp.where(rows == b_win, i2, 0.0), axis=0, keepdims=True)
        v_out[row0 + it:row0 + it + 1, :] = m
        e_out[row0 + it:row0 + it + 1, :] = i1_win * float(PEER_NKEYS) + i2_win
        if it + 1 < k:
            v1_win = jnp.sum(jnp.where(hit, v1, 0.0), axis=0, keepdims=True)
            nxt = rows == b_win + 1.0
            v2_nxt = jnp.sum(jnp.where(nxt, v2, 0.0), axis=0, keepdims=True)
            new = jnp.where(b_win + 1.0 < float(k), v1_win + v2_nxt, -jnp.inf)
            cand = jnp.where(hit, new, cand)
            taken = jnp.where(hit, b_win + 1.0, taken)


def _topk_kernel(x_ref, mod_ref, nw_ref, w_ref, keys_ref, h_ref, e_ref, g_ref,
                 q_ref, v1_s, i1_s, v2_s, i2_s, ts_s, ei_s, gate_s, *, tt, seq):
    b = (pl.program_id(0) * tt) // seq
    mod = mod_ref[pl.ds(b, 1), :]
    sh2 = mod[:, 3 * D_MODEL:4 * D_MODEL]
    sc2 = mod[:, 4 * D_MODEL:5 * D_MODEL]
    hh = _rms(x_ref[...], nw_ref[...]) * (1.0 + sc2) + sh2
    h_ref[...] = hh
    q_ref[...] = _bdot(hh, w_ref[...])
    k = PEER_TOPK
    for h in range(PEER_HEADS):
        for p, (v_s, i_s) in enumerate(((v1_s, i1_s), (v2_s, i2_s))):
            col = (2 * h + p) * PEER_HALF
            st = _sdot(_split(keys_ref[2 * h + p], 2), _split(q_ref[:, col:col + PEER_HALF], 2), NT_DIMS)
            _top16_of_128(st, v_s, i_s)
        _top16_of_pairs(v1_s[...], i1_s[...], v2_s[...], i2_s[...], ts_s, ei_s, h * k)
        ts = ts_s[h * k:(h + 1) * k, :]
        e = jnp.exp(ts - jnp.max(ts, axis=0, keepdims=True))
        gate_s[h * k:(h + 1) * k, :] = e / jnp.sum(e, axis=0, keepdims=True)
    e_ref[...] = ei_s[...].T.astype(I32) * HALF_ROWS
    g_ref[...] = gate_s[...].T


def _topk(x1, mod, nw, wq, keys, n, seq, tt):
    row = lambda i: (i, 0)
    const = lambda i: (0, 0)
    assert seq % tt == 0
    return pl.pallas_call(
        functools.partial(_topk_kernel, tt=tt, seq=seq),
        grid=(n // tt,),
        in_specs=[
            pl.BlockSpec((tt, D_MODEL), row),
            pl.BlockSpec(mod.shape, const),
            pl.BlockSpec((1, D_MODEL), const),
            pl.BlockSpec(wq.shape, const),
            pl.BlockSpec(keys.shape, lambda i: (0, 0, 0)),
        ],
        out_specs=[pl.BlockSpec((tt, D_MODEL), row)] + [pl.BlockSpec((tt, PEER_E), row)] * 2,
        out_shape=[jax.ShapeDtypeStruct((n, D_MODEL), F32), jax.ShapeDtypeStruct((n, PEER_E), I32),
                   jax.ShapeDtypeStruct((n, PEER_E), F32)],
        scratch_shapes=[
            pltpu.VMEM((tt, wq.shape[1]), F32),
            pltpu.VMEM((PEER_TOPK, tt), F32), pltpu.VMEM((PEER_TOPK, tt), F32),
            pltpu.VMEM((PEER_TOPK, tt), F32), pltpu.VMEM((PEER_TOPK, tt), F32),
            pltpu.VMEM((PEER_E, tt), F32), pltpu.VMEM((PEER_E, tt), F32),
            pltpu.VMEM((PEER_E, tt), F32),
        ],
        compiler_params=pltpu.CompilerParams(dimension_semantics=("arbitrary",)),
        name="topk",
    )(x1, mod, nw, wq, keys)


def _staged_tokens(srcs, bufs, sems, tt, body, after_group=None):
    n_groups = tt // STAGE_TOKENS
    nb = STAGE_BUFS
    assert n_groups % nb == 0 and STAGE_TOKENS % 2 == 0

    def copies(grp, slot):
        return [pltpu.make_async_copy(src.at[pl.ds(grp * STAGE_TOKENS, STAGE_TOKENS)], buf.at[slot], sem.at[slot])
                for src, buf, sem in zip(srcs, bufs, sems)]

    for s in range(nb - 1):
        for c in copies(s, s):
            c.start()

    def round_of_groups(i, carry):
        for slot in range(nb):
            grp = nb * i + slot
            for c in copies(jnp.minimum(grp + nb - 1, n_groups - 1), (slot + nb - 1) % nb):
                c.start()
            for c in copies(grp, slot):
                c.wait()
            if after_group is not None:
                after_group(pl.multiple_of(jnp.maximum(grp - 1, 0) * STAGE_TOKENS, STAGE_TOKENS), (slot - 1) % nb)
            for tl in range(STAGE_TOKENS):
                body(grp * STAGE_TOKENS + tl, tl, slot,
                     *[lambda k, buf=buf, slot=slot, tl=tl: buf[slot, tl, k] for buf in bufs])
        return carry

    lax.fori_loop(0, n_groups // nb, round_of_groups, 0)
    for s in range(nb - 1):
        for c in copies(n_groups - 1, (n_groups + s) % nb):
            c.wait()
    if after_group is not None:
        after_group((n_groups - 1) * STAGE_TOKENS, (n_groups - 1) % nb)


def _stage_scratch(dtype):
    return [pltpu.SMEM((STAGE_BUFS, STAGE_TOKENS, PEER_E), dtype), pltpu.SemaphoreType.DMA((STAGE_BUFS,))]


def _pair_rows(tab_ref, idx, m):
    wa = tab_ref[pl.ds(pl.multiple_of(idx(2 * m), HALF_ROWS), HALF_ROWS), :]
    wb = tab_ref[pl.ds(pl.multiple_of(idx(2 * m + 1), HALF_ROWS), HALF_ROWS), :]
    return jnp.concatenate([wa, wb], axis=0)


def _unpack(w):
    return pltpu.bitcast(w << 16, F32), pltpu.bitcast(w & jnp.uint32(0xFFFF0000), F32)


def _peer_u_kernel(e_ref, h_ref, g_ref, tab_ref, o_ref, prod_a, prod_b, idx_s, sem, *, tt):
    prod = (prod_a, prod_b)

    def finish(t, buf):
        part = buf[pl.ds(0, PEER_E, stride=HALF_ROWS), :]
        for s in range(1, HALF_ROWS):
            part = part + buf[pl.ds(s, PEER_E, stride=HALF_ROWS), :]
        act = jnp.sum(part.T, axis=0, keepdims=True)
        o_ref[pl.ds(t, 1), :] = g_ref[pl.ds(t, 1), :] * _gelu(act)

    prod_b[...] = jnp.zeros(prod_b.shape, F32)

    def tok(t, tl, _, idx):
        finish(jnp.maximum(t - 1, 0), prod[(tl + 1) & 1])
        hv = h_ref[t]
        h_lo = jnp.concatenate([hv[0:HALF_ROWS], hv[0:HALF_ROWS]], axis=0)
        h_hi = jnp.concatenate([hv[HALF_ROWS:], hv[HALF_ROWS:]], axis=0)
        for m in range(PEER_E // 2):
            lo, hi = _unpack(_pair_rows(tab_ref, idx, m))
            prod[tl & 1][SUBLANES * m:SUBLANES * (m + 1), :] = lo * h_lo + hi * h_hi

    _staged_tokens([e_ref], [idx_s], [sem], tt, tok)
    finish(tt - 1, prod[(tt - 1) & 1])


def _peer_u(e4, h3, gates, tab, n, tt):
    row = lambda i: (i, 0)
    return pl.pallas_call(
        functools.partial(_peer_u_kernel, tt=tt),
        grid=(n // tt,),
        in_specs=[
            pl.BlockSpec((tt, PEER_E), row),
            pl.BlockSpec((tt, SUBLANES, LANES), lambda i: (i, 0, 0)),
            pl.BlockSpec((tt, PEER_E), row),
            pl.BlockSpec(tab.shape, lambda i: (0, 0), pipeline_mode=pl.Buffered(1)),
        ],
        out_specs=pl.BlockSpec((tt, PEER_E), row),
        out_shape=jax.ShapeDtypeStruct((n, PEER_E), F32),
        scratch_shapes=[pltpu.VMEM((PEER_E * HALF_ROWS, LANES), F32) for _ in range(2)] + _stage_scratch(I32),
        compiler_params=pltpu.CompilerParams(
            dimension_semantics=("arbitrary",), vmem_limit_bytes=VMEM_TABLE_LIMIT),
        name="peer_u",
    )(e4, h3, gates, tab)


def _peer_v_kernel(e_ref, a_ref, x_ref, mod_ref, nw_ref, tab_ref, o_ref, yg_a, yg_b, idx_s, sem_i, wgt_s, sem_w,
                   *, tt, seq):
    assert STAGE_TOKENS == SUBLANES and STAGE_BUFS == 2
    yg = (yg_a, yg_b)
    top = lax.broadcasted_iota(I32, (SUBLANES, LANES), 0) < HALF_ROWS
    b = (pl.program_id(0) * tt) // seq
    gt2 = mod_ref[pl.ds(b, 1), :][:, 5 * D_MODEL:6 * D_MODEL]

    def tok(t, tl, slot, idx, wgt):
        acc_lo = [jnp.zeros((SUBLANES, LANES), F32) for _ in range(2)]
        acc_hi = [jnp.zeros((SUBLANES, LANES), F32) for _ in range(2)]
        for m in range(PEER_E // 2):
            lo, hi = _unpack(_pair_rows(tab_ref, idx, m))
            w = jnp.where(top, wgt(2 * m), wgt(2 * m + 1))
            acc_lo[m % 2] = acc_lo[m % 2] + w * lo
            acc_hi[m % 2] = acc_hi[m % 2] + w * hi
        lo = acc_lo[0] + acc_lo[1]
        hi = acc_hi[0] + acc_hi[1]
        yg[slot][tl, 0:HALF_ROWS, :] = lo[0:HALF_ROWS] + lo[HALF_ROWS:]
        yg[slot][tl, HALF_ROWS:, :] = hi[0:HALF_ROWS] + hi[HALF_ROWS:]

    def residual(t0, slot):
        y = jnp.concatenate([yg[slot][:, s, :] for s in range(SUBLANES)], axis=1)
        rows = pl.ds(t0, SUBLANES)
        o_ref[rows, :] = x_ref[rows, :] + gt2 * _rms(y, nw_ref[...])

    yg[STAGE_BUFS - 1][...] = jnp.zeros(yg[STAGE_BUFS - 1].shape, F32)
    _staged_tokens([e_ref, a_ref], [idx_s, wgt_s], [sem_i, sem_w], tt, tok, after_group=residual)


def _peer_v(e4, act, x1, mod, nw, tab, n, seq, tt):
    row = lambda i: (i, 0)
    const = lambda i: (0, 0)
    assert seq % tt == 0
    return pl.pallas_call(
        functools.partial(_peer_v_kernel, tt=tt, seq=seq),
        grid=(n // tt,),
        in_specs=[
            pl.BlockSpec((tt, PEER_E), row),
            pl.BlockSpec((tt, PEER_E), row),
            pl.BlockSpec((tt, D_MODEL), row),
            pl.BlockSpec(mod.shape, const),
            pl.BlockSpec((1, D_MODEL), const),
            pl.BlockSpec(tab.shape, const, pipeline_mode=pl.Buffered(1)),
        ],
        out_specs=pl.BlockSpec((tt, D_MODEL), row),
        out_shape=jax.ShapeDtypeStruct((n, D_MODEL), F32),
        scratch_shapes=[pltpu.VMEM((STAGE_TOKENS, SUBLANES, LANES), F32) for _ in range(STAGE_BUFS)]
        + _stage_scratch(I32) + _stage_scratch(F32),
        compiler_params=pltpu.CompilerParams(
            dimension_semantics=("arbitrary",), vmem_limit_bytes=VMEM_TABLE_LIMIT),
        name="peer_v",
    )(e4, act, x1, mod, nw, tab)


def _pack_kernel(x_ref, o_ref):
    half = D_MODEL // 2
    bits = pltpu.bitcast(x_ref[...].astype(BF16).astype(F32), U32)
    w = (bits[:, :half] >> 16) | bits[:, half:]
    for s in range(HALF_ROWS):
        o_ref[pl.ds(s, x_ref.shape[0], stride=HALF_ROWS), :] = w[:, s * LANES:(s + 1) * LANES]


def _pack_table(tab):
    e, d = tab.shape
    te = 2048
    assert d == D_MODEL and e % te == 0
    return pl.pallas_call(
        _pack_kernel,
        grid=(e // te,),
        in_specs=[pl.BlockSpec((te, d), lambda i: (i, 0))],
        out_specs=pl.BlockSpec((te * HALF_ROWS, LANES), lambda i: (i, 0)),
        out_shape=jax.ShapeDtypeStruct((e * HALF_ROWS, LANES), U32),
        compiler_params=pltpu.CompilerParams(dimension_semantics=("arbitrary",), vmem_limit_bytes=VMEM_MIXER_LIMIT),
        name="pack",
    )(tab)


def _tile(seq, cap):
    t = min(seq, cap)
    assert seq % t == 0
    return t


def kernel(x, c, w_ada, b_ada, norm_pre_mix, norm_post_mix, w_in_mix, sc_conv_w, dn_conv_w, dn_a_log,
           dn_dt_bias, dn_out_norm, w_sc_out, w_dn_out, w_mix_out, norm_pre_ffn, norm_post_ffn, peer_w_q,
           peer_sub_keys, peer_u, peer_v):
    batch, seq, d = x.shape
    assert d == D_MODEL and batch <= SUBLANES and seq % DN_CHUNK == 0
    depth = w_ada.shape[0]
    n = batch * seq
    x2 = x.reshape(n, d)
    c_pad = jnp.pad(c, ((0, SUBLANES - batch), (0, 0)))
    tab_u = _pack_table(peer_u)
    tab_v = _pack_table(peer_v)
    rep = lambda v: jnp.repeat(v, DN_DK, axis=-1)
    for l in range(depth):
        mod = _ada(c_pad, w_ada[l], b_ada[l][None, :])
        w = w_in_mix[l]
        o_z = C_SC + C_QKV
        o_a = o_z + C_Z
        w_pad = jnp.concatenate(
            [w[:, :o_a], rep(w[:, o_a:o_a + DN_HEADS]), rep(w[:, o_a + DN_HEADS:o_a + 2 * DN_HEADS]),
             w[:, o_a + 2 * DN_HEADS:]], axis=1).astype(BF16)
        tm = _tile(seq, 512)
        parts = _inproj(x2, mod, norm_pre_mix[l][None, :], w_pad, batch, seq, tm)
        weights = [
            sc_conv_w[l], dn_conv_w[l], rep(dn_a_log[l])[None, :], rep(dn_dt_bias[l])[None, :],
            jnp.tile(dn_out_norm[l], DN_HEADS)[None, :], w_sc_out[l].astype(BF16), w_dn_out[l].astype(BF16),
            w_mix_out[l].astype(BF16), norm_post_mix[l][None, :],
        ]
        x1 = _mixer(parts, x2, mod, weights, batch, seq, _tile(seq, 512))
        keys = peer_sub_keys[l].reshape(2 * PEER_HEADS, PEER_NKEYS, PEER_HALF)
        h2, e4, gates = _topk(x1, mod, norm_pre_ffn[l][None, :], peer_w_q[l].astype(BF16), keys, n, seq,
                              _tile(seq, LANES))
        act = _peer_u(e4, h2.reshape(n, SUBLANES, LANES), gates, tab_u, n, _tile(seq, 256))
        x2 = _peer_v(e4, act, x1, mod, norm_post_ffn[l][None, :], tab_v, n, seq, _tile(seq, 128))
    return x2.reshape(batch, seq, d)
```
